```python
import jax, jax.numpy as jnp
from jax import lax
import numpy as np

D_MODEL = 2048
BATCH = 4
SEQ = 2048
DEPTH = 1
DEC_BATCH = 128
DEC_SEQ = 1
PAST_LEN = 16384
PAGE_SIZE = 128

POOL_WINDOWS = (2, 4, 8, 16)
N_POOL_GROUPS = 4
D_POOL = D_MODEL // 2
POOL_GROUP_IN = D_POOL // N_POOL_GROUPS
POOL_GROUP_OUT = D_MODEL // N_POOL_GROUPS
POOL_BUF = 15
GLA_HEADS = 4
D_K = D_MODEL // 2
D_V = D_MODEL
DK_HEAD = D_K // GLA_HEADS
DV_HEAD = D_V // GLA_HEADS
ALPHA_RANK = 16
GATE_TAU = 16.0
GLA_CHUNK = 64
N_EXPERTS = 32
TOP_K = 4
D_FF = D_MODEL
SWIGLU_LIMIT = 7.0
SWIGLU_ALPHA = 1.702
MOE_BLOCK = 128
EPS = 1e-6
IN_SPLITS = (D_POOL, D_K, D_K, D_V, D_V, ALPHA_RANK, D_MODEL, D_MODEL)
IN_COLS = D_POOL + 2 * D_K + 2 * D_V + ALPHA_RANK + 2 * D_MODEL

kernel_name = "pool_gla_moe_adaln_step"


def _rmsnorm(x, g):
    xf = x.astype(jnp.float32)
    y = xf * lax.rsqrt(jnp.mean(xf * xf, axis=-1, keepdims=True) + EPS)
    return (y * g.astype(jnp.float32)).astype(x.dtype)


def _pool_mixer(u, buf, pos0, w_pool, pool_scale):
    B, L, _ = u.shape
    ext = jnp.concatenate([buf.astype(u.dtype), u], axis=1)
    upf = ext.astype(jnp.float32)
    cs = jnp.pad(jnp.cumsum(upf, axis=1), ((0, 0), (1, 0), (0, 0)))
    P = POOL_BUF
    pos = pos0 + jnp.arange(L)
    outs = []
    for g, w in enumerate(POOL_WINDOWS):
        lo, hi = g * POOL_GROUP_IN, (g + 1) * POOL_GROUP_IN
        s = cs[:, P + 1:P + 1 + L, lo:hi] - cs[:, P + 1 - w:P + 1 - w + L, lo:hi]
        cnt = jnp.minimum(pos + 1, w).astype(jnp.float32)[None, :, None]
        p = s / cnt - upf[:, P:, lo:hi]
        outs.append(jnp.einsum('blc,cd->bld', p, w_pool[g].astype(jnp.float32)))
    out = jnp.concatenate(outs, axis=-1) * pool_scale.astype(jnp.float32)
    return out.astype(u.dtype), ext[:, -POOL_BUF:]


def _gla(q, k, v, loga, S0):
    B, L = q.shape[:2]
    C = min(GLA_CHUNK, L)
    n = -(-L // C)
    Lp = n * C

    def to_chunks(t):
        t = jnp.pad(t.astype(jnp.float32), ((0, 0), (0, Lp - L), (0, 0), (0, 0)))
        return t.reshape(B, n, C, GLA_HEADS, -1).transpose(1, 0, 3, 2, 4)

    qc, kc, vc, gc = to_chunks(q), to_chunks(k), to_chunks(v), to_chunks(loga)
    mask = jnp.tril(jnp.ones((C, C), dtype=bool))[:, :, None]

    def step(S, inp):
        qi, ki, vi, gi = inp
        b = jnp.cumsum(gi, axis=2)
        o_inter = jnp.einsum('bhtd,bhde->bhte', qi * jnp.exp(b), S)
        diff = b[:, :, :, None, :] - b[:, :, None, :, :]
        decay = jnp.exp(jnp.where(mask, diff, -jnp.inf))
        att = jnp.einsum('bhtd,bhtsd,bhsd->bhts', qi, decay, ki)
        o = o_inter + jnp.einsum('bhts,bhse->bhte', att, vi)
        b_last = b[:, :, -1:, :]
        S_new = jnp.exp(b_last[:, :, 0, :])[..., None] * S + jnp.einsum(
            'bhsd,bhse->bhde', ki * jnp.exp(b_last - b), vi)
        return S_new, o

    S, o = lax.scan(step, S0.astype(jnp.float32), (qc, kc, vc, gc))
    o = o.transpose(1, 0, 3, 2, 4).reshape(B, Lp, GLA_HEADS, DV_HEAD)[:, :L]
    return o.astype(q.dtype), S.astype(S0.dtype)


def _moe(h, w_router, b_router, w_gu, b_gu, w_down, b_down):
    T = h.shape[0]
    logits = jnp.dot(h, w_router).astype(jnp.float32) + b_router.astype(jnp.float32)
    top_val, top_idx = lax.top_k(logits, TOP_K)
    gates = jax.nn.softmax(top_val, axis=-1)
    A = T * TOP_K
    e_flat = top_idx.reshape(-1)
    tok_flat = jnp.arange(A) // TOP_K
    order = jnp.argsort(e_flat, stable=True)
    se, stok, sg = e_flat[order], tok_flat[order], gates.reshape(-1)[order]
    counts = jnp.bincount(e_flat, length=N_EXPERTS)
    padded = (counts + MOE_BLOCK - 1) // MOE_BLOCK * MOE_BLOCK
    pad_end = jnp.cumsum(padded)
    pad_start = pad_end - padded
    grp_start = jnp.cumsum(counts) - counts
    dest = pad_start[se] + jnp.arange(A) - grp_start[se]
    n_blocks = -(-(A + N_EXPERTS * (MOE_BLOCK - 1)) // MOE_BLOCK)
    rows = n_blocks * MOE_BLOCK
    xb = jnp.zeros((rows, h.shape[1]), h.dtype).at[dest].set(h[stok])
    blk_e = jnp.minimum(jnp.searchsorted(pad_end, jnp.arange(n_blocks) * MOE_BLOCK, side='right'),
                        N_EXPERTS - 1)

    def expert_block(args):
        xblk, e = args
        gu = jnp.dot(xblk, w_gu[e]) + b_gu[e]
        x_glu = jnp.minimum(gu[:, :D_FF], SWIGLU_LIMIT)
        x_lin = jnp.clip(gu[:, D_FF:], -SWIGLU_LIMIT, SWIGLU_LIMIT)
        act = x_glu * jax.nn.sigmoid(SWIGLU_ALPHA * x_glu) * (x_lin + 1.0)
        return jnp.dot(act, w_down[e]) + b_down[e]

    yb = lax.map(expert_block, (xb.reshape(n_blocks, MOE_BLOCK, -1), blk_e)).reshape(rows, -1)
    contrib = sg[:, None].astype(h.dtype) * yb[dest]
    return jnp.zeros_like(h).at[stok].add(contrib)


def _layer(x, c, pool_buf, S0, pos0, w_mod, b_mod, g_norm1, w_in, w_alpha, b_alpha, w_pool,
           pool_scale, g_head, w_out, g_norm2, w_router, b_router, w_gu, b_gu, w_down, b_down):
    B, L, _ = x.shape
    mod = jnp.dot(jax.nn.silu(c), w_mod) + b_mod
    sh1, sc1, gt1, sh2, sc2, gt2 = jnp.split(mod[:, None, :], 6, axis=-1)
    h = _rmsnorm(x, g_norm1) * (1.0 + sc1) + sh1
    z = jnp.einsum('bld,dc->blc', h, w_in)
    idx = [int(i) for i in np.cumsum(IN_SPLITS)[:-1]]
    u, q, k, v, r, a_code, gA, gB = jnp.split(z, idx, axis=-1)
    a_out, new_buf = _pool_mixer(u, pool_buf, pos0, w_pool, pool_scale)
    loga = jax.nn.log_sigmoid(
        (jnp.einsum('blr,rk->blk', a_code, w_alpha) + b_alpha).astype(jnp.float32)) / GATE_TAU
    qh = q.reshape(B, L, GLA_HEADS, DK_HEAD) * (DK_HEAD ** -0.5)
    kh = k.reshape(B, L, GLA_HEADS, DK_HEAD)
    vh = v.reshape(B, L, GLA_HEADS, DV_HEAD)
    o, S_new = _gla(qh, kh, vh, loga.reshape(B, L, GLA_HEADS, DK_HEAD), S0)
    b_out = _rmsnorm(o, g_head).reshape(B, L, D_V) * jax.nn.silu(r)
    m = jax.nn.sigmoid(gA) * a_out + jax.nn.sigmoid(gB) * b_out
    x = x + (1.0 + gt1) * jnp.einsum('bld,de->ble', m, w_out)
    h2 = _rmsnorm(x, g_norm2) * (1.0 + sc2) + sh2
    f = _moe(h2.reshape(B * L, D_MODEL), w_router, b_router, w_gu, b_gu, w_down, b_down)
    x = x + (1.0 + gt2) * f.reshape(B, L, D_MODEL)
    return x, new_buf, S_new


def _run_group(x, c, pool_bufs, gla_states, pos0, w_mod, b_mod, g_norm1, w_in, w_alpha, b_alpha,
               w_pool, pool_scale, g_head, w_out, g_norm2, w_router, b_router, w_gu, b_gu,
               w_down, b_down, g_final):
    new_pool, new_gla = [], []
    for l in range(DEPTH):
        x, pb, s = _layer(x, c, pool_bufs[l], gla_states[l], pos0, w_mod[l], b_mod[l], g_norm1[l],
                          w_in[l], w_alpha[l], b_alpha[l], w_pool[l], pool_scale[l], g_head[l],
                          w_out[l], g_norm2[l], w_router[l], b_router[l], w_gu[l], b_gu[l],
                          w_down[l], b_down[l])
        new_pool.append(pb)
        new_gla.append(s)
    return _rmsnorm(x, g_final), jnp.stack(new_pool), jnp.stack(new_gla)


def setup_inputs(seed: int = 0) -> dict:
    key = jax.random.key(seed)
    ks = jax.random.split(key, 24)

    def nrm(k, shape, s):
        return jax.random.normal(k, shape, jnp.float32) * s

    return {
        "x_prompt": nrm(ks[0], (BATCH, SEQ, D_MODEL), 1.0),
        "x_sample": nrm(ks[1], (DEC_BATCH, DEC_SEQ, D_MODEL), 1.0),
        "c_prompt": nrm(ks[2], (BATCH, D_MODEL), 1.0),
        "c_sample": nrm(ks[3], (DEC_BATCH, D_MODEL), 1.0),
        "state_pool": nrm(ks[4], (DEPTH, DEC_BATCH, POOL_BUF, D_POOL), 1.0),
        "state_gla": nrm(ks[5], (DEPTH, DEC_BATCH, GLA_HEADS, DK_HEAD, DV_HEAD), 1.0),
        "w_mod": nrm(ks[6], (DEPTH, D_MODEL, 6 * D_MODEL), 0.5 * D_MODEL ** -0.5),
        "b_mod": nrm(ks[7], (DEPTH, 6 * D_MODEL), 0.02),
        "g_norm1": 1.0 + nrm(ks[8], (DEPTH, D_MODEL), 0.02),
        "w_in": nrm(ks[9], (DEPTH, D_MODEL, IN_COLS), D_MODEL ** -0.5),
        "w_alpha": nrm(ks[10], (DEPTH, ALPHA_RANK, D_K), ALPHA_RANK ** -0.5),
        "b_alpha": nrm(ks[11], (DEPTH, D_K), 0.1),
        "w_pool": nrm(ks[12], (DEPTH, N_POOL_GROUPS, POOL_GROUP_IN, POOL_GROUP_OUT), POOL_GROUP_IN ** -0.5),
        "pool_scale": 1.0 + nrm(ks[13], (DEPTH, D_MODEL), 0.02),
        "g_head": 1.0 + nrm(ks[14], (DEPTH, DV_HEAD), 0.02),
        "w_out": nrm(ks[15], (DEPTH, D_MODEL, D_MODEL), D_MODEL ** -0.5),
        "g_norm2": 1.0 + nrm(ks[16], (DEPTH, D_MODEL), 0.02),
        "w_router": nrm(ks[17], (DEPTH, D_MODEL, N_EXPERTS), D_MODEL ** -0.5),
        "b_router": nrm(ks[18], (DEPTH, N_EXPERTS), 0.01),
        "w_gu": nrm(ks[19], (DEPTH, N_EXPERTS, D_MODEL, 2 * D_FF), D_MODEL ** -0.5),
        "b_gu": nrm(ks[20], (DEPTH, N_EXPERTS, 2 * D_FF), 0.02),
        "w_down": nrm(ks[21], (DEPTH, N_EXPERTS, D_FF, D_MODEL), D_FF ** -0.5),
        "b_down": nrm(ks[22], (DEPTH, N_EXPERTS, D_MODEL), 0.02),
        "g_final": 1.0 + nrm(ks[23], (D_MODEL,), 0.02),
    }


def reference(x_prompt, x_sample, c_prompt, c_sample, state_pool, state_gla, w_mod, b_mod, g_norm1,
              w_in, w_alpha, b_alpha, w_pool, pool_scale, g_head, w_out, g_norm2, w_router, b_router,
              w_gu, b_gu, w_down, b_down, g_final):
    pool0 = jnp.zeros((DEPTH, x_prompt.shape[0], POOL_BUF, D_POOL), x_prompt.dtype)
    gla0 = jnp.zeros((DEPTH, x_prompt.shape[0], GLA_HEADS, DK_HEAD, DV_HEAD), jnp.float32)
    y_prompt, new_pool_prompt, new_gla_prompt = _run_group(
        x_prompt, c_prompt, pool0, gla0, 0, w_mod, b_mod, g_norm1, w_in, w_alpha, b_alpha, w_pool,
        pool_scale, g_head, w_out, g_norm2, w_router, b_router, w_gu, b_gu, w_down, b_down, g_final)
    y_sample, new_pool_sample, new_gla_sample = _run_group(
        x_sample, c_sample, state_pool, state_gla, PAST_LEN, w_mod, b_mod, g_norm1, w_in, w_alpha,
        b_alpha, w_pool, pool_scale, g_head, w_out, g_norm2, w_router, b_router, w_gu, b_gu, w_down,
        b_down, g_final)
    return (y_prompt, y_sample, new_pool_prompt, new_gla_prompt, new_pool_sample, new_gla_sample)
```

```python
import functools

import jax
import jax.numpy as jnp
from jax import lax
from jax.experimental import pallas as pl
from jax.experimental.pallas import tpu as pltpu

F32 = jnp.float32
BF16 = jnp.bfloat16

D_MODEL = 2048
BATCH = 4
SEQ = 2048
DEC_BATCH = 128
PAST_LEN = 16384
POOL_WINDOWS = (2, 4, 8, 16)
D_POOL = D_MODEL // 2
POOL_GROUP_IN = D_POOL // 4
POOL_GROUP_OUT = D_MODEL // 4
POOL_BUF = 15
GLA_HEADS = 4
D_K = D_MODEL // 2
D_V = D_MODEL
DK_HEAD = D_K // GLA_HEADS
DV_HEAD = D_V // GLA_HEADS
ALPHA_RANK = 16
GATE_TAU = 16.0
N_EXPERTS = 32
TOP_K = 4
D_FF = D_MODEL
SWIGLU_LIMIT = 7.0
SWIGLU_ALPHA = 1.702
EPS = 1e-6

LANES = 128
T_PROMPT = BATCH * SEQ
T_ALL = T_PROMPT + DEC_BATCH
N_ASSIGN = T_ALL * TOP_K

COL_U, COL_Q, COL_K, COL_V, COL_R = 0, 1024, 2048, 3072, 5120
COL_A = 7168
COL_G = COL_A + ALPHA_RANK

GLA_CHUNK = 64
GLA_SUB = 16
MIX_ROWS = 256

MOE_SB = 256
MOE_ITEM_BLOCKS = 8
MOE_MC = MOE_SB * MOE_ITEM_BLOCKS
MOE_TF = 256
MOE_NF = D_FF // MOE_TF
MOE_NBLK = -(-(N_ASSIGN + N_EXPERTS * (MOE_SB - 1)) // MOE_SB)
MOE_ROWS = MOE_NBLK * MOE_SB
MOE_NITEMS = N_EXPERTS + -(-MOE_NBLK // MOE_ITEM_BLOCKS)

VMEM_LIMIT = 56 * 1024 * 1024


def _cparams(sem, vmem=VMEM_LIMIT):
    return pltpu.CompilerParams(dimension_semantics=sem, vmem_limit_bytes=vmem)


def _rms(x, g):
    return x * lax.rsqrt(jnp.mean(x * x, axis=-1, keepdims=True) + EPS) * g


def _sigmoid(x):
    return 1.0 / (1.0 + jnp.exp(-x))


def _mod_kernel(c_ref, w_ref, b_ref, o_ref):
    c = c_ref[...]
    s = (c * _sigmoid(c)).astype(BF16)
    o_ref[...] = jnp.dot(s, w_ref[...].astype(BF16), preferred_element_type=F32) + b_ref[...]


def _mod(c_all, w_mod, b_mod):
    m = c_all.shape[0]
    tn = 1024
    return pl.pallas_call(
        _mod_kernel,
        grid=(6 * D_MODEL // tn,),
        in_specs=[pl.BlockSpec((m, D_MODEL), lambda j: (0, 0)),
                  pl.BlockSpec((D_MODEL, tn), lambda j: (0, j)),
                  pl.BlockSpec((1, tn), lambda j: (0, j))],
        out_specs=pl.BlockSpec((m, tn), lambda j: (0, j)),
        out_shape=jax.ShapeDtypeStruct((m, 6 * D_MODEL), F32),
        compiler_params=_cparams(("arbitrary",)),
        name="mod",
    )(c_all, w_mod, b_mod.reshape(1, -1))


def _mod_spec_prompt(chunk, rows_per_batch_tiles):
    return pl.BlockSpec((1, 1, D_MODEL), lambda i: (i // rows_per_batch_tiles, 0, chunk))


def _mod_spec_sample(chunk):
    return pl.BlockSpec((DEC_BATCH, D_MODEL), lambda i: (0, chunk))


def _modval(ref):
    v = ref[...]
    return v.reshape(v.shape[-2], v.shape[-1])


def _prenorm_kernel(*refs, aliased):
    if aliased:
        refs = refs[1:]
    x_ref, g_ref, sh_ref, sc_ref, o_ref = refs
    y = _rms(x_ref[...], g_ref[...])
    o_ref[...] = (y * (1.0 + _modval(sc_ref)) + _modval(sh_ref)).astype(BF16)


def _prenorm(x_p, x_s, g, mod_p3, mod_s):
    tr = 512
    tiles_per_batch = SEQ // tr
    h = pl.pallas_call(
        functools.partial(_prenorm_kernel, aliased=False),
        grid=(T_PROMPT // tr,),
        in_specs=[pl.BlockSpec((tr, D_MODEL), lambda i: (i, 0)),
                  pl.BlockSpec((1, D_MODEL), lambda i: (0, 0)),
                  _mod_spec_prompt(0, tiles_per_batch), _mod_spec_prompt(1, tiles_per_batch)],
        out_specs=pl.BlockSpec((tr, D_MODEL), lambda i: (i, 0)),
        out_shape=jax.ShapeDtypeStruct((T_ALL, D_MODEL), BF16),
        compiler_params=_cparams(("arbitrary",)),
        name="prenorm_p",
    )(x_p, g, mod_p3, mod_p3)
    return pl.pallas_call(
        functools.partial(_prenorm_kernel, aliased=True),
        grid=(1,),
        in_specs=[pl.BlockSpec(memory_space=pl.ANY),
                  pl.BlockSpec((DEC_BATCH, D_MODEL), lambda i: (0, 0)),
                  pl.BlockSpec((1, D_MODEL), lambda i: (0, 0)),
                  _mod_spec_sample(0), _mod_spec_sample(1)],
        out_specs=pl.BlockSpec((DEC_BATCH, D_MODEL), lambda i: (T_PROMPT // DEC_BATCH, 0)),
        out_shape=jax.ShapeDtypeStruct((T_ALL, D_MODEL), BF16),
        input_output_aliases={0: 0},
        compiler_params=_cparams(("arbitrary",)),
        name="prenorm_s",
    )(h, x_s, g, mod_s, mod_s)


def _inproj_kernel(h_ref, w_ref, o_ref, wb_ref):
    @pl.when(pl.program_id(1) == 0)
    def _():
        wb_ref[...] = w_ref[...].astype(BF16)

    o_ref[...] = jnp.dot(h_ref[...], wb_ref[...], preferred_element_type=F32).astype(o_ref.dtype)


def _inproj(h_all, w, col0, ncols, out_dtype, tn, name):
    tm = 1664
    joff = col0 // tn
    return pl.pallas_call(
        _inproj_kernel,
        grid=(ncols // tn, T_ALL // tm),
        in_specs=[pl.BlockSpec((tm, D_MODEL), lambda j, i: (i, 0)),
                  pl.BlockSpec((D_MODEL, tn), lambda j, i: (0, j + joff))],
        out_specs=pl.BlockSpec((tm, tn), lambda j, i: (i, j)),
        out_shape=jax.ShapeDtypeStruct((T_ALL, ncols), out_dtype),
        scratch_shapes=[pltpu.VMEM((D_MODEL, tn), BF16)],
        compiler_params=_cparams(("arbitrary", "arbitrary")),
        name=name,
    )(h_all, w)


def _log_decay(ac, wal, bal):
    x = jnp.dot(ac.astype(BF16), wal, preferred_element_type=F32) + bal
    return (jnp.minimum(x, 0.0) - jnp.log1p(jnp.exp(-jnp.abs(x)))) / GATE_TAU


def _mixp_kernel(u_ref, q_ref, k_ref, v_ref, r_ref, ac_ref, ga_ref, gb_ref, wal_ref, bal_ref,
                 wpool_ref, pscale_ref, ghead_ref, m_ref, st_ref,
                 ext_ref, state_ref, b_ref, q32_ref, k32_ref, v32_ref, o_ref):
    R = MIX_ROWS
    h = pl.program_id(1)
    t = pl.program_id(2)

    @pl.when(t == 0)
    def _():
        ext_ref[0:16, :] = jnp.zeros((16, POOL_GROUP_IN), F32)
        state_ref[...] = jnp.zeros_like(state_ref)

    u = u_ref[...]
    ext_ref[16:16 + R, :] = u
    e = ext_ref[...]
    s2 = e + pltpu.roll(e, 1, axis=0)
    s4 = s2 + pltpu.roll(s2, 2, axis=0)
    s8 = s4 + pltpu.roll(s4, 4, axis=0)
    s16 = s8 + pltpu.roll(s8, 8, axis=0)
    s = jnp.where(h == 0, s2, jnp.where(h == 1, s4, jnp.where(h == 2, s8, s16)))[16:, :]
    window = jnp.left_shift(2, h)
    pos = t * R + lax.broadcasted_iota(jnp.int32, (R, 1), 0)
    cnt = jnp.minimum(pos + 1, window).astype(F32)
    p = s / cnt - u
    a_out = jnp.dot(p.astype(BF16), wpool_ref[0], preferred_element_type=F32) * pscale_ref[...]
    ext_ref[0:16, :] = ext_ref[R:R + 16, :]

    g = _log_decay(ac_ref[...], wal_ref[...], bal_ref[...])
    ri = lax.broadcasted_iota(jnp.int32, (R, R), 0)
    ci = lax.broadcasted_iota(jnp.int32, (R, R), 1)
    tri = jnp.where((ri >= ci) & (ri // GLA_CHUNK == ci // GLA_CHUNK), 1.0, 0.0).astype(BF16)
    g_hi = g.astype(BF16)
    g_lo = (g - g_hi.astype(F32)).astype(BF16)
    b_ref[...] = (jnp.dot(tri, g_hi, preferred_element_type=F32)
                  + jnp.dot(tri, g_lo, preferred_element_type=F32))
    q32_ref[...] = q_ref[...].astype(F32) * (DK_HEAD ** -0.5)
    k32_ref[...] = k_ref[...].astype(F32)
    v32_ref[...] = v_ref[...].astype(F32)

    nt_dims = (((1,), (1,)), ((), ()))
    tn_dims = (((0,), (0,)), ((), ()))
    sub_iota = lax.broadcasted_iota(jnp.int32, (GLA_SUB, 1), 0)

    def chunk_body(c, carry):
        r0 = pl.multiple_of(c * GLA_CHUNK, GLA_CHUNK)
        bc = b_ref[pl.ds(r0, GLA_CHUNK), :]
        qc = q32_ref[pl.ds(r0, GLA_CHUNK), :]
        kc = k32_ref[pl.ds(r0, GLA_CHUNK), :]
        vc = v32_ref[pl.ds(r0, GLA_CHUNK), :]
        vcb = vc.astype(BF16)
        st = state_ref[...]
        qt = (qc * jnp.exp(bc)).astype(BF16)
        o_ref[pl.ds(r0, GLA_CHUNK), :] = lax.dot_general(qt, st.astype(BF16), nt_dims,
                                                         preferred_element_type=F32)
        for i in range(GLA_CHUNK // GLA_SUB):
            lo = i * GLA_SUB
            bsub = bc[lo:lo + GLA_SUB, :]
            qsub = qc[lo:lo + GLA_SUB, :]
            acc = jnp.zeros((GLA_SUB, DV_HEAD), F32)
            if i > 0:
                ref_row = bc[lo:lo + 1, :]
                qi = (qsub * jnp.exp(bsub - ref_row)).astype(BF16)
                ki = (kc[0:lo, :] * jnp.exp(ref_row - bc[0:lo, :])).astype(BF16)
                att = lax.dot_general(qi, ki, nt_dims, preferred_element_type=F32)
                acc = jnp.dot(att.astype(BF16), vcb[0:lo, :], preferred_element_type=F32)

            def diag_body(j, acc, lo=lo, bsub=bsub, qsub=qsub):
                row = r0 + lo + j
                bs = b_ref[pl.ds(row, 1), :]
                ks = k32_ref[pl.ds(row, 1), :]
                vs = v32_ref[pl.ds(row, 1), :]
                dec = jnp.exp(jnp.where(sub_iota >= j, bsub - bs, -jnp.inf))
                col = jnp.sum(qsub * dec * ks, axis=-1, keepdims=True)
                return acc + col * vs

            acc = lax.fori_loop(0, GLA_SUB, diag_body, acc)
            o_ref[pl.ds(r0 + lo, GLA_SUB), :] += acc
        b_last = bc[GLA_CHUNK - 1:GLA_CHUNK, :]
        kd = (kc * jnp.exp(b_last - bc)).astype(BF16)
        upd = lax.dot_general(vcb, kd, tn_dims, preferred_element_type=F32)
        state_ref[...] = st * jnp.exp(b_last) + upd
        return carry

    lax.fori_loop(0, R // GLA_CHUNK, chunk_body, 0)

    o = _rms(o_ref[...], ghead_ref[...])
    r = r_ref[...].astype(F32)
    b_out = o * (r * _sigmoid(r))
    m = _sigmoid(ga_ref[...].astype(F32)) * a_out + _sigmoid(gb_ref[...].astype(F32)) * b_out
    m_ref[...] = m.astype(BF16)

    @pl.when(t == pl.num_programs(2) - 1)
    def _():
        st_ref[0, 0] = state_ref[...].T


def _mix_prompt(u_all, qkvr, acode, gates, wal, bal, wpool, pscale, ghead):
    R = MIX_ROWS
    nt = SEQ // R

    def rows(b, h, t):
        return b * nt + t

    in_specs = [
        pl.BlockSpec((R, POOL_GROUP_IN), lambda b, h, t: (rows(b, h, t), h)),
        pl.BlockSpec((R, DK_HEAD), lambda b, h, t: (rows(b, h, t), h)),
        pl.BlockSpec((R, DK_HEAD), lambda b, h, t: (rows(b, h, t), GLA_HEADS + h)),
        pl.BlockSpec((R, DV_HEAD), lambda b, h, t: (rows(b, h, t), GLA_HEADS + h)),
        pl.BlockSpec((R, DV_HEAD), lambda b, h, t: (rows(b, h, t), 2 * GLA_HEADS + h)),
        pl.BlockSpec((R, LANES), lambda b, h, t: (rows(b, h, t), 0)),
        pl.BlockSpec((R, DV_HEAD), lambda b, h, t: (rows(b, h, t), h)),
        pl.BlockSpec((R, DV_HEAD), lambda b, h, t: (rows(b, h, t), GLA_HEADS + h)),
        pl.BlockSpec((LANES, DK_HEAD), lambda b, h, t: (0, h)),
        pl.BlockSpec((1, DK_HEAD), lambda b, h, t: (0, h)),
        pl.BlockSpec((1, POOL_GROUP_IN, POOL_GROUP_OUT), lambda b, h, t: (h, 0, 0)),
        pl.BlockSpec((1, POOL_GROUP_OUT), lambda b, h, t: (0, h)),
        pl.BlockSpec((1, DV_HEAD), lambda b, h, t: (0, 0)),
    ]
    out_specs = [
        pl.BlockSpec((R, DV_HEAD), lambda b, h, t: (rows(b, h, t), h)),
        pl.BlockSpec((1, 1, DK_HEAD, DV_HEAD), lambda b, h, t: (b, h, 0, 0)),
    ]
    return pl.pallas_call(
        _mixp_kernel,
        grid=(BATCH, GLA_HEADS, nt),
        in_specs=in_specs,
        out_specs=out_specs,
        out_shape=[jax.ShapeDtypeStruct((T_PROMPT, D_MODEL), BF16),
                   jax.ShapeDtypeStruct((BATCH, GLA_HEADS, DK_HEAD, DV_HEAD), F32)],
        scratch_shapes=[pltpu.VMEM((16 + R, POOL_GROUP_IN), F32),
                        pltpu.VMEM((DV_HEAD, DK_HEAD), F32),
                        pltpu.VMEM((R, DK_HEAD), F32),
                        pltpu.VMEM((R, DK_HEAD), F32),
                        pltpu.VMEM((R, DK_HEAD), F32),
                        pltpu.VMEM((R, DV_HEAD), F32),
                        pltpu.VMEM((R, DV_HEAD), F32)],
        compiler_params=_cparams(("arbitrary", "arbitrary", "arbitrary")),
        name="mix_prompt",
    )(u_all, qkvr, qkvr, qkvr, qkvr, acode, gates, gates, wal, bal, wpool, pscale, ghead)


def _spre_kernel(u_ref, sp_ref, ac_ref, wal_ref, bal_ref, wpool_ref, pscale_ref, a_ref, g_ref):
    u = u_ref[...]
    for gi, w in enumerate(POOL_WINDOWS):
        lo = gi * POOL_GROUP_IN
        s = u[:, lo:lo + POOL_GROUP_IN]
        for j in range(POOL_BUF - (w - 1), POOL_BUF):
            s = s + sp_ref[:, j * D_POOL + lo:j * D_POOL + lo + POOL_GROUP_IN]
        cnt = float(min(PAST_LEN + 1, w))
        p = s / cnt - u[:, lo:lo + POOL_GROUP_IN]
        a = jnp.dot(p.astype(BF16), wpool_ref[gi], preferred_element_type=F32)
        olo = gi * POOL_GROUP_OUT
        a_ref[:, olo:olo + POOL_GROUP_OUT] = a * pscale_ref[:, olo:olo + POOL_GROUP_OUT]
    g_ref[...] = _log_decay(ac_ref[...], wal_ref[...], bal_ref[...])


def _sample_pre(u_all, sp2, acode, wal, bal, wpool, pscale):
    blk = T_PROMPT // DEC_BATCH
    return pl.pallas_call(
        _spre_kernel,
        grid=(1,),
        in_specs=[pl.BlockSpec((DEC_BATCH, D_POOL), lambda i: (blk, 0)),
                  pl.BlockSpec((DEC_BATCH, POOL_BUF * D_POOL), lambda i: (0, 0)),
                  pl.BlockSpec((DEC_BATCH, LANES), lambda i: (blk, 0)),
                  pl.BlockSpec((LANES, D_K), lambda i: (0, 0)),
                  pl.BlockSpec((1, D_K), lambda i: (0, 0)),
                  pl.BlockSpec((4, POOL_GROUP_IN, POOL_GROUP_OUT), lambda i: (0, 0, 0)),
                  pl.BlockSpec((1, D_MODEL), lambda i: (0, 0))],
        out_specs=[pl.BlockSpec((DEC_BATCH, D_MODEL), lambda i: (0, 0)),
                   pl.BlockSpec((DEC_BATCH, D_K), lambda i: (0, 0))],
        out_shape=[jax.ShapeDtypeStruct((DEC_BATCH, D_MODEL), F32),
                   jax.ShapeDtypeStruct((DEC_BATCH, D_K), F32)],
        compiler_params=_cparams(("arbitrary",)),
        name="sample_pre",
    )(u_all, sp2, acode, wal, bal, wpool, pscale)


def _sstate_kernel(gkq_ref, v_ref, r_ref, a_ref, gt_ref, s_ref, ghead_ref, so_ref, m_ref):
    x = gkq_ref[0]
    rowi = lax.broadcasted_iota(jnp.int32, x.shape, 0)
    x = jnp.where(rowi == 0, jnp.exp(x), jnp.where(rowi == 2, x * (DK_HEAD ** -0.5), x))
    xt = x.T
    v = v_ref[0]
    r = r_ref[0]
    a_out = a_ref[0]
    gates = gt_ref[0]
    for h in range(GLA_HEADS):
        cols = xt[h * DK_HEAD:(h + 1) * DK_HEAD, :]
        dec, kcol, qcol = cols[:, 0:1], cols[:, 1:2], cols[:, 2:3]
        lo = h * DV_HEAD
        vrow = v[:, lo:lo + DV_HEAD]
        s_new = dec * s_ref[0, h] + kcol * vrow
        so_ref[0, h] = s_new
        o = jnp.sum(qcol * s_new, axis=0, keepdims=True)
        o = _rms(o, ghead_ref[...])
        rr = r[:, lo:lo + DV_HEAD]
        b_out = o * (rr * _sigmoid(rr))
        m_ref[0, :, lo:lo + DV_HEAD] = (_sigmoid(gates[:, lo:lo + DV_HEAD]) * a_out[:, lo:lo + DV_HEAD]
                                        + _sigmoid(gates[:, D_MODEL + lo:D_MODEL + lo + DV_HEAD]) * b_out)


def _sample_state(gkq, v3, r3, a3, gates3, state, ghead):
    def row3(width):
        return pl.BlockSpec((1, 1, width), lambda i: (i, 0, 0))

    sspec = pl.BlockSpec((1, GLA_HEADS, DK_HEAD, DV_HEAD), lambda i: (i, 0, 0, 0))
    return pl.pallas_call(
        _sstate_kernel,
        grid=(DEC_BATCH,),
        in_specs=[pl.BlockSpec((1, 8, D_K), lambda i: (i, 0, 0)),
                  row3(D_V), row3(D_V), row3(D_MODEL), row3(2 * D_MODEL), sspec,
                  pl.BlockSpec((1, DV_HEAD), lambda i: (0, 0))],
        out_specs=[sspec, row3(D_MODEL)],
        out_shape=[jax.ShapeDtypeStruct((DEC_BATCH, GLA_HEADS, DK_HEAD, DV_HEAD), F32),
                   jax.ShapeDtypeStruct((DEC_BATCH, 1, D_MODEL), F32)],
        compiler_params=_cparams(("arbitrary",)),
        name="sample_state",
    )(gkq, v3, r3, a3, gates3, state, ghead)


def _outproj_kernel(*refs, aliased):
    if aliased:
        refs = refs[4:]
    (m_ref, x_ref, wo_ref, gt1_ref, sh2_ref, sc2_ref, g2_ref, wr_ref, br_ref,
     x1_ref, h2_ref, ti_ref, tg_ref) = refs
    y = jnp.dot(m_ref[...], wo_ref[...], preferred_element_type=F32)
    x1 = x_ref[...] + (1.0 + _modval(gt1_ref)) * y
    x1_ref[...] = x1
    h2 = _rms(x1, g2_ref[...]) * (1.0 + _modval(sc2_ref)) + _modval(sh2_ref)
    h2_ref[...] = h2
    logits = jnp.dot(h2.astype(BF16), wr_ref[...], preferred_element_type=F32) + br_ref[...]
    lane = lax.broadcasted_iota(jnp.int32, logits.shape, 1)
    vals, idxs = [], []
    cur = logits
    for _ in range(TOP_K):
        mx = jnp.max(cur, axis=-1, keepdims=True)
        ix = jnp.min(jnp.where(cur == mx, lane, LANES), axis=-1, keepdims=True)
        vals.append(mx)
        idxs.append(ix)
        cur = jnp.where(lane == ix, -jnp.inf, cur)
    exps = [jnp.exp(v - vals[0]) for v in vals]
    den = exps[0] + exps[1] + exps[2] + exps[3]
    ti = jnp.zeros(logits.shape, jnp.int32)
    tg = jnp.zeros(logits.shape, F32)
    for k in range(TOP_K):
        ti = jnp.where(lane == k, idxs[k], ti)
        tg = jnp.where(lane == k, exps[k] / den, tg)
    ti_ref[...] = ti
    tg_ref[...] = tg


def _outproj(m_p, m_s, x_p, x_s, wo, mod_p3, mod_s, g2, wr, br):
    tr = 512
    tiles_per_batch = SEQ // tr
    out_shape = [jax.ShapeDtypeStruct((T_ALL, D_MODEL), F32),
                 jax.ShapeDtypeStruct((T_ALL, D_MODEL), F32),
                 jax.ShapeDtypeStruct((T_ALL, LANES), jnp.int32),
                 jax.ShapeDtypeStruct((T_ALL, LANES), F32)]
    consts = [pl.BlockSpec((1, D_MODEL), lambda i: (0, 0)),
              pl.BlockSpec((D_MODEL, LANES), lambda i: (0, 0)),
              pl.BlockSpec((1, LANES), lambda i: (0, 0))]
    wspec = pl.BlockSpec((D_MODEL, D_MODEL), lambda i: (0, 0))
    outs = pl.pallas_call(
        functools.partial(_outproj_kernel, aliased=False),
        grid=(T_PROMPT // tr,),
        in_specs=[pl.BlockSpec((tr, D_MODEL), lambda i: (i, 0)),
                  pl.BlockSpec((tr, D_MODEL), lambda i: (i, 0)),
                  wspec,
                  _mod_spec_prompt(2, tiles_per_batch), _mod_spec_prompt(3, tiles_per_batch),
                  _mod_spec_prompt(4, tiles_per_batch)] + consts,
        out_specs=[pl.BlockSpec((tr, D_MODEL), lambda i: (i, 0)),
                   pl.BlockSpec((tr, D_MODEL), lambda i: (i, 0)),
                   pl.BlockSpec((tr, LANES), lambda i: (i, 0)),
                   pl.BlockSpec((tr, LANES), lambda i: (i, 0))],
        out_shape=out_shape,
        compiler_params=_cparams(("arbitrary",)),
        name="outproj_p",
    )(m_p, x_p, wo, mod_p3, mod_p3, mod_p3, g2, wr, br)
    sblk = T_PROMPT // DEC_BATCH
    any_spec = pl.BlockSpec(memory_space=pl.ANY)
    return pl.pallas_call(
        functools.partial(_outproj_kernel, aliased=True),
        grid=(1,),
        in_specs=[any_spec] * 4 + [
            pl.BlockSpec((DEC_BATCH, D_MODEL), lambda i: (0, 0)),
            pl.BlockSpec((DEC_BATCH, D_MODEL), lambda i: (0, 0)),
            wspec,
            _mod_spec_sample(2), _mod_spec_sample(3), _mod_spec_sample(4)] + consts,
        out_specs=[pl.BlockSpec((DEC_BATCH, D_MODEL), lambda i: (sblk, 0)),
                   pl.BlockSpec((DEC_BATCH, D_MODEL), lambda i: (sblk, 0)),
                   pl.BlockSpec((DEC_BATCH, LANES), lambda i: (sblk, 0)),
                   pl.BlockSpec((DEC_BATCH, LANES), lambda i: (sblk, 0))],
        out_shape=out_shape,
        input_output_aliases={0: 0, 1: 1, 2: 2, 3: 3},
        compiler_params=_cparams(("arbitrary",)),
        name="outproj_s",
    )(*outs, m_s, x_s, wo, mod_s, mod_s, mod_s, g2, wr, br)


def _dispatch_kernel(src_ref, h2_hbm, xb_ref, stage_ref, sem):
    i = pl.program_id(0)
    n = pl.num_programs(0)
    slot = i % 2

    def row_copy(blk, j, slot):
        tok = src_ref[blk * MOE_SB + j]
        return pltpu.make_async_copy(h2_hbm.at[pl.ds(tok, 1)], stage_ref.at[slot, pl.ds(j, 1)],
                                     sem.at[slot])

    def start_block(blk, slot):
        def body(j, c):
            row_copy(blk, j, slot).start()
            return c
        lax.fori_loop(0, MOE_SB, body, 0)

    @pl.when(i == 0)
    def _():
        start_block(0, 0)

    @pl.when(i + 1 < n)
    def _():
        start_block(i + 1, 1 - slot)

    pltpu.make_async_copy(h2_hbm.at[pl.ds(0, MOE_SB)], stage_ref.at[slot], sem.at[slot]).wait()
    xb_ref[...] = stage_ref[slot].astype(BF16)


def _dispatch(src_tok, h2_all):
    return pl.pallas_call(
        _dispatch_kernel,
        grid_spec=pltpu.PrefetchScalarGridSpec(
            num_scalar_prefetch=1,
            grid=(MOE_NBLK,),
            in_specs=[pl.BlockSpec(memory_space=pl.ANY)],
            out_specs=pl.BlockSpec((MOE_SB, D_MODEL), lambda i, src: (i, 0)),
            scratch_shapes=[pltpu.VMEM((2, MOE_SB, D_MODEL), F32),
                            pltpu.SemaphoreType.DMA((2,))]),
        out_shape=jax.ShapeDtypeStruct((MOE_ROWS, D_MODEL), BF16),
        compiler_params=_cparams(("arbitrary",)),
        name="dispatch",
    )(src_tok, h2_all)


def _moe_kernel(ie_ref, ib0_ref, inb_ref, ival_ref,
                xb_hbm, wg_ref, wu_ref, wd_ref, bg_ref, bu_ref, bd_ref, yb_hbm,
                xbuf, acc, wgb, wub, wdb, ystage, xsem, ysem):
    i = pl.program_id(0)
    f = pl.program_id(1)
    n_items = pl.num_programs(0)
    nb = inb_ref[i]
    slot = i % 2

    def x_copy(item, blk, slot):
        row0 = (ib0_ref[item] + blk) * MOE_SB
        return pltpu.make_async_copy(xb_hbm.at[pl.ds(row0, MOE_SB)],
                                     xbuf.at[slot, pl.ds(blk * MOE_SB, MOE_SB)], xsem.at[slot])

    def y_copy(blk, ys):
        row0 = (ib0_ref[i] + blk) * MOE_SB
        return pltpu.make_async_copy(ystage.at[ys], yb_hbm.at[pl.ds(row0, MOE_SB)], ysem.at[ys])

    def start_item(item, slot):
        def body(blk, c):
            x_copy(item, blk, slot).start()
            return c
        lax.fori_loop(0, inb_ref[item], body, 0)

    @pl.when(f == 0)
    def _():
        @pl.when(i == 0)
        def _():
            start_item(0, 0)

        @pl.when(i + 1 < n_items)
        def _():
            start_item(i + 1, 1 - slot)

        def wbody(blk, c):
            x_copy(i, blk, slot).wait()
            return c
        lax.fori_loop(0, nb, wbody, 0)

    wgb[...] = wg_ref[0].astype(BF16)
    wub[...] = wu_ref[0].astype(BF16)
    wdb[...] = wd_ref[0].astype(BF16)
    bg = bg_ref[0]
    bu = bu_ref[0]
    bd = bd_ref[0]
    last = f == MOE_NF - 1

    def blk_body(blk, c):
        r0 = pl.multiple_of(blk * MOE_SB, MOE_SB)
        x = xbuf[slot, pl.ds(r0, MOE_SB), :]
        g = jnp.dot(x, wgb[...], preferred_element_type=F32) + bg
        u = jnp.dot(x, wub[...], preferred_element_type=F32) + bu
        xg = jnp.minimum(g, SWIGLU_LIMIT)
        xl = jnp.clip(u, -SWIGLU_LIMIT, SWIGLU_LIMIT)
        act = xg * _sigmoid(SWIGLU_ALPHA * xg) * (xl + 1.0)
        part = jnp.dot(act.astype(BF16), wdb[...], preferred_element_type=F32)

        @pl.when(f == 0)
        def _():
            acc[pl.ds(r0, MOE_SB), :] = part + bd

        @pl.when(f > 0)
        def _():
            acc[pl.ds(r0, MOE_SB), :] += part

        @pl.when(last)
        def _():
            ys = blk % 2

            @pl.when(blk >= 2)
            def _():
                y_copy(blk - 2, ys).wait()

            ystage[ys] = acc[pl.ds(r0, MOE_SB), :]
            y_copy(blk, ys).start()
        return c

    lax.fori_loop(0, nb, blk_body, 0)

    @pl.when(last & (nb >= 1))
    def _():
        y_copy(nb - 1, (nb - 1) % 2).wait()

    @pl.when(last & (nb >= 2))
    def _():
        y_copy(nb - 2, nb % 2).wait()


def _moe(item_e, item_b0, item_nb, item_valid, xb, w_gu, b_gu3, w_down, b_down3):
    def f_eff(f, ival, i):
        return jnp.where(ival[i] > 0, f, MOE_NF - 1)

    in_specs = [
        pl.BlockSpec(memory_space=pl.ANY),
        pl.BlockSpec((1, D_MODEL, MOE_TF), lambda i, f, ie, ib0, inb, iv: (ie[i], 0, f_eff(f, iv, i))),
        pl.BlockSpec((1, D_MODEL, MOE_TF),
                     lambda i, f, ie, ib0, inb, iv: (ie[i], 0, MOE_NF + f_eff(f, iv, i))),
        pl.BlockSpec((1, MOE_TF, D_MODEL), lambda i, f, ie, ib0, inb, iv: (ie[i], f_eff(f, iv, i), 0)),
        pl.BlockSpec((1, 1, MOE_TF), lambda i, f, ie, ib0, inb, iv: (ie[i], 0, f_eff(f, iv, i))),
        pl.BlockSpec((1, 1, MOE_TF),
                     lambda i, f, ie, ib0, inb, iv: (ie[i], 0, MOE_NF + f_eff(f, iv, i))),
        pl.BlockSpec((1, 1, D_MODEL), lambda i, f, ie, ib0, inb, iv: (ie[i], 0, 0)),
    ]
    return pl.pallas_call(
        _moe_kernel,
        grid_spec=pltpu.PrefetchScalarGridSpec(
            num_scalar_prefetch=4,
            grid=(MOE_NITEMS, MOE_NF),
            in_specs=in_specs,
            out_specs=pl.BlockSpec(memory_space=pl.ANY),
            scratch_shapes=[pltpu.VMEM((2, MOE_MC, D_MODEL), BF16),
                            pltpu.VMEM((MOE_MC, D_MODEL), F32),
                            pltpu.VMEM((D_MODEL, MOE_TF), BF16),
                            pltpu.VMEM((D_MODEL, MOE_TF), BF16),
                            pltpu.VMEM((MOE_TF, D_MODEL), BF16),
                            pltpu.VMEM((2, MOE_SB, D_MODEL), F32),
                            pltpu.SemaphoreType.DMA((2,)),
                            pltpu.SemaphoreType.DMA((2,))]),
        out_shape=jax.ShapeDtypeStruct((MOE_ROWS, D_MODEL), F32),
        compiler_params=_cparams(("arbitrary", "arbitrary")),
        name="moe",
    )(item_e, item_b0, item_nb, item_valid, xb, w_gu, w_gu, w_down, b_gu3, b_gu3, b_down3)


FIN_TOK = 128
FIN_ROWS = FIN_TOK * TOP_K


def _final_kernel(dest_ref, yb_hbm, x1_ref, tg_ref, gt2_ref, gf_ref, y_ref, stage_ref, sem, *, tile0):
    i = pl.program_id(0)
    n = pl.num_programs(0)
    slot = i % 2

    def start_tile(tile, slot):
        base = (tile0 + tile) * FIN_ROWS

        def body(j, c):
            row = dest_ref[base + j]
            dst = (j % TOP_K) * FIN_TOK + j // TOP_K
            pltpu.make_async_copy(yb_hbm.at[pl.ds(row, 1)], stage_ref.at[slot, pl.ds(dst, 1)],
                                  sem.at[slot]).start()
            return c
        lax.fori_loop(0, FIN_ROWS, body, 0)

    @pl.when(i == 0)
    def _():
        start_tile(0, 0)

    @pl.when(i + 1 < n)
    def _():
        start_tile(i + 1, 1 - slot)

    pltpu.make_async_copy(yb_hbm.at[pl.ds(0, FIN_ROWS)], stage_ref.at[slot], sem.at[slot]).wait()
    tg = tg_ref[...]
    f = jnp.zeros((FIN_TOK, D_MODEL), F32)
    for k in range(TOP_K):
        f = f + tg[:, k:k + 1] * stage_ref[slot, k * FIN_TOK:(k + 1) * FIN_TOK, :]
    x = x1_ref[...] + (1.0 + _modval(gt2_ref)) * f
    y_ref[...] = _rms(x, gf_ref[...])


def _final(dest, yb, x1_all, tg_all, mod, gf, *, prompt):
    if prompt:
        ntiles, tile0 = T_PROMPT // FIN_TOK, 0
        mod_spec = pl.BlockSpec((1, 1, D_MODEL), lambda i, d: (i // (SEQ // FIN_TOK), 0, 5))
        rows_out = T_PROMPT
    else:
        ntiles, tile0 = 1, T_PROMPT // FIN_TOK
        mod_spec = pl.BlockSpec((DEC_BATCH, D_MODEL), lambda i, d: (0, 5))
        rows_out = DEC_BATCH
    return pl.pallas_call(
        functools.partial(_final_kernel, tile0=tile0),
        grid_spec=pltpu.PrefetchScalarGridSpec(
            num_scalar_prefetch=1,
            grid=(ntiles,),
            in_specs=[pl.BlockSpec(memory_space=pl.ANY),
                      pl.BlockSpec((FIN_TOK, D_MODEL), lambda i, d: (tile0 + i, 0)),
                      pl.BlockSpec((FIN_TOK, LANES), lambda i, d: (tile0 + i, 0)),
                      mod_spec,
                      pl.BlockSpec((1, D_MODEL), lambda i, d: (0, 0))],
            out_specs=pl.BlockSpec((FIN_TOK, D_MODEL), lambda i, d: (i, 0)),
            scratch_shapes=[pltpu.VMEM((2, FIN_ROWS, D_MODEL), F32),
                            pltpu.SemaphoreType.DMA((2,))]),
        out_shape=jax.ShapeDtypeStruct((rows_out, D_MODEL), F32),
        compiler_params=_cparams(("arbitrary",)),
        name="final_p" if prompt else "final_s",
    )(dest, yb, x1_all, tg_all, mod, gf)


def _routing(top_idx):
    e_flat = top_idx.reshape(-1)
    onehot = (e_flat[:, None] == jnp.arange(N_EXPERTS, dtype=jnp.int32)[None, :]).astype(jnp.int32)
    csum = jnp.cumsum(onehot, axis=0)
    counts = csum[-1]
    rank = jnp.take_along_axis(csum, e_flat[:, None], axis=1)[:, 0] - 1
    nblk = (counts + MOE_SB - 1) // MOE_SB
    blk_end = jnp.cumsum(nblk)
    blk_start = blk_end - nblk
    dest = blk_start[e_flat] * MOE_SB + rank

    order = jnp.argsort(e_flat, stable=True)
    grp_start = jnp.cumsum(counts) - counts
    rows = jnp.arange(MOE_ROWS, dtype=jnp.int32)
    row_e = jnp.minimum(jnp.searchsorted(blk_end, rows // MOE_SB, side="right"), N_EXPERTS - 1)
    j = rows - blk_start[row_e] * MOE_SB
    valid = (j < counts[row_e]) & (rows // MOE_SB < blk_end[-1])
    sorted_idx = jnp.clip(grp_start[row_e] + j, 0, N_ASSIGN - 1)
    src_tok = jnp.where(valid, order[sorted_idx] // TOP_K, 0).astype(jnp.int32)

    n_items_e = (nblk + MOE_ITEM_BLOCKS - 1) // MOE_ITEM_BLOCKS
    item_end = jnp.cumsum(n_items_e)
    item_start = item_end - n_items_e
    total_items = item_end[-1]
    it = jnp.arange(MOE_NITEMS, dtype=jnp.int32)
    it_c = jnp.minimum(it, total_items - 1)
    item_e = jnp.minimum(jnp.searchsorted(item_end, it_c, side="right"), N_EXPERTS - 1).astype(jnp.int32)
    jj = it_c - item_start[item_e]
    item_valid = (it < total_items).astype(jnp.int32)
    item_b0 = (blk_start[item_e] + jj * MOE_ITEM_BLOCKS).astype(jnp.int32)
    item_nb = jnp.where(item_valid > 0,
                        jnp.minimum(MOE_ITEM_BLOCKS, nblk[item_e] - jj * MOE_ITEM_BLOCKS), 0).astype(jnp.int32)
    return dest.astype(jnp.int32), src_tok, item_e, item_b0, item_nb, item_valid


def kernel(x_prompt, x_sample, c_prompt, c_sample, state_pool, state_gla, w_mod, b_mod, g_norm1, w_in,
           w_alpha, b_alpha, w_pool, pool_scale, g_head, w_out, g_norm2, w_router, b_router, w_gu, b_gu,
           w_down, b_down, g_final):
    x_p = x_prompt.reshape(T_PROMPT, D_MODEL)
    x_s = x_sample.reshape(DEC_BATCH, D_MODEL)

    c_all = jnp.concatenate([c_prompt, c_sample, jnp.zeros((4, D_MODEL), F32)], axis=0)
    mod = _mod(c_all, w_mod[0], b_mod[0])
    mod_p3 = mod[:BATCH].reshape(BATCH, 1, 6 * D_MODEL)
    mod_s = mod[BATCH:BATCH + DEC_BATCH]

    h_all = _prenorm(x_p, x_s, g_norm1, mod_p3, mod_s)

    w_in0 = w_in[0]
    w_a = jnp.pad(w_in0[:, COL_A:COL_G], ((0, 0), (0, LANES - ALPHA_RANK)))
    w_g = w_in0[:, COL_G:]
    u_all = _inproj(h_all, w_in0, COL_U, D_POOL, F32, 512, "inproj_u")
    qkvr = _inproj(h_all, w_in0, COL_Q, COL_A - COL_Q, BF16, 512, "inproj_qkvr")
    acode = _inproj(h_all, w_a, 0, LANES, F32, LANES, "inproj_a")
    gates = _inproj(h_all, w_g, 0, 2 * D_MODEL, BF16, 512, "inproj_g")

    wal = jnp.pad(w_alpha[0], ((0, LANES - ALPHA_RANK), (0, 0))).astype(BF16)
    bal = b_alpha[0].reshape(1, D_K)
    wpool = w_pool[0].astype(BF16)
    ghead = g_head[0].reshape(1, DV_HEAD)

    m_p, new_gla_p = _mix_prompt(u_all, qkvr, acode, gates, wal, bal, wpool, pool_scale, ghead)

    sp2 = state_pool[0].reshape(DEC_BATCH, POOL_BUF * D_POOL)
    a_s, g_s = _sample_pre(u_all, sp2, acode, wal, bal, wpool, pool_scale)
    zs = qkvr[T_PROMPT:].astype(F32)
    q_s, k_s = zs[:, 0:D_K], zs[:, D_K:2 * D_K]
    gkq = jnp.stack([g_s, k_s, q_s] + [jnp.zeros_like(g_s)] * 5, axis=1)
    v3 = zs[:, 2 * D_K:2 * D_K + D_V].reshape(DEC_BATCH, 1, D_V)
    r3 = zs[:, 2 * D_K + D_V:].reshape(DEC_BATCH, 1, D_V)
    gates3 = gates[T_PROMPT:].astype(F32).reshape(DEC_BATCH, 1, 2 * D_MODEL)
    new_gla_s, m_s3 = _sample_state(gkq, v3, r3, a_s.reshape(DEC_BATCH, 1, D_MODEL), gates3,
                                    state_gla[0], ghead)
    m_s = m_s3.reshape(DEC_BATCH, D_MODEL).astype(BF16)

    wo = w_out[0].astype(BF16)
    wr = jnp.pad(w_router[0], ((0, 0), (0, LANES - N_EXPERTS))).astype(BF16)
    br = jnp.concatenate([b_router[0], jnp.full((LANES - N_EXPERTS,), -1e30, F32)]).reshape(1, LANES)
    x1_all, h2_all, ti_all, tg_all = _outproj(m_p, m_s, x_p, x_s, wo, mod_p3, mod_s, g_norm2, wr, br)

    dest, src_tok, item_e, item_b0, item_nb, item_valid = _routing(ti_all[:, :TOP_K])
    xb = _dispatch(src_tok, h2_all)
    yb = _moe(item_e, item_b0, item_nb, item_valid, xb, w_gu[0], b_gu[0].reshape(N_EXPERTS, 1, 2 * D_FF),
              w_down[0], b_down[0].reshape(N_EXPERTS, 1, D_MODEL))

    gf = g_final.reshape(1, D_MODEL)
    y_p = _final(dest, yb, x1_all, tg_all, mod_p3, gf, prompt=True)
    y_s = _final(dest, yb, x1_all, tg_all, mod_s, gf, prompt=False)

    u_p = u_all[:T_PROMPT].reshape(BATCH, SEQ, D_POOL)
    new_pool_p = u_p[:, SEQ - POOL_BUF:, :][None]
    new_pool_s = jnp.concatenate([state_pool[0][:, 1:, :], u_all[T_PROMPT:][:, None, :]], axis=1)[None]
    return (y_p.reshape(BATCH, SEQ, D_MODEL), y_s.reshape(DEC_BATCH, 1, D_MODEL),
            new_pool_p, new_gla_p[None], new_pool_s, new_gla_s[None])
```

```python
import functools

import jax
import jax.numpy as jnp
from jax import lax
from jax.experimental import pallas as pl
from jax.experimental.pallas import tpu as pltpu

F32 = jnp.float32
BF16 = jnp.bfloat16

D_MODEL = 2048
BATCH = 4
SEQ = 2048
DEC_BATCH = 128
PAST_LEN = 16384
POOL_WINDOWS = (2, 4, 8, 16)
D_POOL = D_MODEL // 2
POOL_GROUP_IN = D_POOL // 4
POOL_GROUP_OUT = D_MODEL // 4
POOL_BUF = 15
GLA_HEADS = 4
D_K = D_MODEL // 2
D_V = D_MODEL
DK_HEAD = D_K // GLA_HEADS
DV_HEAD = D_V // GLA_HEADS
ALPHA_RANK = 16
GATE_TAU = 16.0
N_EXPERTS = 32
TOP_K = 4
D_FF = D_MODEL
SWIGLU_LIMIT = 7.0
SWIGLU_ALPHA = 1.702
EPS = 1e-6

LANES = 128
T_PROMPT = BATCH * SEQ
T_ALL = T_PROMPT + DEC_BATCH
N_ASSIGN = T_ALL * TOP_K

COL_U, COL_Q, COL_K, COL_V, COL_R = 0, 1024, 2048, 3072, 5120
COL_A = 7168
COL_G = COL_A + ALPHA_RANK

GLA_CHUNK = 128
GLA_SAFE_LOG_DECAY = -60.0
GLA_SUB = 16
MIX_ROWS = 256

MOE_SB = 256
MOE_ITEM_BLOCKS = 6
MOE_MC = MOE_SB * MOE_ITEM_BLOCKS
MOE_TF = 256
MOE_NF = D_FF // MOE_TF
MOE_NBLK = -(-(N_ASSIGN + N_EXPERTS * (MOE_SB - 1)) // MOE_SB)
MOE_ROWS = MOE_NBLK * MOE_SB
MOE_NITEMS = N_EXPERTS + -(-MOE_NBLK // MOE_ITEM_BLOCKS)

VMEM_LIMIT = 56 * 1024 * 1024


def _cparams(sem, vmem=VMEM_LIMIT):
    return pltpu.CompilerParams(dimension_semantics=sem, vmem_limit_bytes=vmem)


def _rms(x, g):
    return x * lax.rsqrt(jnp.mean(x * x, axis=-1, keepdims=True) + EPS) * g


def _sigmoid(x):
    return 1.0 / (1.0 + jnp.exp(-x))


def _mod_kernel(c_ref, w_ref, b_ref, o_ref):
    c = c_ref[...]
    s = (c * _sigmoid(c)).astype(BF16)
    o_ref[...] = jnp.dot(s, w_ref[...].astype(BF16), preferred_element_type=F32) + b_ref[...]


def _mod(c_all, w_mod, b_mod):
    m = c_all.shape[0]
    tn = 1024
    return pl.pallas_call(
        _mod_kernel,
        grid=(6 * D_MODEL // tn,),
        in_specs=[pl.BlockSpec((m, D_MODEL), lambda j: (0, 0)),
                  pl.BlockSpec((D_MODEL, tn), lambda j: (0, j)),
                  pl.BlockSpec((1, tn), lambda j: (0, j))],
        out_specs=pl.BlockSpec((m, tn), lambda j: (0, j)),
        out_shape=jax.ShapeDtypeStruct((m, 6 * D_MODEL), F32),
        compiler_params=_cparams(("arbitrary",)),
        name="mod",
    )(c_all, w_mod, b_mod.reshape(1, -1))


ROW_TILE = 512
ROW_TILES_P = T_PROMPT // ROW_TILE
_TILES_PER_BATCH = SEQ // ROW_TILE


def _prompt_tile(i):
    return jnp.minimum(i, ROW_TILES_P - 1)


def _prompt_rows_spec(width):
    return pl.BlockSpec((ROW_TILE, width), lambda i: (_prompt_tile(i), 0))


def _sample_rows_spec(width):
    return pl.BlockSpec((DEC_BATCH, width), lambda i: (0, 0))


def _mod_spec_prompt(chunk):
    return pl.BlockSpec((1, 1, D_MODEL), lambda i: (_prompt_tile(i) // _TILES_PER_BATCH, 0, chunk))


def _mod_spec_sample(chunk):
    return pl.BlockSpec((DEC_BATCH, D_MODEL), lambda i: (0, chunk))


def _modval(ref):
    v = ref[...]
    return v.reshape(v.shape[-2], v.shape[-1])


def _prenorm_kernel(xp_ref, xs_ref, g_ref, shp_ref, scp_ref, shs_ref, scs_ref, o_ref):
    i = pl.program_id(0)

    def body(x_ref, sh_ref, sc_ref):
        y = _rms(x_ref[...], g_ref[...])
        return (y * (1.0 + _modval(sc_ref)) + _modval(sh_ref)).astype(BF16)

    @pl.when(i < ROW_TILES_P)
    def _():
        o_ref[...] = body(xp_ref, shp_ref, scp_ref)

    @pl.when(i == ROW_TILES_P)
    def _():
        o_ref[0:DEC_BATCH, :] = body(xs_ref, shs_ref, scs_ref)


def _prenorm(x_p, x_s, g, mod_p3, mod_s):
    return pl.pallas_call(
        _prenorm_kernel,
        grid=(ROW_TILES_P + 1,),
        in_specs=[_prompt_rows_spec(D_MODEL), _sample_rows_spec(D_MODEL),
                  pl.BlockSpec((1, D_MODEL), lambda i: (0, 0)),
                  _mod_spec_prompt(0), _mod_spec_prompt(1), _mod_spec_sample(0), _mod_spec_sample(1)],
        out_specs=pl.BlockSpec((ROW_TILE, D_MODEL), lambda i: (i, 0)),
        out_shape=jax.ShapeDtypeStruct((T_ALL, D_MODEL), BF16),
        compiler_params=_cparams(("arbitrary",)),
        name="prenorm",
    )(x_p, x_s, g, mod_p3, mod_p3, mod_s, mod_s)


def _inproj_kernel(h_ref, w_ref, o_ref, wb_ref):
    @pl.when(pl.program_id(1) == 0)
    def _():
        wb_ref[...] = w_ref[...].astype(BF16)

    o_ref[...] = jnp.dot(h_ref[...], wb_ref[...], preferred_element_type=F32).astype(o_ref.dtype)


def _inproj(h_all, w, col0, ncols, out_dtype, tn, name):
    tm = 1664
    joff = col0 // tn
    return pl.pallas_call(
        _inproj_kernel,
        grid=(ncols // tn, T_ALL // tm),
        in_specs=[pl.BlockSpec((tm, D_MODEL), lambda j, i: (i, 0)),
                  pl.BlockSpec((D_MODEL, tn), lambda j, i: (0, j + joff))],
        out_specs=pl.BlockSpec((tm, tn), lambda j, i: (i, j)),
        out_shape=jax.ShapeDtypeStruct((T_ALL, ncols), out_dtype),
        scratch_shapes=[pltpu.VMEM((D_MODEL, tn), BF16)],
        compiler_params=_cparams(("arbitrary", "arbitrary")),
        name=name,
    )(h_all, w)


def _log_decay(ac, wal, bal):
    x = jnp.dot(ac.astype(BF16), wal, preferred_element_type=F32) + bal
    return (jnp.minimum(x, 0.0) - jnp.log1p(jnp.exp(-jnp.abs(x)))) / GATE_TAU


def _mixp_kernel(u_ref, q_ref, k_ref, v_ref, r_ref, ac_ref, ga_ref, gb_ref, wal_ref, bal_ref,
                 wpool_ref, pscale_ref, ghead_ref, m_ref, st_ref,
                 ext_ref, state_ref, b_ref, q32_ref, k32_ref, v32_ref, o_ref):
    R = MIX_ROWS
    h = pl.program_id(1)
    t = pl.program_id(2)

    @pl.when(t == 0)
    def _():
        ext_ref[0:16, :] = jnp.zeros((16, POOL_GROUP_IN), F32)
        state_ref[...] = jnp.zeros_like(state_ref)

    u = u_ref[...]
    ext_ref[16:16 + R, :] = u
    e = ext_ref[...]
    s2 = e + pltpu.roll(e, 1, axis=0)
    s4 = s2 + pltpu.roll(s2, 2, axis=0)
    s8 = s4 + pltpu.roll(s4, 4, axis=0)
    s16 = s8 + pltpu.roll(s8, 8, axis=0)
    s = jnp.where(h == 0, s2, jnp.where(h == 1, s4, jnp.where(h == 2, s8, s16)))[16:, :]
    window = jnp.left_shift(2, h)
    pos = t * R + lax.broadcasted_iota(jnp.int32, (R, 1), 0)
    cnt = jnp.minimum(pos + 1, window).astype(F32)
    p = s / cnt - u
    a_out = jnp.dot(p.astype(BF16), wpool_ref[0], preferred_element_type=F32) * pscale_ref[...]
    ext_ref[0:16, :] = ext_ref[R:R + 16, :]

    C = GLA_CHUNK
    g = _log_decay(ac_ref[...], wal_ref[...], bal_ref[...])
    ri = lax.broadcasted_iota(jnp.int32, (C, C), 0)
    ci = lax.broadcasted_iota(jnp.int32, (C, C), 1)
    causal = ri >= ci
    tri = jnp.where(causal, 1.0, 0.0).astype(BF16)
    g_hi = g.astype(BF16)
    g_lo = (g - g_hi.astype(F32)).astype(BF16)
    for c in range(R // C):
        b_ref[c * C:(c + 1) * C, :] = (
            jnp.dot(tri, g_hi[c * C:(c + 1) * C, :], preferred_element_type=F32)
            + jnp.dot(tri, g_lo[c * C:(c + 1) * C, :], preferred_element_type=F32))
    q32_ref[...] = q_ref[...].astype(F32) * (DK_HEAD ** -0.5)
    k32_ref[...] = k_ref[...].astype(F32)
    v32_ref[...] = v_ref[...].astype(F32)

    nt_dims = (((1,), (1,)), ((), ()))
    tn_dims = (((0,), (0,)), ((), ()))
    sub_iota = lax.broadcasted_iota(jnp.int32, (GLA_SUB, 1), 0)

    for c in range(R // C):
        r0 = c * C
        bc = b_ref[r0:r0 + C, :]
        qc = q32_ref[r0:r0 + C, :]
        kc = k32_ref[r0:r0 + C, :]
        vcb = v32_ref[r0:r0 + C, :].astype(BF16)
        st = state_ref[...]
        b_last = bc[C - 1:C, :]
        e_last = jnp.exp(b_last)
        safe = jnp.min(b_last) >= GLA_SAFE_LOG_DECAY
        qt = (qc * jnp.exp(bc)).astype(BF16)
        o_inter = lax.dot_general(qt, st.astype(BF16), nt_dims, preferred_element_type=F32)

        @pl.when(safe)
        def _(r0=r0, bc=bc, kc=kc, vcb=vcb, st=st, e_last=e_last, qt=qt, o_inter=o_inter):
            kt = kc * jnp.exp(-bc)
            att = lax.dot_general(qt, kt.astype(BF16), nt_dims, preferred_element_type=F32)
            att = jnp.where(causal, att, 0.0).astype(BF16)
            o_ref[r0:r0 + C, :] = o_inter + jnp.dot(att, vcb, preferred_element_type=F32)
            kd = (kt * e_last).astype(BF16)
            state_ref[...] = st * e_last + lax.dot_general(vcb, kd, tn_dims, preferred_element_type=F32)

        @pl.when(jnp.logical_not(safe))
        def _(r0=r0, bc=bc, qc=qc, kc=kc, vcb=vcb, st=st, b_last=b_last, e_last=e_last, o_inter=o_inter):
            o_ref[r0:r0 + C, :] = o_inter
            for i in range(C // GLA_SUB):
                lo = i * GLA_SUB
                bsub = bc[lo:lo + GLA_SUB, :]
                qsub = qc[lo:lo + GLA_SUB, :]
                acc = jnp.zeros((GLA_SUB, DV_HEAD), F32)
                if i > 0:
                    ref_row = bc[lo:lo + 1, :]
                    qi = (qsub * jnp.exp(bsub - ref_row)).astype(BF16)
                    ki = (kc[0:lo, :] * jnp.exp(ref_row - bc[0:lo, :])).astype(BF16)
                    att = lax.dot_general(qi, ki, nt_dims, preferred_element_type=F32)
                    acc = jnp.dot(att.astype(BF16), vcb[0:lo, :], preferred_element_type=F32)

                def diag_body(j, acc, lo=lo, bsub=bsub, qsub=qsub):
                    row = r0 + lo + j
                    bs = b_ref[pl.ds(row, 1), :]
                    ks = k32_ref[pl.ds(row, 1), :]
                    vs = v32_ref[pl.ds(row, 1), :]
                    dec = jnp.exp(jnp.where(sub_iota >= j, bsub - bs, -jnp.inf))
                    col = jnp.sum(qsub * dec * ks, axis=-1, keepdims=True)
                    return acc + col * vs

                acc = lax.fori_loop(0, GLA_SUB, diag_body, acc)
                o_ref[r0 + lo:r0 + lo + GLA_SUB, :] += acc
            kd = (kc * jnp.exp(b_last - bc)).astype(BF16)
            state_ref[...] = st * e_last + lax.dot_general(vcb, kd, tn_dims, preferred_element_type=F32)

    o = _rms(o_ref[...], ghead_ref[...])
    r = r_ref[...].astype(F32)
    b_out = o * (r * _sigmoid(r))
    m = _sigmoid(ga_ref[...].astype(F32)) * a_out + _sigmoid(gb_ref[...].astype(F32)) * b_out
    m_ref[...] = m.astype(BF16)

    @pl.when(t == pl.num_programs(2) - 1)
    def _():
        st_ref[0, 0] = state_ref[...].T


def _mix_prompt(u_all, qkvr, acode, gates, wal, bal, wpool, pscale, ghead):
    R = MIX_ROWS
    nt = SEQ // R

    def rows(b, h, t):
        return b * nt + t

    in_specs = [
        pl.BlockSpec((R, POOL_GROUP_IN), lambda b, h, t: (rows(b, h, t), h)),
        pl.BlockSpec((R, DK_HEAD), lambda b, h, t: (rows(b, h, t), h)),
        pl.BlockSpec((R, DK_HEAD), lambda b, h, t: (rows(b, h, t), GLA_HEADS + h)),
        pl.BlockSpec((R, DV_HEAD), lambda b, h, t: (rows(b, h, t), GLA_HEADS + h)),
        pl.BlockSpec((R, DV_HEAD), lambda b, h, t: (rows(b, h, t), 2 * GLA_HEADS + h)),
        pl.BlockSpec((R, LANES), lambda b, h, t: (rows(b, h, t), 0)),
        pl.BlockSpec((R, DV_HEAD), lambda b, h, t: (rows(b, h, t), h)),
        pl.BlockSpec((R, DV_HEAD), lambda b, h, t: (rows(b, h, t), GLA_HEADS + h)),
        pl.BlockSpec((LANES, DK_HEAD), lambda b, h, t: (0, h)),
        pl.BlockSpec((1, DK_HEAD), lambda b, h, t: (0, h)),
        pl.BlockSpec((1, POOL_GROUP_IN, POOL_GROUP_OUT), lambda b, h, t: (h, 0, 0)),
        pl.BlockSpec((1, POOL_GROUP_OUT), lambda b, h, t: (0, h)),
        pl.BlockSpec((1, DV_HEAD), lambda b, h, t: (0, 0)),
    ]
    out_specs = [
        pl.BlockSpec((R, DV_HEAD), lambda b, h, t: (rows(b, h, t), h)),
        pl.BlockSpec((1, 1, DK_HEAD, DV_HEAD), lambda b, h, t: (b, h, 0, 0)),
    ]
    return pl.pallas_call(
        _mixp_kernel,
        grid=(BATCH, GLA_HEADS, nt),
        in_specs=in_specs,
        out_specs=out_specs,
        out_shape=[jax.ShapeDtypeStruct((T_PROMPT, D_MODEL), BF16),
                   jax.ShapeDtypeStruct((BATCH, GLA_HEADS, DK_HEAD, DV_HEAD), F32)],
        scratch_shapes=[pltpu.VMEM((16 + R, POOL_GROUP_IN), F32),
                        pltpu.VMEM((DV_HEAD, DK_HEAD), F32),
                        pltpu.VMEM((R, DK_HEAD), F32),
                        pltpu.VMEM((R, DK_HEAD), F32),
                        pltpu.VMEM((R, DK_HEAD), F32),
                        pltpu.VMEM((R, DV_HEAD), F32),
                        pltpu.VMEM((R, DV_HEAD), F32)],
        compiler_params=_cparams(("arbitrary", "arbitrary", "arbitrary")),
        name="mix_prompt",
    )(u_all, qkvr, qkvr, qkvr, qkvr, acode, gates, gates, wal, bal, wpool, pscale, ghead)


def _spre_kernel(u_ref, sp_ref, ac_ref, wal_ref, bal_ref, wpool_ref, pscale_ref, a_ref, g_ref):
    u = u_ref[...]
    for gi, w in enumerate(POOL_WINDOWS):
        lo = gi * POOL_GROUP_IN
        s = u[:, lo:lo + POOL_GROUP_IN]
        for j in range(POOL_BUF - (w - 1), POOL_BUF):
            s = s + sp_ref[:, j * D_POOL + lo:j * D_POOL + lo + POOL_GROUP_IN]
        cnt = float(min(PAST_LEN + 1, w))
        p = s / cnt - u[:, lo:lo + POOL_GROUP_IN]
        a = jnp.dot(p.astype(BF16), wpool_ref[gi], preferred_element_type=F32)
        olo = gi * POOL_GROUP_OUT
        a_ref[:, olo:olo + POOL_GROUP_OUT] = a * pscale_ref[:, olo:olo + POOL_GROUP_OUT]
    g_ref[...] = _log_decay(ac_ref[...], wal_ref[...], bal_ref[...])


def _sample_pre(u_all, sp2, acode, wal, bal, wpool, pscale):
    blk = T_PROMPT // DEC_BATCH
    return pl.pallas_call(
        _spre_kernel,
        grid=(1,),
        in_specs=[pl.BlockSpec((DEC_BATCH, D_POOL), lambda i: (blk, 0)),
                  pl.BlockSpec((DEC_BATCH, POOL_BUF * D_POOL), lambda i: (0, 0)),
                  pl.BlockSpec((DEC_BATCH, LANES), lambda i: (blk, 0)),
                  pl.BlockSpec((LANES, D_K), lambda i: (0, 0)),
                  pl.BlockSpec((1, D_K), lambda i: (0, 0)),
                  pl.BlockSpec((4, POOL_GROUP_IN, POOL_GROUP_OUT), lambda i: (0, 0, 0)),
                  pl.BlockSpec((1, D_MODEL), lambda i: (0, 0))],
        out_specs=[pl.BlockSpec((DEC_BATCH, D_MODEL), lambda i: (0, 0)),
                   pl.BlockSpec((DEC_BATCH, D_K), lambda i: (0, 0))],
        out_shape=[jax.ShapeDtypeStruct((DEC_BATCH, D_MODEL), F32),
                   jax.ShapeDtypeStruct((DEC_BATCH, D_K), F32)],
        compiler_params=_cparams(("arbitrary",)),
        name="sample_pre",
    )(u_all, sp2, acode, wal, bal, wpool, pscale)


def _sstate_kernel(gkq_ref, v_ref, r_ref, a_ref, gt_ref, s_ref, ghead_ref, so_ref, m_ref):
    x = gkq_ref[0]
    rowi = lax.broadcasted_iota(jnp.int32, x.shape, 0)
    x = jnp.where(rowi == 0, jnp.exp(x), jnp.where(rowi == 2, x * (DK_HEAD ** -0.5), x))
    xt = x.T
    v = v_ref[0]
    r = r_ref[0]
    a_out = a_ref[0]
    gates = gt_ref[0]
    for h in range(GLA_HEADS):
        cols = xt[h * DK_HEAD:(h + 1) * DK_HEAD, :]
        dec, kcol, qcol = cols[:, 0:1], cols[:, 1:2], cols[:, 2:3]
        lo = h * DV_HEAD
        vrow = v[:, lo:lo + DV_HEAD]
        s_new = dec * s_ref[0, h] + kcol * vrow
        so_ref[0, h] = s_new
        o = jnp.sum(qcol * s_new, axis=0, keepdims=True)
        o = _rms(o, ghead_ref[...])
        rr = r[:, lo:lo + DV_HEAD]
        b_out = o * (rr * _sigmoid(rr))
        m_ref[0, :, lo:lo + DV_HEAD] = (_sigmoid(gates[:, lo:lo + DV_HEAD]) * a_out[:, lo:lo + DV_HEAD]
                                        + _sigmoid(gates[:, D_MODEL + lo:D_MODEL + lo + DV_HEAD]) * b_out)


def _sample_state(gkq, v3, r3, a3, gates3, state, ghead):
    def row3(width):
        return pl.BlockSpec((1, 1, width), lambda i: (i, 0, 0))

    sspec = pl.BlockSpec((1, GLA_HEADS, DK_HEAD, DV_HEAD), lambda i: (i, 0, 0, 0))
    return pl.pallas_call(
        _sstate_kernel,
        grid=(DEC_BATCH,),
        in_specs=[pl.BlockSpec((1, 8, D_K), lambda i: (i, 0, 0)),
                  row3(D_V), row3(D_V), row3(D_MODEL), row3(2 * D_MODEL), sspec,
                  pl.BlockSpec((1, DV_HEAD), lambda i: (0, 0))],
        out_specs=[sspec, row3(D_MODEL)],
        out_shape=[jax.ShapeDtypeStruct((DEC_BATCH, GLA_HEADS, DK_HEAD, DV_HEAD), F32),
                   jax.ShapeDtypeStruct((DEC_BATCH, 1, D_MODEL), F32)],
        compiler_params=_cparams(("arbitrary",)),
        name="sample_state",
    )(gkq, v3, r3, a3, gates3, state, ghead)


ROW_LINES = D_MODEL // LANES


def _store_row_tiled(ref, row0, nrows, val):
    for c in range(ROW_LINES):
        ref[pl.ds(row0 * ROW_LINES + c, nrows, stride=ROW_LINES), :] = val[:, c * LANES:(c + 1) * LANES]


def _load_row_tiled(ref, row0, nrows, c):
    return ref[pl.ds(row0 * ROW_LINES + c, nrows, stride=ROW_LINES), :]


def _outproj_kernel(mp_ref, ms_ref, xp_ref, xs_ref, wo_ref, gt1p_ref, sh2p_ref, sc2p_ref,
                    gt1s_ref, sh2s_ref, sc2s_ref, g2_ref, wr_ref, br_ref,
                    x1_ref, h2_ref, ti_ref, tg_ref):
    i = pl.program_id(0)

    def body(nrows, m_ref, x_ref, gt1_ref, sh2_ref, sc2_ref):
        y = jnp.dot(m_ref[...], wo_ref[...], preferred_element_type=F32)
        x1 = x_ref[...] + (1.0 + _modval(gt1_ref)) * y
        x1_ref[0:nrows, :] = x1
        h2 = _rms(x1, g2_ref[...]) * (1.0 + _modval(sc2_ref)) + _modval(sh2_ref)
        _store_row_tiled(h2_ref, 0, nrows, h2)
        logits = jnp.dot(h2.astype(BF16), wr_ref[...], preferred_element_type=F32) + br_ref[...]
        lane = lax.broadcasted_iota(jnp.int32, logits.shape, 1)
        vals, idxs = [], []
        cur = logits
        for _ in range(TOP_K):
            mx = jnp.max(cur, axis=-1, keepdims=True)
            ix = jnp.min(jnp.where(cur == mx, lane, LANES), axis=-1, keepdims=True)
            vals.append(mx)
            idxs.append(ix)
            cur = jnp.where(lane == ix, -jnp.inf, cur)
        exps = [jnp.exp(v - vals[0]) for v in vals]
        den = exps[0] + exps[1] + exps[2] + exps[3]
        ti = jnp.full(logits.shape, -1, jnp.int32)
        tg = jnp.zeros(logits.shape, F32)
        for k in range(TOP_K):
            ti = jnp.where(lane == k, idxs[k], ti)
            tg = jnp.where(lane == k, exps[k] / den, tg)
        ti_ref[0:nrows, :] = ti
        tg_ref[0:nrows, :] = tg

    @pl.when(i < ROW_TILES_P)
    def _():
        body(ROW_TILE, mp_ref, xp_ref, gt1p_ref, sh2p_ref, sc2p_ref)

    @pl.when(i == ROW_TILES_P)
    def _():
        body(DEC_BATCH, ms_ref, xs_ref, gt1s_ref, sh2s_ref, sc2s_ref)


def _outproj(m_p, m_s, x_p, x_s, wo, mod_p3, mod_s, g2, wr, br):
    return pl.pallas_call(
        _outproj_kernel,
        grid=(ROW_TILES_P + 1,),
        in_specs=[_prompt_rows_spec(D_MODEL), _sample_rows_spec(D_MODEL),
                  _prompt_rows_spec(D_MODEL), _sample_rows_spec(D_MODEL),
                  pl.BlockSpec((D_MODEL, D_MODEL), lambda i: (0, 0)),
                  _mod_spec_prompt(2), _mod_spec_prompt(3), _mod_spec_prompt(4),
                  _mod_spec_sample(2), _mod_spec_sample(3), _mod_spec_sample(4),
                  pl.BlockSpec((1, D_MODEL), lambda i: (0, 0)),
                  pl.BlockSpec((D_MODEL, LANES), lambda i: (0, 0)),
                  pl.BlockSpec((1, LANES), lambda i: (0, 0))],
        out_specs=[pl.BlockSpec((ROW_TILE, D_MODEL), lambda i: (i, 0)),
                   pl.BlockSpec((ROW_TILE * ROW_LINES, LANES), lambda i: (i, 0)),
                   pl.BlockSpec((ROW_TILE, LANES), lambda i: (i, 0)),
                   pl.BlockSpec((ROW_TILE, LANES), lambda i: (i, 0))],
        out_shape=[jax.ShapeDtypeStruct((T_ALL, D_MODEL), F32),
                   jax.ShapeDtypeStruct((T_ALL * ROW_LINES, LANES), F32),
                   jax.ShapeDtypeStruct((T_ALL, LANES), jnp.int32),
                   jax.ShapeDtypeStruct((T_ALL, LANES), F32)],
        compiler_params=_cparams(("arbitrary",)),
        name="outproj",
    )(m_p, m_s, x_p, x_s, wo, mod_p3, mod_p3, mod_p3, mod_s, mod_s, mod_s, g2, wr, br)


ROUTE_TILE = 640
ROUTE_TILES = T_ALL // ROUTE_TILE


def _route_kernel(ti_ref, dest_ref, cnt_ref, bstart_ref, run_ref, base_ref):
    phase = pl.program_id(0)
    t = pl.program_id(1)
    ti = ti_ref[...]
    lane = lax.broadcasted_iota(jnp.int32, (ROUTE_TILE, LANES), 1)
    onehots = [lane == ti[:, k:k + 1] for k in range(TOP_K)]
    per_tok = jnp.zeros((ROUTE_TILE, LANES), F32)
    for oh in onehots:
        per_tok = per_tok + jnp.where(oh, 1.0, 0.0)

    @pl.when((phase == 0) & (t == 0))
    def _():
        run_ref[...] = jnp.zeros_like(run_ref)

    @pl.when(phase == 0)
    def _():
        run_ref[...] += jnp.sum(per_tok, axis=0, keepdims=True)

    @pl.when((phase == 1) & (t == 0))
    def _():
        counts = run_ref[...]
        nblk = jnp.floor((counts + (MOE_SB - 1)) * (1.0 / MOE_SB))
        ri = lax.broadcasted_iota(jnp.int32, (LANES, LANES), 0)
        ci = lax.broadcasted_iota(jnp.int32, (LANES, LANES), 1)
        before = jnp.where(ri < ci, 1.0, 0.0).astype(BF16)
        nb8 = jnp.broadcast_to(nblk, (8, LANES)).astype(BF16)
        bstart = jnp.dot(nb8, before, preferred_element_type=F32)[0:1, :]
        cnt_ref[...] = counts.astype(jnp.int32)
        bstart_ref[...] = bstart.astype(jnp.int32)
        base_ref[...] = bstart * MOE_SB
        run_ref[...] = jnp.zeros_like(run_ref)

    @pl.when(phase == 1)
    def _():
        ri = lax.broadcasted_iota(jnp.int32, (ROUTE_TILE, ROUTE_TILE), 0)
        ci = lax.broadcasted_iota(jnp.int32, (ROUTE_TILE, ROUTE_TILE), 1)
        earlier = jnp.where(ri > ci, 1.0, 0.0).astype(BF16)
        prior = jnp.dot(earlier, per_tok.astype(BF16), preferred_element_type=F32)
        pos = prior + (run_ref[...] + base_ref[...])
        dest = jnp.full((ROUTE_TILE, LANES), -1, jnp.int32)
        for k, oh in enumerate(onehots):
            d = jnp.sum(jnp.where(oh, pos, 0.0), axis=-1, keepdims=True).astype(jnp.int32)
            dest = jnp.where(lane == k, d, dest)
        dest_ref[...] = dest
        run_ref[...] += jnp.sum(per_tok, axis=0, keepdims=True)


def _route(ti_all):
    return pl.pallas_call(
        _route_kernel,
        grid=(2, ROUTE_TILES),
        in_specs=[pl.BlockSpec((ROUTE_TILE, LANES), lambda p, t: (t, 0))],
        out_specs=[pl.BlockSpec((ROUTE_TILE, LANES), lambda p, t: (t * p, 0)),
                   pl.BlockSpec((1, LANES), lambda p, t: (0, 0)),
                   pl.BlockSpec((1, LANES), lambda p, t: (0, 0))],
        out_shape=[jax.ShapeDtypeStruct((T_ALL, LANES), jnp.int32),
                   jax.ShapeDtypeStruct((1, LANES), jnp.int32),
                   jax.ShapeDtypeStruct((1, LANES), jnp.int32)],
        scratch_shapes=[pltpu.VMEM((1, LANES), F32), pltpu.VMEM((1, LANES), F32)],
        compiler_params=_cparams(("arbitrary", "arbitrary")),
        name="route",
    )(ti_all)


GATHER_UNROLL = 8


def _dispatch_kernel(src_ref, h2_hbm, xb_ref, stage_ref, sem):
    i = pl.program_id(0)
    n = pl.num_programs(0)
    slot = i % 2

    def start_block(blk, slot):
        def body(j, c):
            tok = src_ref[blk * MOE_SB + j]
            pltpu.make_async_copy(h2_hbm.at[pl.ds(tok * ROW_LINES, ROW_LINES)],
                                  stage_ref.at[slot, pl.ds(j * ROW_LINES, ROW_LINES)],
                                  sem.at[slot]).start()
            return c
        lax.fori_loop(0, MOE_SB, body, 0, unroll=GATHER_UNROLL)

    @pl.when(i == 0)
    def _():
        start_block(0, 0)

    @pl.when(i + 1 < n)
    def _():
        start_block(i + 1, 1 - slot)

    pltpu.make_async_copy(h2_hbm.at[pl.ds(0, MOE_SB * ROW_LINES)], stage_ref.at[slot], sem.at[slot]).wait()
    for c in range(ROW_LINES):
        xb_ref[:, c * LANES:(c + 1) * LANES] = _load_row_tiled(stage_ref.at[slot], 0, MOE_SB, c).astype(BF16)


def _dispatch(src_tok, h2_tiled):
    return pl.pallas_call(
        _dispatch_kernel,
        grid_spec=pltpu.PrefetchScalarGridSpec(
            num_scalar_prefetch=1,
            grid=(MOE_NBLK,),
            in_specs=[pl.BlockSpec(memory_space=pl.ANY)],
            out_specs=pl.BlockSpec((MOE_SB, D_MODEL), lambda i, src: (i, 0)),
            scratch_shapes=[pltpu.VMEM((2, MOE_SB * ROW_LINES, LANES), F32),
                            pltpu.SemaphoreType.DMA((2,))]),
        out_shape=jax.ShapeDtypeStruct((MOE_ROWS, D_MODEL), BF16),
        compiler_params=_cparams(("arbitrary",)),
        name="dispatch",
    )(src_tok, h2_tiled)


def _moe_kernel(ie_ref, ib0_ref, inb_ref, ival_ref, nused_ref,
                xb_hbm, wg_ref, wu_ref, wd_ref, bg_ref, bu_ref, bd_ref, yb_hbm,
                xbuf, acc, wgb, wub, wdb, ystage, xsem, ysem):
    i = pl.program_id(0)
    f = pl.program_id(1)
    n_items = pl.num_programs(0)
    nb = inb_ref[i]
    slot = i % 2

    def x_copy(item, blk, slot):
        row0 = (ib0_ref[item] + blk) * MOE_SB
        return pltpu.make_async_copy(xb_hbm.at[pl.ds(row0, MOE_SB)],
                                     xbuf.at[slot, pl.ds(blk * MOE_SB, MOE_SB)], xsem.at[slot])

    def y_copy_abs(gblk, ys):
        return pltpu.make_async_copy(ystage.at[ys],
                                     yb_hbm.at[pl.ds(gblk * (MOE_SB * ROW_LINES), MOE_SB * ROW_LINES)],
                                     ysem.at[ys])

    def y_copy(blk, ys):
        return y_copy_abs(ib0_ref[i] + blk, ys)

    def start_item(item, slot):
        def body(blk, c):
            x_copy(item, blk, slot).start()
            return c
        lax.fori_loop(0, inb_ref[item], body, 0)

    @pl.when(f == 0)
    def _():
        @pl.when(i == 0)
        def _():
            start_item(0, 0)

        @pl.when(i + 1 < n_items)
        def _():
            start_item(i + 1, 1 - slot)

        def wbody(blk, c):
            x_copy(i, blk, slot).wait()
            return c
        lax.fori_loop(0, nb, wbody, 0)

    wgb[...] = wg_ref[0].astype(BF16)
    wub[...] = wu_ref[0].astype(BF16)
    wdb[...] = wd_ref[0].astype(BF16)
    bg = bg_ref[0]
    bu = bu_ref[0]
    bd = bd_ref[0]
    last = f == MOE_NF - 1

    def partial_out(blk):
        r0 = pl.multiple_of(blk * MOE_SB, MOE_SB)
        x = xbuf[slot, pl.ds(r0, MOE_SB), :]
        g = jnp.dot(x, wgb[...], preferred_element_type=F32) + bg
        u = jnp.dot(x, wub[...], preferred_element_type=F32) + bu
        xg = jnp.minimum(g, SWIGLU_LIMIT)
        xl = jnp.clip(u, -SWIGLU_LIMIT, SWIGLU_LIMIT)
        act = xg * _sigmoid(SWIGLU_ALPHA * xg) * (xl + 1.0)
        return r0, jnp.dot(act.astype(BF16), wdb[...], preferred_element_type=F32)

    def accumulate(blk, r0, part):
        @pl.when(f == 0)
        def _():
            acc[pl.ds(r0, MOE_SB), :] = part + bd

        @pl.when(f > 0)
        def _():
            acc[pl.ds(r0, MOE_SB), :] += part

        @pl.when(last)
        def _():
            ys = blk % 2

            @pl.when(blk >= 2)
            def _():
                y_copy(blk - 2, ys).wait()

            _store_row_tiled(ystage.at[ys], 0, MOE_SB, acc[pl.ds(r0, MOE_SB), :])
            y_copy(blk, ys).start()

    def pair_body(p, c):
        ra, pa = partial_out(2 * p)
        rb, pb = partial_out(2 * p + 1)
        accumulate(2 * p, ra, pa)
        accumulate(2 * p + 1, rb, pb)
        return c

    lax.fori_loop(0, nb // 2, pair_body, 0)

    @pl.when(nb % 2 == 1)
    def _():
        r0, part = partial_out(nb - 1)
        accumulate(nb - 1, r0, part)

    @pl.when(last & (nb >= 1))
    def _():
        y_copy(nb - 1, (nb - 1) % 2).wait()

    @pl.when(last & (nb >= 2))
    def _():
        y_copy(nb - 2, nb % 2).wait()

    @pl.when(last & (i == n_items - 1))
    def _():
        ystage[0] = jnp.zeros((MOE_SB * ROW_LINES, LANES), F32)

        def zstart(gblk, c):
            y_copy_abs(gblk, 0).start()
            return c

        def zwait(gblk, c):
            y_copy_abs(gblk, 0).wait()
            return c
        lax.fori_loop(nused_ref[0], MOE_NBLK, zstart, 0)
        lax.fori_loop(nused_ref[0], MOE_NBLK, zwait, 0)


def _moe(item_e, item_b0, item_nb, item_valid, nused, xb, w_gu, b_gu3, w_down, b_down3):
    def f_eff(f, ival, i):
        return jnp.where(ival[i] > 0, f, MOE_NF - 1)

    in_specs = [
        pl.BlockSpec(memory_space=pl.ANY),
        pl.BlockSpec((1, D_MODEL, MOE_TF), lambda i, f, ie, ib0, inb, iv, nu: (ie[i], 0, f_eff(f, iv, i))),
        pl.BlockSpec((1, D_MODEL, MOE_TF),
                     lambda i, f, ie, ib0, inb, iv, nu: (ie[i], 0, MOE_NF + f_eff(f, iv, i))),
        pl.BlockSpec((1, MOE_TF, D_MODEL), lambda i, f, ie, ib0, inb, iv, nu: (ie[i], f_eff(f, iv, i), 0)),
        pl.BlockSpec((1, 1, MOE_TF), lambda i, f, ie, ib0, inb, iv, nu: (ie[i], 0, f_eff(f, iv, i))),
        pl.BlockSpec((1, 1, MOE_TF),
                     lambda i, f, ie, ib0, inb, iv, nu: (ie[i], 0, MOE_NF + f_eff(f, iv, i))),
        pl.BlockSpec((1, 1, D_MODEL), lambda i, f, ie, ib0, inb, iv, nu: (ie[i], 0, 0)),
    ]
    return pl.pallas_call(
        _moe_kernel,
        grid_spec=pltpu.PrefetchScalarGridSpec(
            num_scalar_prefetch=5,
            grid=(MOE_NITEMS, MOE_NF),
            in_specs=in_specs,
            out_specs=pl.BlockSpec(memory_space=pl.ANY),
            scratch_shapes=[pltpu.VMEM((2, MOE_MC, D_MODEL), BF16),
                            pltpu.VMEM((MOE_MC, D_MODEL), F32),
                            pltpu.VMEM((D_MODEL, MOE_TF), BF16),
                            pltpu.VMEM((D_MODEL, MOE_TF), BF16),
                            pltpu.VMEM((MOE_TF, D_MODEL), BF16),
                            pltpu.VMEM((2, MOE_SB * ROW_LINES, LANES), F32),
                            pltpu.SemaphoreType.DMA((2,)),
                            pltpu.SemaphoreType.DMA((2,))]),
        out_shape=jax.ShapeDtypeStruct((MOE_ROWS * ROW_LINES, LANES), F32),
        compiler_params=_cparams(("arbitrary", "arbitrary")),
        name="moe",
    )(item_e, item_b0, item_nb, item_valid, nused, xb, w_gu, w_gu, w_down, b_gu3, b_gu3, b_down3)


FIN_TOK = 128
FIN_ROWS = FIN_TOK * TOP_K


FIN_TILES_P = T_PROMPT // FIN_TOK


def _final_kernel(dest_ref, yb_hbm, x1_ref, tg_ref, gt2p_ref, gt2s_ref, gf_ref, yp_ref, ys_ref, stage_ref, sem):
    i = pl.program_id(0)
    n = pl.num_programs(0)
    slot = i % 2

    def start_tile(tile, slot):
        base = tile * FIN_ROWS

        def body(j, c):
            row = dest_ref[base + j]
            dst = (j % TOP_K) * FIN_TOK + j // TOP_K
            pltpu.make_async_copy(yb_hbm.at[pl.ds(row * ROW_LINES, ROW_LINES)],
                                  stage_ref.at[slot, pl.ds(dst * ROW_LINES, ROW_LINES)],
                                  sem.at[slot]).start()
            return c
        lax.fori_loop(0, FIN_ROWS, body, 0, unroll=GATHER_UNROLL)

    @pl.when(i == 0)
    def _():
        start_tile(0, 0)

    @pl.when(i + 1 < n)
    def _():
        start_tile(i + 1, 1 - slot)

    pltpu.make_async_copy(yb_hbm.at[pl.ds(0, FIN_ROWS * ROW_LINES)], stage_ref.at[slot], sem.at[slot]).wait()
    tg = tg_ref[...]
    gate = [jnp.broadcast_to(tg[:, k:k + 1], (FIN_TOK, LANES)) for k in range(TOP_K)]
    pieces = []
    for c in range(ROW_LINES):
        fc = gate[0] * _load_row_tiled(stage_ref.at[slot], 0, FIN_TOK, c)
        for k in range(1, TOP_K):
            fc = fc + gate[k] * _load_row_tiled(stage_ref.at[slot], k * FIN_TOK, FIN_TOK, c)
        pieces.append(fc)
    f = jnp.concatenate(pieces, axis=-1)

    def out(gt2_ref):
        return _rms(x1_ref[...] + (1.0 + _modval(gt2_ref)) * f, gf_ref[...])

    @pl.when(i < FIN_TILES_P)
    def _():
        yp_ref[...] = out(gt2p_ref)

    @pl.when(i == FIN_TILES_P)
    def _():
        ys_ref[...] = out(gt2s_ref)


def _final(dest, yb, x1_all, tg_all, mod_p3, mod_s, gf):
    def ptile(i):
        return jnp.minimum(i, FIN_TILES_P - 1)

    return pl.pallas_call(
        _final_kernel,
        grid_spec=pltpu.PrefetchScalarGridSpec(
            num_scalar_prefetch=1,
            grid=(FIN_TILES_P + 1,),
            in_specs=[pl.BlockSpec(memory_space=pl.ANY),
                      pl.BlockSpec((FIN_TOK, D_MODEL), lambda i, d: (i, 0)),
                      pl.BlockSpec((FIN_TOK, LANES), lambda i, d: (i, 0)),
                      pl.BlockSpec((1, 1, D_MODEL), lambda i, d: (ptile(i) // (SEQ // FIN_TOK), 0, 5)),
                      pl.BlockSpec((DEC_BATCH, D_MODEL), lambda i, d: (0, 5)),
                      pl.BlockSpec((1, D_MODEL), lambda i, d: (0, 0))],
            out_specs=[pl.BlockSpec((FIN_TOK, D_MODEL), lambda i, d: (ptile(i), 0)),
                       pl.BlockSpec((DEC_BATCH, D_MODEL), lambda i, d: (0, 0))],
            scratch_shapes=[pltpu.VMEM((2, FIN_ROWS * ROW_LINES, LANES), F32),
                            pltpu.SemaphoreType.DMA((2,))]),
        out_shape=[jax.ShapeDtypeStruct((T_PROMPT, D_MODEL), F32),
                   jax.ShapeDtypeStruct((DEC_BATCH, D_MODEL), F32)],
        compiler_params=_cparams(("arbitrary",)),
        name="final",
    )(dest, yb, x1_all, tg_all, mod_p3, mod_s, gf)


def _work_items(counts, blk_start):
    nblk = (counts + MOE_SB - 1) // MOE_SB
    n_items_e = (nblk + MOE_ITEM_BLOCKS - 1) // MOE_ITEM_BLOCKS
    item_end = jnp.cumsum(n_items_e)
    item_start = item_end - n_items_e
    total_items = item_end[-1]
    it = jnp.arange(MOE_NITEMS, dtype=jnp.int32)
    it_c = jnp.minimum(it, total_items - 1)
    item_e = jnp.minimum(jnp.sum((item_end[None, :] <= it_c[:, None]).astype(jnp.int32), axis=1),
                         N_EXPERTS - 1).astype(jnp.int32)
    jj = it_c - item_start[item_e]
    item_valid = (it < total_items).astype(jnp.int32)
    item_b0 = (blk_start[item_e] + jj * MOE_ITEM_BLOCKS).astype(jnp.int32)
    item_nb = jnp.where(item_valid > 0,
                        jnp.minimum(MOE_ITEM_BLOCKS, nblk[item_e] - jj * MOE_ITEM_BLOCKS), 0).astype(jnp.int32)
    nused = (blk_start[-1] + nblk[-1]).reshape(1).astype(jnp.int32)
    return item_e, item_b0, item_nb, item_valid, nused


def kernel(x_prompt, x_sample, c_prompt, c_sample, state_pool, state_gla, w_mod, b_mod, g_norm1, w_in,
           w_alpha, b_alpha, w_pool, pool_scale, g_head, w_out, g_norm2, w_router, b_router, w_gu, b_gu,
           w_down, b_down, g_final):
    x_p = x_prompt.reshape(T_PROMPT, D_MODEL)
    x_s = x_sample.reshape(DEC_BATCH, D_MODEL)

    c_all = jnp.concatenate([c_prompt, c_sample, jnp.zeros((4, D_MODEL), F32)], axis=0)
    mod = _mod(c_all, w_mod[0], b_mod[0])
    mod_p3 = mod[:BATCH].reshape(BATCH, 1, 6 * D_MODEL)
    mod_s = mod[BATCH:BATCH + DEC_BATCH]

    h_all = _prenorm(x_p, x_s, g_norm1, mod_p3, mod_s)

    w_in0 = w_in[0]
    w_a = jnp.pad(w_in0[:, COL_A:COL_G], ((0, 0), (0, LANES - ALPHA_RANK)))
    w_g = w_in0[:, COL_G:]
    u_all = _inproj(h_all, w_in0, COL_U, D_POOL, F32, 512, "inproj_u")
    qkvr = _inproj(h_all, w_in0, COL_Q, COL_A - COL_Q, BF16, 512, "inproj_qkvr")
    acode = _inproj(h_all, w_a, 0, LANES, F32, LANES, "inproj_a")
    gates = _inproj(h_all, w_g, 0, 2 * D_MODEL, BF16, 512, "inproj_g")

    wal = jnp.pad(w_alpha[0], ((0, LANES - ALPHA_RANK), (0, 0))).astype(BF16)
    bal = b_alpha[0].reshape(1, D_K)
    wpool = w_pool[0].astype(BF16)
    ghead = g_head[0].reshape(1, DV_HEAD)

    m_p, new_gla_p = _mix_prompt(u_all, qkvr, acode, gates, wal, bal, wpool, pool_scale, ghead)

    sp2 = state_pool[0].reshape(DEC_BATCH, POOL_BUF * D_POOL)
    a_s, g_s = _sample_pre(u_all, sp2, acode, wal, bal, wpool, pool_scale)
    zs = qkvr[T_PROMPT:].astype(F32)
    q_s, k_s = zs[:, 0:D_K], zs[:, D_K:2 * D_K]
    gkq = jnp.stack([g_s, k_s, q_s] + [jnp.zeros_like(g_s)] * 5, axis=1)
    v3 = zs[:, 2 * D_K:2 * D_K + D_V].reshape(DEC_BATCH, 1, D_V)
    r3 = zs[:, 2 * D_K + D_V:].reshape(DEC_BATCH, 1, D_V)
    gates3 = gates[T_PROMPT:].astype(F32).reshape(DEC_BATCH, 1, 2 * D_MODEL)
    new_gla_s, m_s3 = _sample_state(gkq, v3, r3, a_s.reshape(DEC_BATCH, 1, D_MODEL), gates3,
                                    state_gla[0], ghead)
    m_s = m_s3.reshape(DEC_BATCH, D_MODEL).astype(BF16)

    wo = w_out[0].astype(BF16)
    wr = jnp.pad(w_router[0], ((0, 0), (0, LANES - N_EXPERTS))).astype(BF16)
    br = jnp.concatenate([b_router[0], jnp.full((LANES - N_EXPERTS,), -1e30, F32)]).reshape(1, LANES)
    x1_all, h2_tiled, ti_all, tg_all = _outproj(m_p, m_s, x_p, x_s, wo, mod_p3, mod_s, g_norm2, wr, br)

    dest_t, counts_t, bstart_t = _route(ti_all)
    dest = dest_t[:, :TOP_K].reshape(-1)
    src_tok = jnp.zeros((MOE_ROWS,), jnp.int32).at[dest].set(
        jnp.arange(N_ASSIGN, dtype=jnp.int32) // TOP_K, unique_indices=True)
    item_e, item_b0, item_nb, item_valid, nused = _work_items(counts_t[0, :N_EXPERTS], bstart_t[0, :N_EXPERTS])

    xb = _dispatch(src_tok, h2_tiled)
    yb = _moe(item_e, item_b0, item_nb, item_valid, nused, xb, w_gu[0],
              b_gu[0].reshape(N_EXPERTS, 1, 2 * D_FF), w_down[0], b_down[0].reshape(N_EXPERTS, 1, D_MODEL))

    y_p, y_s = _final(dest, yb, x1_all, tg_all, mod_p3, mod_s, g_final.reshape(1, D_MODEL))

    u_p = u_all[:T_PROMPT].reshape(BATCH, SEQ, D_POOL)
    new_pool_p = u_p[:, SEQ - POOL_BUF:, :][None]
    new_pool_s = jnp.concatenate([state_pool[0][:, 1:, :], u_all[T_PROMPT:][:, None, :]], axis=1)[None]
    return (y_p.reshape(BATCH, SEQ, D_MODEL), y_s.reshape(DEC_BATCH, 1, D_MODEL),
            new_pool_p, new_gla_p[None], new_pool_s, new_gla_s[None])
```

```python
import functools

import jax
import jax.numpy as jnp
from jax import lax
from jax.experimental import pallas as pl
from jax.experimental.pallas import tpu as pltpu

F32 = jnp.float32
BF16 = jnp.bfloat16

D_MODEL = 2048
BATCH = 4
SEQ = 2048
DEC_BATCH = 128
PAST_LEN = 16384
POOL_WINDOWS = (2, 4, 8, 16)
D_POOL = D_MODEL // 2
POOL_GROUP_IN = D_POOL // 4
POOL_GROUP_OUT = D_MODEL // 4
POOL_BUF = 15
GLA_HEADS = 4
D_K = D_MODEL // 2
D_V = D_MODEL
DK_HEAD = D_K // GLA_HEADS
DV_HEAD = D_V // GLA_HEADS
ALPHA_RANK = 16
GATE_TAU = 16.0
N_EXPERTS = 32
TOP_K = 4
D_FF = D_MODEL
SWIGLU_LIMIT = 7.0
SWIGLU_ALPHA = 1.702
EPS = 1e-6

LANES = 128
T_PROMPT = BATCH * SEQ
T_ALL = T_PROMPT + DEC_BATCH
N_ASSIGN = T_ALL * TOP_K

COL_U, COL_Q, COL_K, COL_V, COL_R = 0, 1024, 2048, 3072, 5120
COL_A = 7168
COL_G = COL_A + ALPHA_RANK

GLA_CHUNK = 128
GLA_SAFE_LOG_DECAY = -60.0
GLA_SUB = 16
MIX_ROWS = 256

MOE_SB = 256
MOE_ITEM_BLOCKS = 6
MOE_MC = MOE_SB * MOE_ITEM_BLOCKS
MOE_TF = 256
MOE_NF = D_FF // MOE_TF
MOE_NBLK = -(-(N_ASSIGN + N_EXPERTS * (MOE_SB - 1)) // MOE_SB)
MOE_ROWS = MOE_NBLK * MOE_SB
MOE_NITEMS = N_EXPERTS + -(-MOE_NBLK // MOE_ITEM_BLOCKS)

VMEM_LIMIT = 56 * 1024 * 1024


def _cparams(sem, vmem=VMEM_LIMIT):
    return pltpu.CompilerParams(dimension_semantics=sem, vmem_limit_bytes=vmem)


def _rms(x, g):
    return x * lax.rsqrt(jnp.mean(x * x, axis=-1, keepdims=True) + EPS) * g


def _sigmoid(x):
    return 1.0 / (1.0 + jnp.exp(-x))


def _mod_kernel(c_ref, w_ref, b_ref, o_ref):
    c = c_ref[...]
    s = (c * _sigmoid(c)).astype(BF16)
    o_ref[...] = jnp.dot(s, w_ref[...].astype(BF16), preferred_element_type=F32) + b_ref[...]


def _mod(c_all, w_mod, b_mod):
    m = c_all.shape[0]
    tn = 1024
    return pl.pallas_call(
        _mod_kernel,
        grid=(6 * D_MODEL // tn,),
        in_specs=[pl.BlockSpec((m, D_MODEL), lambda j: (0, 0)),
                  pl.BlockSpec((D_MODEL, tn), lambda j: (0, j)),
                  pl.BlockSpec((1, tn), lambda j: (0, j))],
        out_specs=pl.BlockSpec((m, tn), lambda j: (0, j)),
        out_shape=jax.ShapeDtypeStruct((m, 6 * D_MODEL), F32),
        compiler_params=_cparams(("arbitrary",)),
        name="mod",
    )(c_all, w_mod, b_mod.reshape(1, -1))


ROW_TILE = 512
ROW_TILES_P = T_PROMPT // ROW_TILE
_TILES_PER_BATCH = SEQ // ROW_TILE


def _prompt_tile(i):
    return jnp.minimum(i, ROW_TILES_P - 1)


def _prompt_rows_spec(width):
    return pl.BlockSpec((ROW_TILE, width), lambda i: (_prompt_tile(i), 0))


def _sample_rows_spec(width):
    return pl.BlockSpec((DEC_BATCH, width), lambda i: (0, 0))


def _mod_spec_prompt(chunk):
    return pl.BlockSpec((1, 1, D_MODEL), lambda i: (_prompt_tile(i) // _TILES_PER_BATCH, 0, chunk))


def _mod_spec_sample(chunk):
    return pl.BlockSpec((DEC_BATCH, D_MODEL), lambda i: (0, chunk))


def _modval(ref):
    v = ref[...]
    return v.reshape(v.shape[-2], v.shape[-1])


def _prenorm_kernel(xp_ref, xs_ref, g_ref, shp_ref, scp_ref, shs_ref, scs_ref, o_ref):
    i = pl.program_id(0)

    def body(x_ref, sh_ref, sc_ref):
        y = _rms(x_ref[...], g_ref[...])
        return (y * (1.0 + _modval(sc_ref)) + _modval(sh_ref)).astype(BF16)

    @pl.when(i < ROW_TILES_P)
    def _():
        o_ref[...] = body(xp_ref, shp_ref, scp_ref)

    @pl.when(i == ROW_TILES_P)
    def _():
        o_ref[0:DEC_BATCH, :] = body(xs_ref, shs_ref, scs_ref)


def _prenorm(x_p, x_s, g, mod_p3, mod_s):
    return pl.pallas_call(
        _prenorm_kernel,
        grid=(ROW_TILES_P + 1,),
        in_specs=[_prompt_rows_spec(D_MODEL), _sample_rows_spec(D_MODEL),
                  pl.BlockSpec((1, D_MODEL), lambda i: (0, 0)),
                  _mod_spec_prompt(0), _mod_spec_prompt(1), _mod_spec_sample(0), _mod_spec_sample(1)],
        out_specs=pl.BlockSpec((ROW_TILE, D_MODEL), lambda i: (i, 0)),
        out_shape=jax.ShapeDtypeStruct((T_ALL, D_MODEL), BF16),
        compiler_params=_cparams(("arbitrary",)),
        name="prenorm",
    )(x_p, x_s, g, mod_p3, mod_p3, mod_s, mod_s)


def _inproj_kernel(h_ref, wt_ref, o_ref, wb_ref):
    @pl.when(pl.program_id(1) == 0)
    def _():
        wb_ref[...] = wt_ref[...].astype(BF16)

    o_ref[...] = lax.dot_general(h_ref[...], wb_ref[...], (((1,), (1,)), ((), ())),
                                 preferred_element_type=F32).astype(o_ref.dtype)


def _inproj(h_all, wt, row0, nrows, out_dtype, tn, name):
    tm = 1664
    joff = row0 // tn
    return pl.pallas_call(
        _inproj_kernel,
        grid=(nrows // tn, T_ALL // tm),
        in_specs=[pl.BlockSpec((tm, D_MODEL), lambda j, i: (i, 0)),
                  pl.BlockSpec((tn, D_MODEL), lambda j, i: (j + joff, 0))],
        out_specs=pl.BlockSpec((tm, tn), lambda j, i: (i, j)),
        out_shape=jax.ShapeDtypeStruct((T_ALL, nrows), out_dtype),
        scratch_shapes=[pltpu.VMEM((tn, D_MODEL), BF16)],
        compiler_params=_cparams(("arbitrary", "arbitrary")),
        name=name,
    )(h_all, wt)


def _log_decay(ac, wal, bal):
    x = jnp.dot(ac.astype(BF16), wal, preferred_element_type=F32) + bal
    return (jnp.minimum(x, 0.0) - jnp.log1p(jnp.exp(-jnp.abs(x)))) / GATE_TAU


def _mixp_kernel(u_ref, q_ref, k_ref, v_ref, r_ref, ac_ref, ga_ref, gb_ref, wal_ref, bal_ref,
                 wpool_ref, pscale_ref, ghead_ref, m_ref, st_ref,
                 ext_ref, state_ref, b_ref, q32_ref, k32_ref, v32_ref, o_ref):
    R = MIX_ROWS
    h = pl.program_id(1)
    t = pl.program_id(2)

    @pl.when(t == 0)
    def _():
        ext_ref[0:16, :] = jnp.zeros((16, POOL_GROUP_IN), F32)
        state_ref[...] = jnp.zeros_like(state_ref)

    u = u_ref[...]
    ext_ref[16:16 + R, :] = u
    e = ext_ref[...]
    s2 = e + pltpu.roll(e, 1, axis=0)
    s4 = s2 + pltpu.roll(s2, 2, axis=0)
    s8 = s4 + pltpu.roll(s4, 4, axis=0)
    s16 = s8 + pltpu.roll(s8, 8, axis=0)
    s = jnp.where(h == 0, s2, jnp.where(h == 1, s4, jnp.where(h == 2, s8, s16)))[16:, :]
    window = jnp.left_shift(2, h)
    pos = t * R + lax.broadcasted_iota(jnp.int32, (R, 1), 0)
    cnt = jnp.minimum(pos + 1, window).astype(F32)
    p = s / cnt - u
    a_out = jnp.dot(p.astype(BF16), wpool_ref[0], preferred_element_type=F32) * pscale_ref[...]
    ext_ref[0:16, :] = ext_ref[R:R + 16, :]

    C = GLA_CHUNK
    g = _log_decay(ac_ref[...], wal_ref[...], bal_ref[...])
    ri = lax.broadcasted_iota(jnp.int32, (C, C), 0)
    ci = lax.broadcasted_iota(jnp.int32, (C, C), 1)
    causal = ri >= ci
    tri = jnp.where(causal, 1.0, 0.0).astype(BF16)
    g_hi = g.astype(BF16)
    g_lo = (g - g_hi.astype(F32)).astype(BF16)
    for c in range(R // C):
        b_ref[c * C:(c + 1) * C, :] = (
            jnp.dot(tri, g_hi[c * C:(c + 1) * C, :], preferred_element_type=F32)
            + jnp.dot(tri, g_lo[c * C:(c + 1) * C, :], preferred_element_type=F32))
    q32_ref[...] = q_ref[...].astype(F32) * (DK_HEAD ** -0.5)
    k32_ref[...] = k_ref[...].astype(F32)
    v32_ref[...] = v_ref[...].astype(F32)

    nt_dims = (((1,), (1,)), ((), ()))
    tn_dims = (((0,), (0,)), ((), ()))
    sub_iota = lax.broadcasted_iota(jnp.int32, (GLA_SUB, 1), 0)

    for c in range(R // C):
        r0 = c * C
        bc = b_ref[r0:r0 + C, :]
        qc = q32_ref[r0:r0 + C, :]
        kc = k32_ref[r0:r0 + C, :]
        vcb = v32_ref[r0:r0 + C, :].astype(BF16)
        st = state_ref[...]
        b_last = bc[C - 1:C, :]
        e_last = jnp.exp(b_last)
        safe = jnp.min(b_last) >= GLA_SAFE_LOG_DECAY
        qt = (qc * jnp.exp(bc)).astype(BF16)
        o_inter = lax.dot_general(qt, st.astype(BF16), nt_dims, preferred_element_type=F32)

        @pl.when(safe)
        def _(r0=r0, bc=bc, kc=kc, vcb=vcb, st=st, e_last=e_last, qt=qt, o_inter=o_inter):
            kt = kc * jnp.exp(-bc)
            att = lax.dot_general(qt, kt.astype(BF16), nt_dims, preferred_element_type=F32)
            att = jnp.where(causal, att, 0.0).astype(BF16)
            o_ref[r0:r0 + C, :] = o_inter + jnp.dot(att, vcb, preferred_element_type=F32)
            kd = (kt * e_last).astype(BF16)
            state_ref[...] = st * e_last + lax.dot_general(vcb, kd, tn_dims, preferred_element_type=F32)

        @pl.when(jnp.logical_not(safe))
        def _(r0=r0, bc=bc, qc=qc, kc=kc, vcb=vcb, st=st, b_last=b_last, e_last=e_last, o_inter=o_inter):
            o_ref[r0:r0 + C, :] = o_inter
            for i in range(C // GLA_SUB):
                lo = i * GLA_SUB
                bsub = bc[lo:lo + GLA_SUB, :]
                qsub = qc[lo:lo + GLA_SUB, :]
                acc = jnp.zeros((GLA_SUB, DV_HEAD), F32)
                if i > 0:
                    ref_row = bc[lo:lo + 1, :]
                    qi = (qsub * jnp.exp(bsub - ref_row)).astype(BF16)
                    ki = (kc[0:lo, :] * jnp.exp(ref_row - bc[0:lo, :])).astype(BF16)
                    att = lax.dot_general(qi, ki, nt_dims, preferred_element_type=F32)
                    acc = jnp.dot(att.astype(BF16), vcb[0:lo, :], preferred_element_type=F32)

                def diag_body(j, acc, lo=lo, bsub=bsub, qsub=qsub):
                    row = r0 + lo + j
                    bs = b_ref[pl.ds(row, 1), :]
                    ks = k32_ref[pl.ds(row, 1), :]
                    vs = v32_ref[pl.ds(row, 1), :]
                    dec = jnp.exp(jnp.where(sub_iota >= j, bsub - bs, -jnp.inf))
                    col = jnp.sum(qsub * dec * ks, axis=-1, keepdims=True)
                    return acc + col * vs

                acc = lax.fori_loop(0, GLA_SUB, diag_body, acc)
                o_ref[r0 + lo:r0 + lo + GLA_SUB, :] += acc
            kd = (kc * jnp.exp(b_last - bc)).astype(BF16)
            state_ref[...] = st * e_last + lax.dot_general(vcb, kd, tn_dims, preferred_element_type=F32)

    o = _rms(o_ref[...], ghead_ref[...])
    r = r_ref[...].astype(F32)
    b_out = o * (r * _sigmoid(r))
    m = _sigmoid(ga_ref[...].astype(F32)) * a_out + _sigmoid(gb_ref[...].astype(F32)) * b_out
    m_ref[...] = m.astype(BF16)

    @pl.when(t == pl.num_programs(2) - 1)
    def _():
        st_ref[0, 0] = state_ref[...].T


def _mix_prompt(u_all, qkvr, acode, gates, wal, bal, wpool, pscale, ghead):
    R = MIX_ROWS
    nt = SEQ // R

    def rows(b, h, t):
        return b * nt + t

    in_specs = [
        pl.BlockSpec((R, POOL_GROUP_IN), lambda b, h, t: (rows(b, h, t), h)),
        pl.BlockSpec((R, DK_HEAD), lambda b, h, t: (rows(b, h, t), h)),
        pl.BlockSpec((R, DK_HEAD), lambda b, h, t: (rows(b, h, t), GLA_HEADS + h)),
        pl.BlockSpec((R, DV_HEAD), lambda b, h, t: (rows(b, h, t), GLA_HEADS + h)),
        pl.BlockSpec((R, DV_HEAD), lambda b, h, t: (rows(b, h, t), 2 * GLA_HEADS + h)),
        pl.BlockSpec((R, LANES), lambda b, h, t: (rows(b, h, t), 0)),
        pl.BlockSpec((R, DV_HEAD), lambda b, h, t: (rows(b, h, t), h)),
        pl.BlockSpec((R, DV_HEAD), lambda b, h, t: (rows(b, h, t), GLA_HEADS + h)),
        pl.BlockSpec((LANES, DK_HEAD), lambda b, h, t: (0, h)),
        pl.BlockSpec((1, DK_HEAD), lambda b, h, t: (0, h)),
        pl.BlockSpec((1, POOL_GROUP_IN, POOL_GROUP_OUT), lambda b, h, t: (h, 0, 0)),
        pl.BlockSpec((1, POOL_GROUP_OUT), lambda b, h, t: (0, h)),
        pl.BlockSpec((1, DV_HEAD), lambda b, h, t: (0, 0)),
    ]
    out_specs = [
        pl.BlockSpec((R, DV_HEAD), lambda b, h, t: (rows(b, h, t), h)),
        pl.BlockSpec((1, 1, DK_HEAD, DV_HEAD), lambda b, h, t: (b, h, 0, 0)),
    ]
    return pl.pallas_call(
        _mixp_kernel,
        grid=(BATCH, GLA_HEADS, nt),
        in_specs=in_specs,
        out_specs=out_specs,
        out_shape=[jax.ShapeDtypeStruct((T_PROMPT, D_MODEL), BF16),
                   jax.ShapeDtypeStruct((BATCH, GLA_HEADS, DK_HEAD, DV_HEAD), F32)],
        scratch_shapes=[pltpu.VMEM((16 + R, POOL_GROUP_IN), F32),
                        pltpu.VMEM((DV_HEAD, DK_HEAD), F32),
                        pltpu.VMEM((R, DK_HEAD), F32),
                        pltpu.VMEM((R, DK_HEAD), F32),
                        pltpu.VMEM((R, DK_HEAD), F32),
                        pltpu.VMEM((R, DV_HEAD), F32),
                        pltpu.VMEM((R, DV_HEAD), F32)],
        compiler_params=_cparams(("arbitrary", "arbitrary", "arbitrary")),
        name="mix_prompt",
    )(u_all, qkvr, qkvr, qkvr, qkvr, acode, gates, gates, wal, bal, wpool, pscale, ghead)


def _spre_kernel(u_ref, sp_ref, ac_ref, wal_ref, bal_ref, wpool_ref, pscale_ref, a_ref, g_ref):
    u = u_ref[...]
    for gi, w in enumerate(POOL_WINDOWS):
        lo = gi * POOL_GROUP_IN
        s = u[:, lo:lo + POOL_GROUP_IN]
        for j in range(POOL_BUF - (w - 1), POOL_BUF):
            s = s + sp_ref[j, :, lo:lo + POOL_GROUP_IN]
        cnt = float(min(PAST_LEN + 1, w))
        p = s / cnt - u[:, lo:lo + POOL_GROUP_IN]
        a = jnp.dot(p.astype(BF16), wpool_ref[gi], preferred_element_type=F32)
        olo = gi * POOL_GROUP_OUT
        a_ref[:, olo:olo + POOL_GROUP_OUT] = a * pscale_ref[:, olo:olo + POOL_GROUP_OUT]
    g_ref[...] = _log_decay(ac_ref[...], wal_ref[...], bal_ref[...])


def _sample_pre(u_all, sp2, acode, wal, bal, wpool, pscale):
    blk = T_PROMPT // DEC_BATCH
    return pl.pallas_call(
        _spre_kernel,
        grid=(1,),
        in_specs=[pl.BlockSpec((DEC_BATCH, D_POOL), lambda i: (blk, 0)),
                  pl.BlockSpec((POOL_BUF, DEC_BATCH, D_POOL), lambda i: (0, 0, 0)),
                  pl.BlockSpec((DEC_BATCH, LANES), lambda i: (blk, 0)),
                  pl.BlockSpec((LANES, D_K), lambda i: (0, 0)),
                  pl.BlockSpec((1, D_K), lambda i: (0, 0)),
                  pl.BlockSpec((4, POOL_GROUP_IN, POOL_GROUP_OUT), lambda i: (0, 0, 0)),
                  pl.BlockSpec((1, D_MODEL), lambda i: (0, 0))],
        out_specs=[pl.BlockSpec((DEC_BATCH, D_MODEL), lambda i: (0, 0)),
                   pl.BlockSpec((DEC_BATCH, D_K), lambda i: (0, 0))],
        out_shape=[jax.ShapeDtypeStruct((DEC_BATCH, D_MODEL), F32),
                   jax.ShapeDtypeStruct((DEC_BATCH, D_K), F32)],
        compiler_params=_cparams(("arbitrary",)),
        name="sample_pre",
    )(u_all, sp2, acode, wal, bal, wpool, pscale)


def _sstate_kernel(gkq_ref, v_ref, r_ref, a_ref, gt_ref, s_ref, ghead_ref, so_ref, m_ref):
    x = gkq_ref[0]
    rowi = lax.broadcasted_iota(jnp.int32, x.shape, 0)
    x = jnp.where(rowi == 0, jnp.exp(x), jnp.where(rowi == 2, x * (DK_HEAD ** -0.5), x))
    xt = x.T
    v = v_ref[0]
    r = r_ref[0]
    a_out = a_ref[0]
    gates = gt_ref[0]
    for h in range(GLA_HEADS):
        cols = xt[h * DK_HEAD:(h + 1) * DK_HEAD, :]
        dec, kcol, qcol = cols[:, 0:1], cols[:, 1:2], cols[:, 2:3]
        lo = h * DV_HEAD
        vrow = v[:, lo:lo + DV_HEAD]
        s_new = dec * s_ref[0, h] + kcol * vrow
        so_ref[0, h] = s_new
        o = jnp.sum(qcol * s_new, axis=0, keepdims=True)
        o = _rms(o, ghead_ref[...])
        rr = r[:, lo:lo + DV_HEAD]
        b_out = o * (rr * _sigmoid(rr))
        m_ref[0, :, lo:lo + DV_HEAD] = (_sigmoid(gates[:, lo:lo + DV_HEAD]) * a_out[:, lo:lo + DV_HEAD]
                                        + _sigmoid(gates[:, D_MODEL + lo:D_MODEL + lo + DV_HEAD]) * b_out)


def _sample_state(gkq, v3, r3, a3, gates3, state, ghead):
    def row3(width):
        return pl.BlockSpec((1, 1, width), lambda i: (i, 0, 0))

    sspec = pl.BlockSpec((1, GLA_HEADS, DK_HEAD, DV_HEAD), lambda i: (i, 0, 0, 0))
    return pl.pallas_call(
        _sstate_kernel,
        grid=(DEC_BATCH,),
        in_specs=[pl.BlockSpec((1, 8, D_K), lambda i: (i, 0, 0)),
                  row3(D_V), row3(D_V), row3(D_MODEL), row3(2 * D_MODEL), sspec,
                  pl.BlockSpec((1, DV_HEAD), lambda i: (0, 0))],
        out_specs=[sspec, row3(D_MODEL)],
        out_shape=[jax.ShapeDtypeStruct((DEC_BATCH, GLA_HEADS, DK_HEAD, DV_HEAD), F32),
                   jax.ShapeDtypeStruct((DEC_BATCH, 1, D_MODEL), F32)],
        compiler_params=_cparams(("arbitrary",)),
        name="sample_state",
    )(gkq, v3, r3, a3, gates3, state, ghead)


def _outproj_kernel(mp_ref, ms_ref, xp_ref, xs_ref, wo_ref, gt1p_ref, sh2p_ref, sc2p_ref,
                    gt1s_ref, sh2s_ref, sc2s_ref, g2_ref, wr_ref, br_ref,
                    x1_ref, h2_ref, ti_ref, tg_ref):
    i = pl.program_id(0)

    def body(nrows, m_ref, x_ref, gt1_ref, sh2_ref, sc2_ref):
        y = jnp.dot(m_ref[...], wo_ref[...], preferred_element_type=F32)
        x1 = x_ref[...] + (1.0 + _modval(gt1_ref)) * y
        x1_ref[0:nrows, :] = x1
        h2 = _rms(x1, g2_ref[...]) * (1.0 + _modval(sc2_ref)) + _modval(sh2_ref)
        h2_ref[0:nrows, :] = h2
        logits = jnp.dot(h2.astype(BF16), wr_ref[...], preferred_element_type=F32) + br_ref[...]
        lane = lax.broadcasted_iota(jnp.int32, logits.shape, 1)
        vals, idxs = [], []
        cur = logits
        for _ in range(TOP_K):
            mx = jnp.max(cur, axis=-1, keepdims=True)
            ix = jnp.min(jnp.where(cur == mx, lane, LANES), axis=-1, keepdims=True)
            vals.append(mx)
            idxs.append(ix)
            cur = jnp.where(lane == ix, -jnp.inf, cur)
        exps = [jnp.exp(v - vals[0]) for v in vals]
        den = exps[0] + exps[1] + exps[2] + exps[3]
        ti = jnp.full(logits.shape, -1, jnp.int32)
        tg = jnp.zeros(logits.shape, F32)
        for k in range(TOP_K):
            ti = jnp.where(lane == k, idxs[k], ti)
            tg = jnp.where(lane == k, exps[k] / den, tg)
        ti_ref[0:nrows, :] = ti
        tg_ref[0:nrows, :] = tg

    @pl.when(i < ROW_TILES_P)
    def _():
        body(ROW_TILE, mp_ref, xp_ref, gt1p_ref, sh2p_ref, sc2p_ref)

    @pl.when(i == ROW_TILES_P)
    def _():
        body(DEC_BATCH, ms_ref, xs_ref, gt1s_ref, sh2s_ref, sc2s_ref)


def _outproj(m_p, m_s, x_p, x_s, wo, mod_p3, mod_s, g2, wr, br):
    return pl.pallas_call(
        _outproj_kernel,
        grid=(ROW_TILES_P + 1,),
        in_specs=[_prompt_rows_spec(D_MODEL), _sample_rows_spec(D_MODEL),
                  _prompt_rows_spec(D_MODEL), _sample_rows_spec(D_MODEL),
                  pl.BlockSpec((D_MODEL, D_MODEL), lambda i: (0, 0)),
                  _mod_spec_prompt(2), _mod_spec_prompt(3), _mod_spec_prompt(4),
                  _mod_spec_sample(2), _mod_spec_sample(3), _mod_spec_sample(4),
                  pl.BlockSpec((1, D_MODEL), lambda i: (0, 0)),
                  pl.BlockSpec((D_MODEL, LANES), lambda i: (0, 0)),
                  pl.BlockSpec((1, LANES), lambda i: (0, 0))],
        out_specs=[pl.BlockSpec((ROW_TILE, D_MODEL), lambda i: (i, 0)),
                   pl.BlockSpec((ROW_TILE, D_MODEL), lambda i: (i, 0)),
                   pl.BlockSpec((ROW_TILE, LANES), lambda i: (i, 0)),
                   pl.BlockSpec((ROW_TILE, LANES), lambda i: (i, 0))],
        out_shape=[jax.ShapeDtypeStruct((T_ALL, D_MODEL), F32),
                   jax.ShapeDtypeStruct((T_ALL, D_MODEL), F32),
                   jax.ShapeDtypeStruct((T_ALL, LANES), jnp.int32),
                   jax.ShapeDtypeStruct((T_ALL, LANES), F32)],
        compiler_params=_cparams(("arbitrary",)),
        name="outproj",
    )(m_p, m_s, x_p, x_s, wo, mod_p3, mod_p3, mod_p3, mod_s, mod_s, mod_s, g2, wr, br)


ROUTE_TILE = 640
ROUTE_TILES = T_ALL // ROUTE_TILE


def _route_kernel(ti_ref, dest_ref, cnt_ref, bstart_ref, run_ref, base_ref):
    phase = pl.program_id(0)
    t = pl.program_id(1)
    ti = ti_ref[...]
    lane = lax.broadcasted_iota(jnp.int32, (ROUTE_TILE, LANES), 1)
    onehots = [lane == ti[:, k:k + 1] for k in range(TOP_K)]
    per_tok = jnp.zeros((ROUTE_TILE, LANES), F32)
    for oh in onehots:
        per_tok = per_tok + jnp.where(oh, 1.0, 0.0)

    @pl.when((phase == 0) & (t == 0))
    def _():
        run_ref[...] = jnp.zeros_like(run_ref)

    @pl.when(phase == 0)
    def _():
        run_ref[...] += jnp.sum(per_tok, axis=0, keepdims=True)

    @pl.when((phase == 1) & (t == 0))
    def _():
        counts = run_ref[...]
        nblk = jnp.floor((counts + (MOE_SB - 1)) * (1.0 / MOE_SB))
        ri = lax.broadcasted_iota(jnp.int32, (LANES, LANES), 0)
        ci = lax.broadcasted_iota(jnp.int32, (LANES, LANES), 1)
        before = jnp.where(ri < ci, 1.0, 0.0).astype(BF16)
        nb8 = jnp.broadcast_to(nblk, (8, LANES)).astype(BF16)
        bstart = jnp.dot(nb8, before, preferred_element_type=F32)[0:1, :]
        cnt_ref[...] = counts.astype(jnp.int32)
        bstart_ref[...] = bstart.astype(jnp.int32)
        base_ref[...] = bstart * MOE_SB
        run_ref[...] = jnp.zeros_like(run_ref)

    @pl.when(phase == 1)
    def _():
        ri = lax.broadcasted_iota(jnp.int32, (ROUTE_TILE, ROUTE_TILE), 0)
        ci = lax.broadcasted_iota(jnp.int32, (ROUTE_TILE, ROUTE_TILE), 1)
        earlier = jnp.where(ri > ci, 1.0, 0.0).astype(BF16)
        prior = jnp.dot(earlier, per_tok.astype(BF16), preferred_element_type=F32)
        pos = prior + (run_ref[...] + base_ref[...])
        dest = jnp.full((ROUTE_TILE, LANES), -1, jnp.int32)
        for k, oh in enumerate(onehots):
            d = jnp.sum(jnp.where(oh, pos, 0.0), axis=-1, keepdims=True).astype(jnp.int32)
            dest = jnp.where(lane == k, d, dest)
        dest_ref[...] = dest
        run_ref[...] += jnp.sum(per_tok, axis=0, keepdims=True)


def _route(ti_all):
    return pl.pallas_call(
        _route_kernel,
        grid=(2, ROUTE_TILES),
        in_specs=[pl.BlockSpec((ROUTE_TILE, LANES), lambda p, t: (t, 0))],
        out_specs=[pl.BlockSpec((ROUTE_TILE, LANES), lambda p, t: (t * p, 0)),
                   pl.BlockSpec((1, LANES), lambda p, t: (0, 0)),
                   pl.BlockSpec((1, LANES), lambda p, t: (0, 0))],
        out_shape=[jax.ShapeDtypeStruct((T_ALL, LANES), jnp.int32),
                   jax.ShapeDtypeStruct((1, LANES), jnp.int32),
                   jax.ShapeDtypeStruct((1, LANES), jnp.int32)],
        scratch_shapes=[pltpu.VMEM((1, LANES), F32), pltpu.VMEM((1, LANES), F32)],
        compiler_params=_cparams(("arbitrary", "arbitrary")),
        name="route",
    )(ti_all)


GATHER_UNROLL = 8
GATHER_RING = 3


def _start_row_gather(n_rows, src_row, hbm_ref, dst_ref, sem, dst_row=lambda g, u: g * GATHER_UNROLL + u):
    def body(g, c):
        for u in range(GATHER_UNROLL):
            pltpu.make_async_copy(hbm_ref.at[pl.ds(src_row(g * GATHER_UNROLL + u), 1)],
                                  dst_ref.at[pl.ds(dst_row(g, u), 1)], sem).start(priority=u % 2)
        return c
    lax.fori_loop(0, n_rows // GATHER_UNROLL, body, 0)


def _wait_row_gather(n_rows, hbm_ref, dst_ref, sem):
    pltpu.make_async_copy(hbm_ref.at[pl.ds(0, n_rows)], dst_ref, sem).wait()


def _dispatch_kernel(src_ref, h2_hbm, xb_ref, stage_ref, sem):
    i = pl.program_id(0)
    n = pl.num_programs(0)

    def start_block(blk):
        slot = lax.rem(blk, GATHER_RING)
        _start_row_gather(MOE_SB, lambda j: src_ref[blk * MOE_SB + j], h2_hbm, stage_ref.at[slot], sem.at[slot])

    @pl.when(i == 0)
    def _():
        for b in range(GATHER_RING - 1):
            start_block(b)

    @pl.when(i + GATHER_RING - 1 < n)
    def _():
        start_block(i + GATHER_RING - 1)

    slot = lax.rem(i, GATHER_RING)
    _wait_row_gather(MOE_SB, h2_hbm, stage_ref.at[slot], sem.at[slot])
    xb_ref[...] = stage_ref[slot].astype(BF16)


def _dispatch(src_tok, h2_all):
    return pl.pallas_call(
        _dispatch_kernel,
        grid_spec=pltpu.PrefetchScalarGridSpec(
            num_scalar_prefetch=1,
            grid=(MOE_NBLK,),
            in_specs=[pl.BlockSpec(memory_space=pl.ANY)],
            out_specs=pl.BlockSpec((MOE_SB, D_MODEL), lambda i, src: (i, 0)),
            scratch_shapes=[pltpu.VMEM((GATHER_RING, MOE_SB, D_MODEL), F32),
                            pltpu.SemaphoreType.DMA((GATHER_RING,))]),
        out_shape=jax.ShapeDtypeStruct((MOE_ROWS, D_MODEL), BF16),
        compiler_params=_cparams(("arbitrary",)),
        name="dispatch",
    )(src_tok, h2_all)


def _moe_kernel(ie_ref, ib0_ref, inb_ref, ival_ref, nused_ref,
                xb_hbm, wg_ref, wu_ref, wd_ref, bg_ref, bu_ref, bd_ref, yb_hbm,
                xbuf, acc, wgb, wub, wdb, xsem, ysem):
    i = pl.program_id(0)
    f = pl.program_id(1)
    n_items = pl.num_programs(0)
    nb = inb_ref[i]
    slot = i % 2

    def x_copy(item, blk, slot):
        row0 = (ib0_ref[item] + blk) * MOE_SB
        return pltpu.make_async_copy(xb_hbm.at[pl.ds(row0, MOE_SB)],
                                     xbuf.at[slot, pl.ds(blk * MOE_SB, MOE_SB)], xsem.at[slot])

    def y_copy(gblk, r0):
        return pltpu.make_async_copy(acc.at[pl.ds(r0, MOE_SB)], yb_hbm.at[pl.ds(gblk * MOE_SB, MOE_SB)], ysem)

    def wait_y_copies(count):
        def body(b, c):
            y_copy(0, 0).wait()
            return c
        lax.fori_loop(0, count, body, 0)

    def start_item(item, slot):
        def body(blk, c):
            x_copy(item, blk, slot).start()
            return c
        lax.fori_loop(0, inb_ref[item], body, 0)

    @pl.when(f == 0)
    def _():
        @pl.when(i == 0)
        def _():
            start_item(0, 0)

        @pl.when(i + 1 < n_items)
        def _():
            start_item(i + 1, 1 - slot)

        def wbody(blk, c):
            x_copy(i, blk, slot).wait()
            return c
        lax.fori_loop(0, nb, wbody, 0)

        @pl.when(i > 0)
        def _():
            wait_y_copies(inb_ref[i - 1])

    wgb[...] = wg_ref[0].astype(BF16)
    wub[...] = wu_ref[0].astype(BF16)
    wdb[...] = wd_ref[0].astype(BF16)
    bg = bg_ref[0]
    bu = bu_ref[0]
    bd = bd_ref[0]

    def run_blocks(first, final):
        def body(blk, c):
            r0 = pl.multiple_of(blk * MOE_SB, MOE_SB)
            x = xbuf[slot, pl.ds(r0, MOE_SB), :]
            g = jnp.dot(x, wgb[...], preferred_element_type=F32) + bg
            u = jnp.dot(x, wub[...], preferred_element_type=F32) + bu
            xg = jnp.minimum(g, SWIGLU_LIMIT)
            xl = jnp.clip(u, -SWIGLU_LIMIT, SWIGLU_LIMIT)
            act = xg * _sigmoid(SWIGLU_ALPHA * xg) * (xl + 1.0)
            part = jnp.dot(act.astype(BF16), wdb[...], preferred_element_type=F32)
            if first:
                acc[pl.ds(r0, MOE_SB), :] = part + bd
            else:
                acc[pl.ds(r0, MOE_SB), :] += part
            if final:
                y_copy(ib0_ref[i] + blk, r0).start()
            return c
        lax.fori_loop(0, nb, body, 0)

    @pl.when(f == 0)
    def _():
        run_blocks(True, False)

    @pl.when((f > 0) & (f < MOE_NF - 1))
    def _():
        run_blocks(False, False)

    @pl.when(f == MOE_NF - 1)
    def _():
        run_blocks(False, True)

    @pl.when((f == MOE_NF - 1) & (i == n_items - 1))
    def _():
        wait_y_copies(nb)
        acc[0:MOE_SB, :] = jnp.zeros((MOE_SB, D_MODEL), F32)

        def zstart(gblk, c):
            y_copy(gblk, 0).start()
            return c
        lax.fori_loop(nused_ref[0], MOE_NBLK, zstart, 0)
        wait_y_copies(MOE_NBLK - nused_ref[0])


def _moe(item_e, item_b0, item_nb, item_valid, nused, xb, w_gu, b_gu3, w_down, b_down3):
    def f_eff(f, ival, i):
        return jnp.where(ival[i] > 0, f, MOE_NF - 1)

    in_specs = [
        pl.BlockSpec(memory_space=pl.ANY),
        pl.BlockSpec((1, D_MODEL, MOE_TF), lambda i, f, ie, ib0, inb, iv, nu: (ie[i], 0, f_eff(f, iv, i))),
        pl.BlockSpec((1, D_MODEL, MOE_TF),
                     lambda i, f, ie, ib0, inb, iv, nu: (ie[i], 0, MOE_NF + f_eff(f, iv, i))),
        pl.BlockSpec((1, MOE_TF, D_MODEL), lambda i, f, ie, ib0, inb, iv, nu: (ie[i], f_eff(f, iv, i), 0)),
        pl.BlockSpec((1, 1, MOE_TF), lambda i, f, ie, ib0, inb, iv, nu: (ie[i], 0, f_eff(f, iv, i))),
        pl.BlockSpec((1, 1, MOE_TF),
                     lambda i, f, ie, ib0, inb, iv, nu: (ie[i], 0, MOE_NF + f_eff(f, iv, i))),
        pl.BlockSpec((1, 1, D_MODEL), lambda i, f, ie, ib0, inb, iv, nu: (ie[i], 0, 0)),
    ]
    return pl.pallas_call(
        _moe_kernel,
        grid_spec=pltpu.PrefetchScalarGridSpec(
            num_scalar_prefetch=5,
            grid=(MOE_NITEMS, MOE_NF),
            in_specs=in_specs,
            out_specs=pl.BlockSpec(memory_space=pl.ANY),
            scratch_shapes=[pltpu.VMEM((2, MOE_MC, D_MODEL), BF16),
                            pltpu.VMEM((MOE_MC, D_MODEL), F32),
                            pltpu.VMEM((D_MODEL, MOE_TF), BF16),
                            pltpu.VMEM((D_MODEL, MOE_TF), BF16),
                            pltpu.VMEM((MOE_TF, D_MODEL), BF16),
                            pltpu.SemaphoreType.DMA((2,)),
                            pltpu.SemaphoreType.DMA(())]),
        out_shape=jax.ShapeDtypeStruct((MOE_ROWS, D_MODEL), F32),
        compiler_params=_cparams(("arbitrary", "arbitrary")),
        name="moe",
    )(item_e, item_b0, item_nb, item_valid, nused, xb, w_gu, w_gu, w_down, b_gu3, b_gu3, b_down3)


FIN_TOK = 128
FIN_ROWS = FIN_TOK * TOP_K


FIN_TILES_P = T_PROMPT // FIN_TOK


def _final_kernel(dest_ref, yb_hbm, x1_ref, tg_ref, gt2p_ref, gt2s_ref, gf_ref, yp_ref, ys_ref, stage_ref, sem):
    i = pl.program_id(0)
    n = pl.num_programs(0)

    def start_tile(tile):
        slot = lax.rem(tile, GATHER_RING)
        _start_row_gather(FIN_ROWS, lambda j: dest_ref[tile * FIN_ROWS + j], yb_hbm, stage_ref.at[slot],
                          sem.at[slot],
                          dst_row=lambda g, u: (u % TOP_K) * FIN_TOK + g * (GATHER_UNROLL // TOP_K) + u // TOP_K)

    @pl.when(i == 0)
    def _():
        for b in range(GATHER_RING - 1):
            start_tile(b)

    @pl.when(i + GATHER_RING - 1 < n)
    def _():
        start_tile(i + GATHER_RING - 1)

    slot = lax.rem(i, GATHER_RING)
    _wait_row_gather(FIN_ROWS, yb_hbm, stage_ref.at[slot], sem.at[slot])
    tg = tg_ref[...]
    f = jnp.zeros((FIN_TOK, D_MODEL), F32)
    for k in range(TOP_K):
        f = f + tg[:, k:k + 1] * stage_ref[slot, k * FIN_TOK:(k + 1) * FIN_TOK, :]

    def out(gt2_ref):
        return _rms(x1_ref[...] + (1.0 + _modval(gt2_ref)) * f, gf_ref[...])

    @pl.when(i < FIN_TILES_P)
    def _():
        yp_ref[...] = out(gt2p_ref)

    @pl.when(i == FIN_TILES_P)
    def _():
        ys_ref[...] = out(gt2s_ref)


def _final(dest, yb, x1_all, tg_all, mod_p3, mod_s, gf):
    def ptile(i):
        return jnp.minimum(i, FIN_TILES_P - 1)

    return pl.pallas_call(
        _final_kernel,
        grid_spec=pltpu.PrefetchScalarGridSpec(
            num_scalar_prefetch=1,
            grid=(FIN_TILES_P + 1,),
            in_specs=[pl.BlockSpec(memory_space=pl.ANY),
                      pl.BlockSpec((FIN_TOK, D_MODEL), lambda i, d: (i, 0)),
                      pl.BlockSpec((FIN_TOK, LANES), lambda i, d: (i, 0)),
                      pl.BlockSpec((1, 1, D_MODEL), lambda i, d: (ptile(i) // (SEQ // FIN_TOK), 0, 5)),
                      pl.BlockSpec((DEC_BATCH, D_MODEL), lambda i, d: (0, 5)),
                      pl.BlockSpec((1, D_MODEL), lambda i, d: (0, 0))],
            out_specs=[pl.BlockSpec((FIN_TOK, D_MODEL), lambda i, d: (ptile(i), 0)),
                       pl.BlockSpec((DEC_BATCH, D_MODEL), lambda i, d: (0, 0))],
            scratch_shapes=[pltpu.VMEM((GATHER_RING, FIN_ROWS, D_MODEL), F32),
                            pltpu.SemaphoreType.DMA((GATHER_RING,))]),
        out_shape=[jax.ShapeDtypeStruct((T_PROMPT, D_MODEL), F32),
                   jax.ShapeDtypeStruct((DEC_BATCH, D_MODEL), F32)],
        compiler_params=_cparams(("arbitrary",)),
        name="final",
    )(dest, yb, x1_all, tg_all, mod_p3, mod_s, gf)


def _work_items(counts, blk_start):
    nblk = (counts + MOE_SB - 1) // MOE_SB
    n_items_e = (nblk + MOE_ITEM_BLOCKS - 1) // MOE_ITEM_BLOCKS
    item_end = jnp.cumsum(n_items_e)
    item_start = item_end - n_items_e
    total_items = item_end[-1]
    it = jnp.arange(MOE_NITEMS, dtype=jnp.int32)
    it_c = jnp.minimum(it, total_items - 1)
    item_e = jnp.minimum(jnp.sum((item_end[None, :] <= it_c[:, None]).astype(jnp.int32), axis=1),
                         N_EXPERTS - 1).astype(jnp.int32)
    jj = it_c - item_start[item_e]
    item_valid = (it < total_items).astype(jnp.int32)
    item_b0 = (blk_start[item_e] + jj * MOE_ITEM_BLOCKS).astype(jnp.int32)
    item_nb = jnp.where(item_valid > 0,
                        jnp.minimum(MOE_ITEM_BLOCKS, nblk[item_e] - jj * MOE_ITEM_BLOCKS), 0).astype(jnp.int32)
    nused = (blk_start[-1] + nblk[-1]).reshape(1).astype(jnp.int32)
    return item_e, item_b0, item_nb, item_valid, nused


def kernel(x_prompt, x_sample, c_prompt, c_sample, state_pool, state_gla, w_mod, b_mod, g_norm1, w_in,
           w_alpha, b_alpha, w_pool, pool_scale, g_head, w_out, g_norm2, w_router, b_router, w_gu, b_gu,
           w_down, b_down, g_final):
    x_p = x_prompt.reshape(T_PROMPT, D_MODEL)
    x_s = x_sample.reshape(DEC_BATCH, D_MODEL)

    c_all = jnp.concatenate([c_prompt, c_sample, jnp.zeros((4, D_MODEL), F32)], axis=0)
    mod = _mod(c_all, w_mod[0], b_mod[0])
    mod_p3 = mod[:BATCH].reshape(BATCH, 1, 6 * D_MODEL)
    mod_s = mod[BATCH:BATCH + DEC_BATCH]

    h_all = _prenorm(x_p, x_s, g_norm1, mod_p3, mod_s)

    wt = jnp.swapaxes(w_in, 1, 2)[0]
    w_a = jnp.pad(wt[COL_A:COL_G], ((0, LANES - ALPHA_RANK), (0, 0)))
    w_g = wt[COL_G:]
    u_all = _inproj(h_all, wt, COL_U, D_POOL, F32, 512, "inproj_u")
    qkvr = _inproj(h_all, wt, COL_Q, COL_A - COL_Q, BF16, 512, "inproj_qkvr")
    acode = _inproj(h_all, w_a, 0, LANES, F32, LANES, "inproj_a")
    gates = _inproj(h_all, w_g, 0, 2 * D_MODEL, BF16, 512, "inproj_g")

    wal = jnp.pad(w_alpha[0], ((0, LANES - ALPHA_RANK), (0, 0))).astype(BF16)
    bal = b_alpha[0].reshape(1, D_K)
    wpool = w_pool[0].astype(BF16)
    ghead = g_head[0].reshape(1, DV_HEAD)

    m_p, new_gla_p = _mix_prompt(u_all, qkvr, acode, gates, wal, bal, wpool, pool_scale, ghead)

    sp2 = jnp.transpose(state_pool[0], (1, 0, 2))
    a_s, g_s = _sample_pre(u_all, sp2, acode, wal, bal, wpool, pool_scale)
    zs = qkvr[T_PROMPT:].astype(F32)
    q_s, k_s = zs[:, 0:D_K], zs[:, D_K:2 * D_K]
    gkq = jnp.stack([g_s, k_s, q_s] + [jnp.zeros_like(g_s)] * 5, axis=1)
    v3 = zs[:, 2 * D_K:2 * D_K + D_V].reshape(DEC_BATCH, 1, D_V)
    r3 = zs[:, 2 * D_K + D_V:].reshape(DEC_BATCH, 1, D_V)
    gates3 = gates[T_PROMPT:].astype(F32).reshape(DEC_BATCH, 1, 2 * D_MODEL)
    new_gla_s, m_s3 = _sample_state(gkq, v3, r3, a_s.reshape(DEC_BATCH, 1, D_MODEL), gates3,
                                    state_gla[0], ghead)
    m_s = m_s3.reshape(DEC_BATCH, D_MODEL).astype(BF16)

    wo = w_out[0].astype(BF16)
    wr = jnp.pad(w_router[0], ((0, 0), (0, LANES - N_EXPERTS))).astype(BF16)
    br = jnp.concatenate([b_router[0], jnp.full((LANES - N_EXPERTS,), -1e30, F32)]).reshape(1, LANES)
    x1_all, h2_all, ti_all, tg_all = _outproj(m_p, m_s, x_p, x_s, wo, mod_p3, mod_s, g_norm2, wr, br)

    dest_t, counts_t, bstart_t = _route(ti_all)
    dest = dest_t[:, :TOP_K].reshape(-1)
    src_tok = jnp.zeros((MOE_ROWS,), jnp.int32).at[dest].set(
        jnp.arange(N_ASSIGN, dtype=jnp.int32) // TOP_K, unique_indices=True)
    item_e, item_b0, item_nb, item_valid, nused = _work_items(counts_t[0, :N_EXPERTS], bstart_t[0, :N_EXPERTS])

    xb = _dispatch(src_tok, h2_all)
    yb = _moe(item_e, item_b0, item_nb, item_valid, nused, xb, w_gu[0],
              b_gu[0].reshape(N_EXPERTS, 1, 2 * D_FF), w_down[0], b_down[0].reshape(N_EXPERTS, 1, D_MODEL))

    y_p, y_s = _final(dest, yb, x1_all, tg_all, mod_p3, mod_s, g_final.reshape(1, D_MODEL))

    u_p = u_all[:T_PROMPT].reshape(BATCH, SEQ, D_POOL)
    new_pool_p = u_p[:, SEQ - POOL_BUF:, :][None]
    new_pool_s = jnp.concatenate([state_pool[0][:, 1:, :], u_all[T_PROMPT:][:, None, :]], axis=1)[None]
    return (y_p.reshape(BATCH, SEQ, D_MODEL), y_s.reshape(DEC_BATCH, 1, D_MODEL),
            new_pool_p, new_gla_p[None], new_pool_s, new_gla_s[None])
```

```python
import functools

import jax
import jax.numpy as jnp
from jax import lax
from jax.experimental import pallas as pl
from jax.experimental.pallas import tpu as pltpu

F32 = jnp.float32
BF16 = jnp.bfloat16

D_MODEL = 2048
BATCH = 4
SEQ = 2048
DEC_BATCH = 128
PAST_LEN = 16384
POOL_WINDOWS = (2, 4, 8, 16)
D_POOL = D_MODEL // 2
POOL_GROUP_IN = D_POOL // 4
POOL_GROUP_OUT = D_MODEL // 4
POOL_BUF = 15
GLA_HEADS = 4
D_K = D_MODEL // 2
D_V = D_MODEL
DK_HEAD = D_K // GLA_HEADS
DV_HEAD = D_V // GLA_HEADS
ALPHA_RANK = 16
GATE_TAU = 16.0
N_EXPERTS = 32
TOP_K = 4
D_FF = D_MODEL
SWIGLU_LIMIT = 7.0
SWIGLU_ALPHA = 1.702
EPS = 1e-6

LANES = 128
T_PROMPT = BATCH * SEQ
T_ALL = T_PROMPT + DEC_BATCH
N_ASSIGN = T_ALL * TOP_K

COL_U, COL_Q, COL_K, COL_V, COL_R = 0, 1024, 2048, 3072, 5120
COL_A = 7168
COL_G = COL_A + ALPHA_RANK

GLA_CHUNK = 128
GLA_SAFE_LOG_DECAY = -60.0
GLA_SUB = 16
MIX_ROWS = 256

MOE_SB = 256
MOE_ITEM_BLOCKS = 6
MOE_MC = MOE_SB * MOE_ITEM_BLOCKS
MOE_TF = 256
MOE_NF = D_FF // MOE_TF
MOE_NBLK = -(-(N_ASSIGN + N_EXPERTS * (MOE_SB - 1)) // MOE_SB)
MOE_ROWS = MOE_NBLK * MOE_SB
MOE_NITEMS = N_EXPERTS + -(-MOE_NBLK // MOE_ITEM_BLOCKS)

VMEM_LIMIT = 56 * 1024 * 1024


def _cparams(sem, vmem=VMEM_LIMIT):
    return pltpu.CompilerParams(dimension_semantics=sem, vmem_limit_bytes=vmem)


def _rms(x, g):
    return x * lax.rsqrt(jnp.mean(x * x, axis=-1, keepdims=True) + EPS) * g


def _sigmoid(x):
    return 1.0 / (1.0 + jnp.exp(-x))


def _mod_kernel(c_ref, w_ref, b_ref, o_ref):
    c = c_ref[...]
    s = (c * _sigmoid(c)).astype(BF16)
    o_ref[...] = jnp.dot(s, w_ref[...].astype(BF16), preferred_element_type=F32) + b_ref[...]


def _mod(c_all, w_mod, b_mod):
    m = c_all.shape[0]
    tn = 1024
    return pl.pallas_call(
        _mod_kernel,
        grid=(6 * D_MODEL // tn,),
        in_specs=[pl.BlockSpec((m, D_MODEL), lambda j: (0, 0)),
                  pl.BlockSpec((D_MODEL, tn), lambda j: (0, j)),
                  pl.BlockSpec((1, tn), lambda j: (0, j))],
        out_specs=pl.BlockSpec((m, tn), lambda j: (0, j)),
        out_shape=jax.ShapeDtypeStruct((m, 6 * D_MODEL), F32),
        compiler_params=_cparams(("arbitrary",)),
        name="mod",
    )(c_all, w_mod, b_mod.reshape(1, -1))


ROW_TILE = 512
ROW_TILES_P = T_PROMPT // ROW_TILE
_TILES_PER_BATCH = SEQ // ROW_TILE


def _prompt_tile(i):
    return jnp.minimum(i, ROW_TILES_P - 1)


def _prompt_rows_spec(width):
    return pl.BlockSpec((ROW_TILE, width), lambda i: (_prompt_tile(i), 0))


def _sample_rows_spec(width):
    return pl.BlockSpec((DEC_BATCH, width), lambda i: (0, 0))


def _mod_spec_prompt(chunk):
    return pl.BlockSpec((1, 1, D_MODEL), lambda i: (_prompt_tile(i) // _TILES_PER_BATCH, 0, chunk))


def _mod_spec_sample(chunk):
    return pl.BlockSpec((DEC_BATCH, D_MODEL), lambda i: (0, chunk))


def _modval(ref):
    v = ref[...]
    return v.reshape(v.shape[-2], v.shape[-1])


def _prenorm_kernel(xp_ref, xs_ref, g_ref, shp_ref, scp_ref, shs_ref, scs_ref, o_ref):
    i = pl.program_id(0)

    def body(x_ref, sh_ref, sc_ref):
        y = _rms(x_ref[...], g_ref[...])
        return (y * (1.0 + _modval(sc_ref)) + _modval(sh_ref)).astype(BF16)

    @pl.when(i < ROW_TILES_P)
    def _():
        o_ref[...] = body(xp_ref, shp_ref, scp_ref)

    @pl.when(i == ROW_TILES_P)
    def _():
        o_ref[0:DEC_BATCH, :] = body(xs_ref, shs_ref, scs_ref)


def _prenorm(x_p, x_s, g, mod_p3, mod_s):
    return pl.pallas_call(
        _prenorm_kernel,
        grid=(ROW_TILES_P + 1,),
        in_specs=[_prompt_rows_spec(D_MODEL), _sample_rows_spec(D_MODEL),
                  pl.BlockSpec((1, D_MODEL), lambda i: (0, 0)),
                  _mod_spec_prompt(0), _mod_spec_prompt(1), _mod_spec_sample(0), _mod_spec_sample(1)],
        out_specs=pl.BlockSpec((ROW_TILE, D_MODEL), lambda i: (i, 0)),
        out_shape=jax.ShapeDtypeStruct((T_ALL, D_MODEL), BF16),
        compiler_params=_cparams(("arbitrary",)),
        name="prenorm",
    )(x_p, x_s, g, mod_p3, mod_p3, mod_s, mod_s)


def _inproj_kernel(h_ref, wt_ref, o_ref, wb_ref):
    @pl.when(pl.program_id(1) == 0)
    def _():
        wb_ref[...] = wt_ref[...].astype(BF16)

    o_ref[...] = lax.dot_general(h_ref[...], wb_ref[...], (((1,), (1,)), ((), ())),
                                 preferred_element_type=F32).astype(o_ref.dtype)


def _inproj(h_all, wt, row0, nrows, out_dtype, tn, name):
    tm = 1664
    joff = row0 // tn
    return pl.pallas_call(
        _inproj_kernel,
        grid=(nrows // tn, T_ALL // tm),
        in_specs=[pl.BlockSpec((tm, D_MODEL), lambda j, i: (i, 0)),
                  pl.BlockSpec((tn, D_MODEL), lambda j, i: (j + joff, 0))],
        out_specs=pl.BlockSpec((tm, tn), lambda j, i: (i, j)),
        out_shape=jax.ShapeDtypeStruct((T_ALL, nrows), out_dtype),
        scratch_shapes=[pltpu.VMEM((tn, D_MODEL), BF16)],
        compiler_params=_cparams(("arbitrary", "arbitrary")),
        name=name,
    )(h_all, wt)


def _log_decay(ac, wal, bal):
    x = jnp.dot(ac.astype(BF16), wal, preferred_element_type=F32) + bal
    return (jnp.minimum(x, 0.0) - jnp.log1p(jnp.exp(-jnp.abs(x)))) / GATE_TAU


def _mixp_kernel(u_ref, q_ref, k_ref, v_ref, r_ref, ac_ref, ga_ref, gb_ref, wal_ref, bal_ref,
                 wpool_ref, pscale_ref, ghead_ref, m_ref, st_ref,
                 ext_ref, state_ref, b_ref, q32_ref, k32_ref, v32_ref, o_ref):
    R = MIX_ROWS
    h = pl.program_id(1)
    t = pl.program_id(2)

    @pl.when(t == 0)
    def _():
        ext_ref[0:16, :] = jnp.zeros((16, POOL_GROUP_IN), F32)
        state_ref[...] = jnp.zeros_like(state_ref)

    u = u_ref[...]
    ext_ref[16:16 + R, :] = u
    e = ext_ref[...]
    s2 = e + pltpu.roll(e, 1, axis=0)
    s4 = s2 + pltpu.roll(s2, 2, axis=0)
    s8 = s4 + pltpu.roll(s4, 4, axis=0)
    s16 = s8 + pltpu.roll(s8, 8, axis=0)
    s = jnp.where(h == 0, s2, jnp.where(h == 1, s4, jnp.where(h == 2, s8, s16)))[16:, :]
    window = jnp.left_shift(2, h)
    pos = t * R + lax.broadcasted_iota(jnp.int32, (R, 1), 0)
    cnt = jnp.minimum(pos + 1, window).astype(F32)
    p = s / cnt - u
    a_out = jnp.dot(p.astype(BF16), wpool_ref[0], preferred_element_type=F32) * pscale_ref[...]
    ext_ref[0:16, :] = ext_ref[R:R + 16, :]

    C = GLA_CHUNK
    g = _log_decay(ac_ref[...], wal_ref[...], bal_ref[...])
    ri = lax.broadcasted_iota(jnp.int32, (C, C), 0)
    ci = lax.broadcasted_iota(jnp.int32, (C, C), 1)
    causal = ri >= ci
    tri = jnp.where(causal, 1.0, 0.0).astype(BF16)
    g_hi = g.astype(BF16)
    g_lo = (g - g_hi.astype(F32)).astype(BF16)
    for c in range(R // C):
        b_ref[c * C:(c + 1) * C, :] = (
            jnp.dot(tri, g_hi[c * C:(c + 1) * C, :], preferred_element_type=F32)
            + jnp.dot(tri, g_lo[c * C:(c + 1) * C, :], preferred_element_type=F32))
    q32_ref[...] = q_ref[...].astype(F32) * (DK_HEAD ** -0.5)
    k32_ref[...] = k_ref[...].astype(F32)
    v32_ref[...] = v_ref[...].astype(F32)

    nt_dims = (((1,), (1,)), ((), ()))
    tn_dims = (((0,), (0,)), ((), ()))
    sub_iota = lax.broadcasted_iota(jnp.int32, (GLA_SUB, 1), 0)

    for c in range(R // C):
        r0 = c * C
        bc = b_ref[r0:r0 + C, :]
        qc = q32_ref[r0:r0 + C, :]
        kc = k32_ref[r0:r0 + C, :]
        vcb = v32_ref[r0:r0 + C, :].astype(BF16)
        st = state_ref[...]
        b_last = bc[C - 1:C, :]
        e_last = jnp.exp(b_last)
        safe = jnp.min(b_last) >= GLA_SAFE_LOG_DECAY
        qt = (qc * jnp.exp(bc)).astype(BF16)
        o_inter = lax.dot_general(qt, st.astype(BF16), nt_dims, preferred_element_type=F32)

        @pl.when(safe)
        def _(r0=r0, bc=bc, kc=kc, vcb=vcb, st=st, e_last=e_last, qt=qt, o_inter=o_inter):
            kt = kc * jnp.exp(-bc)
            att = lax.dot_general(qt, kt.astype(BF16), nt_dims, preferred_element_type=F32)
            att = jnp.where(causal, att, 0.0).astype(BF16)
            o_ref[r0:r0 + C, :] = o_inter + jnp.dot(att, vcb, preferred_element_type=F32)
            kd = (kt * e_last).astype(BF16)
            state_ref[...] = st * e_last + lax.dot_general(vcb, kd, tn_dims, preferred_element_type=F32)

        @pl.when(jnp.logical_not(safe))
        def _(r0=r0, bc=bc, qc=qc, kc=kc, vcb=vcb, st=st, b_last=b_last, e_last=e_last, o_inter=o_inter):
            o_ref[r0:r0 + C, :] = o_inter
            for i in range(C // GLA_SUB):
                lo = i * GLA_SUB
                bsub = bc[lo:lo + GLA_SUB, :]
                qsub = qc[lo:lo + GLA_SUB, :]
                acc = jnp.zeros((GLA_SUB, DV_HEAD), F32)
                if i > 0:
                    ref_row = bc[lo:lo + 1, :]
                    qi = (qsub * jnp.exp(bsub - ref_row)).astype(BF16)
                    ki = (kc[0:lo, :] * jnp.exp(ref_row - bc[0:lo, :])).astype(BF16)
                    att = lax.dot_general(qi, ki, nt_dims, preferred_element_type=F32)
                    acc = jnp.dot(att.astype(BF16), vcb[0:lo, :], preferred_element_type=F32)

                def diag_body(j, acc, lo=lo, bsub=bsub, qsub=qsub):
                    row = r0 + lo + j
                    bs = b_ref[pl.ds(row, 1), :]
                    ks = k32_ref[pl.ds(row, 1), :]
                    vs = v32_ref[pl.ds(row, 1), :]
                    dec = jnp.exp(jnp.where(sub_iota >= j, bsub - bs, -jnp.inf))
                    col = jnp.sum(qsub * dec * ks, axis=-1, keepdims=True)
                    return acc + col * vs

                acc = lax.fori_loop(0, GLA_SUB, diag_body, acc)
                o_ref[r0 + lo:r0 + lo + GLA_SUB, :] += acc
            kd = (kc * jnp.exp(b_last - bc)).astype(BF16)
            state_ref[...] = st * e_last + lax.dot_general(vcb, kd, tn_dims, preferred_element_type=F32)

    o = _rms(o_ref[...], ghead_ref[...])
    r = r_ref[...].astype(F32)
    b_out = o * (r * _sigmoid(r))
    m = _sigmoid(ga_ref[...].astype(F32)) * a_out + _sigmoid(gb_ref[...].astype(F32)) * b_out
    m_ref[...] = m.astype(BF16)

    @pl.when(t == pl.num_programs(2) - 1)
    def _():
        st_ref[0, 0] = state_ref[...].T


def _mix_prompt(u_all, qkvr, acode, gates, wal, bal, wpool, pscale, ghead):
    R = MIX_ROWS
    nt = SEQ // R

    def rows(b, h, t):
        return b * nt + t

    in_specs = [
        pl.BlockSpec((R, POOL_GROUP_IN), lambda b, h, t: (rows(b, h, t), h)),
        pl.BlockSpec((R, DK_HEAD), lambda b, h, t: (rows(b, h, t), h)),
        pl.BlockSpec((R, DK_HEAD), lambda b, h, t: (rows(b, h, t), GLA_HEADS + h)),
        pl.BlockSpec((R, DV_HEAD), lambda b, h, t: (rows(b, h, t), GLA_HEADS + h)),
        pl.BlockSpec((R, DV_HEAD), lambda b, h, t: (rows(b, h, t), 2 * GLA_HEADS + h)),
        pl.BlockSpec((R, LANES), lambda b, h, t: (rows(b, h, t), 0)),
        pl.BlockSpec((R, DV_HEAD), lambda b, h, t: (rows(b, h, t), h)),
        pl.BlockSpec((R, DV_HEAD), lambda b, h, t: (rows(b, h, t), GLA_HEADS + h)),
        pl.BlockSpec((LANES, DK_HEAD), lambda b, h, t: (0, h)),
        pl.BlockSpec((1, DK_HEAD), lambda b, h, t: (0, h)),
        pl.BlockSpec((1, POOL_GROUP_IN, POOL_GROUP_OUT), lambda b, h, t: (h, 0, 0)),
        pl.BlockSpec((1, POOL_GROUP_OUT), lambda b, h, t: (0, h)),
        pl.BlockSpec((1, DV_HEAD), lambda b, h, t: (0, 0)),
    ]
    out_specs = [
        pl.BlockSpec((R, DV_HEAD), lambda b, h, t: (rows(b, h, t), h)),
        pl.BlockSpec((1, 1, DK_HEAD, DV_HEAD), lambda b, h, t: (b, h, 0, 0)),
    ]
    return pl.pallas_call(
        _mixp_kernel,
        grid=(BATCH, GLA_HEADS, nt),
        in_specs=in_specs,
        out_specs=out_specs,
        out_shape=[jax.ShapeDtypeStruct((T_PROMPT, D_MODEL), BF16),
                   jax.ShapeDtypeStruct((BATCH, GLA_HEADS, DK_HEAD, DV_HEAD), F32)],
        scratch_shapes=[pltpu.VMEM((16 + R, POOL_GROUP_IN), F32),
                        pltpu.VMEM((DV_HEAD, DK_HEAD), F32),
                        pltpu.VMEM((R, DK_HEAD), F32),
                        pltpu.VMEM((R, DK_HEAD), F32),
                        pltpu.VMEM((R, DK_HEAD), F32),
                        pltpu.VMEM((R, DV_HEAD), F32),
                        pltpu.VMEM((R, DV_HEAD), F32)],
        compiler_params=_cparams(("arbitrary", "arbitrary", "arbitrary")),
        name="mix_prompt",
    )(u_all, qkvr, qkvr, qkvr, qkvr, acode, gates, gates, wal, bal, wpool, pscale, ghead)


def _spre_kernel(u_ref, sp_ref, ac_ref, wal_ref, bal_ref, wpool_ref, pscale_ref, a_ref, g_ref):
    u = u_ref[...]
    for gi, w in enumerate(POOL_WINDOWS):
        lo = gi * POOL_GROUP_IN
        s = u[:, lo:lo + POOL_GROUP_IN]
        for j in range(POOL_BUF - (w - 1), POOL_BUF):
            s = s + sp_ref[j, :, lo:lo + POOL_GROUP_IN]
        cnt = float(min(PAST_LEN + 1, w))
        p = s / cnt - u[:, lo:lo + POOL_GROUP_IN]
        a = jnp.dot(p.astype(BF16), wpool_ref[gi], preferred_element_type=F32)
        olo = gi * POOL_GROUP_OUT
        a_ref[:, olo:olo + POOL_GROUP_OUT] = a * pscale_ref[:, olo:olo + POOL_GROUP_OUT]
    g_ref[...] = _log_decay(ac_ref[...], wal_ref[...], bal_ref[...])


def _sample_pre(u_all, sp2, acode, wal, bal, wpool, pscale):
    blk = T_PROMPT // DEC_BATCH
    return pl.pallas_call(
        _spre_kernel,
        grid=(1,),
        in_specs=[pl.BlockSpec((DEC_BATCH, D_POOL), lambda i: (blk, 0)),
                  pl.BlockSpec((POOL_BUF, DEC_BATCH, D_POOL), lambda i: (0, 0, 0)),
                  pl.BlockSpec((DEC_BATCH, LANES), lambda i: (blk, 0)),
                  pl.BlockSpec((LANES, D_K), lambda i: (0, 0)),
                  pl.BlockSpec((1, D_K), lambda i: (0, 0)),
                  pl.BlockSpec((4, POOL_GROUP_IN, POOL_GROUP_OUT), lambda i: (0, 0, 0)),
                  pl.BlockSpec((1, D_MODEL), lambda i: (0, 0))],
        out_specs=[pl.BlockSpec((DEC_BATCH, D_MODEL), lambda i: (0, 0)),
                   pl.BlockSpec((DEC_BATCH, D_K), lambda i: (0, 0))],
        out_shape=[jax.ShapeDtypeStruct((DEC_BATCH, D_MODEL), F32),
                   jax.ShapeDtypeStruct((DEC_BATCH, D_K), F32)],
        compiler_params=_cparams(("arbitrary",)),
        name="sample_pre",
    )(u_all, sp2, acode, wal, bal, wpool, pscale)


def _sstate_kernel(gkq_ref, v_ref, r_ref, a_ref, gt_ref, s_ref, ghead_ref, so_ref, m_ref):
    x = gkq_ref[0]
    rowi = lax.broadcasted_iota(jnp.int32, x.shape, 0)
    x = jnp.where(rowi == 0, jnp.exp(x), jnp.where(rowi == 2, x * (DK_HEAD ** -0.5), x))
    xt = x.T
    v = v_ref[0]
    r = r_ref[0]
    a_out = a_ref[0]
    gates = gt_ref[0]
    for h in range(GLA_HEADS):
        cols = xt[h * DK_HEAD:(h + 1) * DK_HEAD, :]
        dec, kcol, qcol = cols[:, 0:1], cols[:, 1:2], cols[:, 2:3]
        lo = h * DV_HEAD
        vrow = v[:, lo:lo + DV_HEAD]
        s_new = dec * s_ref[0, h] + kcol * vrow
        so_ref[0, h] = s_new
        o = jnp.sum(qcol * s_new, axis=0, keepdims=True)
        o = _rms(o, ghead_ref[...])
        rr = r[:, lo:lo + DV_HEAD]
        b_out = o * (rr * _sigmoid(rr))
        m_ref[0, :, lo:lo + DV_HEAD] = (_sigmoid(gates[:, lo:lo + DV_HEAD]) * a_out[:, lo:lo + DV_HEAD]
                                        + _sigmoid(gates[:, D_MODEL + lo:D_MODEL + lo + DV_HEAD]) * b_out)


def _sample_state(gkq, v3, r3, a3, gates3, state, ghead):
    def row3(width):
        return pl.BlockSpec((1, 1, width), lambda i: (i, 0, 0))

    sspec = pl.BlockSpec((1, GLA_HEADS, DK_HEAD, DV_HEAD), lambda i: (i, 0, 0, 0))
    return pl.pallas_call(
        _sstate_kernel,
        grid=(DEC_BATCH,),
        in_specs=[pl.BlockSpec((1, 8, D_K), lambda i: (i, 0, 0)),
                  row3(D_V), row3(D_V), row3(D_MODEL), row3(2 * D_MODEL), sspec,
                  pl.BlockSpec((1, DV_HEAD), lambda i: (0, 0))],
        out_specs=[sspec, row3(D_MODEL)],
        out_shape=[jax.ShapeDtypeStruct((DEC_BATCH, GLA_HEADS, DK_HEAD, DV_HEAD), F32),
                   jax.ShapeDtypeStruct((DEC_BATCH, 1, D_MODEL), F32)],
        compiler_params=_cparams(("arbitrary",)),
        name="sample_state",
    )(gkq, v3, r3, a3, gates3, state, ghead)


def _outproj_kernel(mp_ref, ms_ref, xp_ref, xs_ref, wo_ref, gt1p_ref, sh2p_ref, sc2p_ref,
                    gt1s_ref, sh2s_ref, sc2s_ref, g2_ref, wr_ref, br_ref,
                    x1_ref, h2_ref, ti_ref, tg_ref):
    i = pl.program_id(0)

    def body(nrows, m_ref, x_ref, gt1_ref, sh2_ref, sc2_ref):
        y = jnp.dot(m_ref[...], wo_ref[...], preferred_element_type=F32)
        x1 = x_ref[...] + (1.0 + _modval(gt1_ref)) * y
        x1_ref[0:nrows, :] = x1
        h2 = _rms(x1, g2_ref[...]) * (1.0 + _modval(sc2_ref)) + _modval(sh2_ref)
        h2_ref[0:nrows, :] = h2
        logits = jnp.dot(h2.astype(BF16), wr_ref[...], preferred_element_type=F32) + br_ref[...]
        lane = lax.broadcasted_iota(jnp.int32, logits.shape, 1)
        vals, idxs = [], []
        cur = logits
        for _ in range(TOP_K):
            mx = jnp.max(cur, axis=-1, keepdims=True)
            ix = jnp.min(jnp.where(cur == mx, lane, LANES), axis=-1, keepdims=True)
            vals.append(mx)
            idxs.append(ix)
            cur = jnp.where(lane == ix, -jnp.inf, cur)
        exps = [jnp.exp(v - vals[0]) for v in vals]
        den = exps[0] + exps[1] + exps[2] + exps[3]
        ti = jnp.full(logits.shape, -1, jnp.int32)
        tg = jnp.zeros(logits.shape, F32)
        for k in range(TOP_K):
            ti = jnp.where(lane == k, idxs[k], ti)
            tg = jnp.where(lane == k, exps[k] / den, tg)
        ti_ref[0:nrows, :] = ti
        tg_ref[0:nrows, :] = tg

    @pl.when(i < ROW_TILES_P)
    def _():
        body(ROW_TILE, mp_ref, xp_ref, gt1p_ref, sh2p_ref, sc2p_ref)

    @pl.when(i == ROW_TILES_P)
    def _():
        body(DEC_BATCH, ms_ref, xs_ref, gt1s_ref, sh2s_ref, sc2s_ref)


def _outproj(m_p, m_s, x_p, x_s, wo, mod_p3, mod_s, g2, wr, br):
    return pl.pallas_call(
        _outproj_kernel,
        grid=(ROW_TILES_P + 1,),
        in_specs=[_prompt_rows_spec(D_MODEL), _sample_rows_spec(D_MODEL),
                  _prompt_rows_spec(D_MODEL), _sample_rows_spec(D_MODEL),
                  pl.BlockSpec((D_MODEL, D_MODEL), lambda i: (0, 0)),
                  _mod_spec_prompt(2), _mod_spec_prompt(3), _mod_spec_prompt(4),
                  _mod_spec_sample(2), _mod_spec_sample(3), _mod_spec_sample(4),
                  pl.BlockSpec((1, D_MODEL), lambda i: (0, 0)),
                  pl.BlockSpec((D_MODEL, LANES), lambda i: (0, 0)),
                  pl.BlockSpec((1, LANES), lambda i: (0, 0))],
        out_specs=[pl.BlockSpec((ROW_TILE, D_MODEL), lambda i: (i, 0)),
                   pl.BlockSpec((ROW_TILE, D_MODEL), lambda i: (i, 0)),
                   pl.BlockSpec((ROW_TILE, LANES), lambda i: (i, 0)),
                   pl.BlockSpec((ROW_TILE, LANES), lambda i: (i, 0))],
        out_shape=[jax.ShapeDtypeStruct((T_ALL, D_MODEL), F32),
                   jax.ShapeDtypeStruct((T_ALL, D_MODEL), F32),
                   jax.ShapeDtypeStruct((T_ALL, LANES), jnp.int32),
                   jax.ShapeDtypeStruct((T_ALL, LANES), F32)],
        compiler_params=_cparams(("arbitrary",)),
        name="outproj",
    )(m_p, m_s, x_p, x_s, wo, mod_p3, mod_p3, mod_p3, mod_s, mod_s, mod_s, g2, wr, br)


ROUTE_TILE = 640
ROUTE_TILES = T_ALL // ROUTE_TILE


def _route_kernel(ti_ref, dest_ref, cnt_ref, bstart_ref, run_ref, base_ref):
    phase = pl.program_id(0)
    t = pl.program_id(1)
    ti = ti_ref[...]
    lane = lax.broadcasted_iota(jnp.int32, (ROUTE_TILE, LANES), 1)
    onehots = [lane == ti[:, k:k + 1] for k in range(TOP_K)]
    per_tok = jnp.zeros((ROUTE_TILE, LANES), F32)
    for oh in onehots:
        per_tok = per_tok + jnp.where(oh, 1.0, 0.0)

    @pl.when((phase == 0) & (t == 0))
    def _():
        run_ref[...] = jnp.zeros_like(run_ref)

    @pl.when(phase == 0)
    def _():
        run_ref[...] += jnp.sum(per_tok, axis=0, keepdims=True)

    @pl.when((phase == 1) & (t == 0))
    def _():
        counts = run_ref[...]
        nblk = jnp.floor((counts + (MOE_SB - 1)) * (1.0 / MOE_SB))
        ri = lax.broadcasted_iota(jnp.int32, (LANES, LANES), 0)
        ci = lax.broadcasted_iota(jnp.int32, (LANES, LANES), 1)
        before = jnp.where(ri < ci, 1.0, 0.0).astype(BF16)
        nb8 = jnp.broadcast_to(nblk, (8, LANES)).astype(BF16)
        bstart = jnp.dot(nb8, before, preferred_element_type=F32)[0:1, :]
        cnt_ref[...] = counts.astype(jnp.int32)
        bstart_ref[...] = bstart.astype(jnp.int32)
        base_ref[...] = bstart * MOE_SB
        run_ref[...] = jnp.zeros_like(run_ref)

    @pl.when(phase == 1)
    def _():
        ri = lax.broadcasted_iota(jnp.int32, (ROUTE_TILE, ROUTE_TILE), 0)
        ci = lax.broadcasted_iota(jnp.int32, (ROUTE_TILE, ROUTE_TILE), 1)
        earlier = jnp.where(ri > ci, 1.0, 0.0).astype(BF16)
        prior = jnp.dot(earlier, per_tok.astype(BF16), preferred_element_type=F32)
        pos = prior + (run_ref[...] + base_ref[...])
        dest = jnp.full((ROUTE_TILE, LANES), -1, jnp.int32)
        for k, oh in enumerate(onehots):
            d = jnp.sum(jnp.where(oh, pos, 0.0), axis=-1, keepdims=True).astype(jnp.int32)
            dest = jnp.where(lane == k, d, dest)
        dest_ref[...] = dest
        run_ref[...] += jnp.sum(per_tok, axis=0, keepdims=True)


def _route(ti_all):
    return pl.pallas_call(
        _route_kernel,
        grid=(2, ROUTE_TILES),
        in_specs=[pl.BlockSpec((ROUTE_TILE, LANES), lambda p, t: (t, 0))],
        out_specs=[pl.BlockSpec((ROUTE_TILE, LANES), lambda p, t: (t * p, 0)),
                   pl.BlockSpec((1, LANES), lambda p, t: (0, 0)),
                   pl.BlockSpec((1, LANES), lambda p, t: (0, 0))],
        out_shape=[jax.ShapeDtypeStruct((T_ALL, LANES), jnp.int32),
                   jax.ShapeDtypeStruct((1, LANES), jnp.int32),
                   jax.ShapeDtypeStruct((1, LANES), jnp.int32)],
        scratch_shapes=[pltpu.VMEM((1, LANES), F32), pltpu.VMEM((1, LANES), F32)],
        compiler_params=_cparams(("arbitrary", "arbitrary")),
        name="route",
    )(ti_all)


GATHER_UNROLL = 8
GATHER_RING = 3


def _start_row_gather(n_rows, src_row, hbm_ref, dst_ref, sem, dst_row=lambda g, u: g * GATHER_UNROLL + u):
    def body(g, c):
        for u in range(GATHER_UNROLL):
            pltpu.make_async_copy(hbm_ref.at[pl.ds(src_row(g * GATHER_UNROLL + u), 1)],
                                  dst_ref.at[pl.ds(dst_row(g, u), 1)], sem).start(priority=u % 2)
        return c
    lax.fori_loop(0, n_rows // GATHER_UNROLL, body, 0)


def _wait_row_gather(n_rows, hbm_ref, dst_ref, sem):
    pltpu.make_async_copy(hbm_ref.at[pl.ds(0, n_rows)], dst_ref, sem).wait()


DISP_TOK = 640
DISP_ROWS = DISP_TOK * TOP_K


def _dispatch_kernel(dest_ref, cnt_ref, bstart_ref, nused_ref, h2_ref, xb_hbm, zero_ref, sem, zsem):
    i = pl.program_id(0)

    def body(g, c):
        for u in range(GATHER_UNROLL):
            row = dest_ref[i * DISP_ROWS + g * GATHER_UNROLL + u]
            tok = g * (GATHER_UNROLL // TOP_K) + u // TOP_K
            pltpu.make_async_copy(h2_ref.at[pl.ds(tok, 1)], xb_hbm.at[pl.ds(row, 1)], sem).start(priority=u % 2)
        return c
    lax.fori_loop(0, DISP_ROWS // GATHER_UNROLL, body, 0)

    def pad_range(e):
        nblk = (cnt_ref[e] + MOE_SB - 1) // MOE_SB
        return bstart_ref[e] * MOE_SB + cnt_ref[e], (bstart_ref[e] + nblk) * MOE_SB

    def zero_row(row):
        return pltpu.make_async_copy(zero_ref.at[pl.ds(0, 1)], xb_hbm.at[pl.ds(row, 1)], zsem)

    def zero_block(blk):
        return pltpu.make_async_copy(zero_ref, xb_hbm.at[pl.ds(blk * MOE_SB, MOE_SB)], zsem)

    def for_each_unowned(row_fn, block_fn):
        def per_expert(e, c):
            lo, hi = pad_range(e)
            lax.fori_loop(lo, hi, lambda r, cc: (row_fn(r), cc)[1], 0)
            return c
        lax.fori_loop(0, N_EXPERTS, per_expert, 0)
        lax.fori_loop(nused_ref[0], MOE_NBLK, lambda b, cc: (block_fn(b), cc)[1], 0)

    @pl.when(i == 0)
    def _():
        zero_ref[...] = jnp.zeros_like(zero_ref)
        for_each_unowned(lambda r: zero_row(r).start(), lambda b: zero_block(b).start())

    for _ in range(TOP_K):
        pltpu.make_async_copy(h2_ref, xb_hbm.at[pl.ds(0, DISP_TOK)], sem).wait()

    @pl.when(i == 0)
    def _():
        for_each_unowned(lambda r: zero_row(r).wait(), lambda b: zero_block(b).wait())


def _dispatch(dest, counts, blk_start, nused, h2_all):
    return pl.pallas_call(
        _dispatch_kernel,
        grid_spec=pltpu.PrefetchScalarGridSpec(
            num_scalar_prefetch=4,
            grid=(T_ALL // DISP_TOK,),
            in_specs=[pl.BlockSpec((DISP_TOK, D_MODEL), lambda i, d, c, b, n: (i, 0))],
            out_specs=pl.BlockSpec(memory_space=pl.ANY),
            scratch_shapes=[pltpu.VMEM((MOE_SB, D_MODEL), F32),
                            pltpu.SemaphoreType.DMA(()),
                            pltpu.SemaphoreType.DMA(())]),
        out_shape=jax.ShapeDtypeStruct((MOE_ROWS, D_MODEL), F32),
        compiler_params=_cparams(("arbitrary",)),
        name="dispatch",
    )(dest, counts, blk_start, nused, h2_all)


def _moe_kernel(ie_ref, ib0_ref, inb_ref, ival_ref, nused_ref,
                xb_hbm, wg_ref, wu_ref, wd_ref, bg_ref, bu_ref, bd_ref, yb_hbm,
                xbuf, xstage, acc, wgb, wub, wdb, xsem, ysem):
    i = pl.program_id(0)
    f = pl.program_id(1)
    n_items = pl.num_programs(0)
    nb = inb_ref[i]

    def x_copy(blk, s):
        row0 = (ib0_ref[i] + blk) * MOE_SB
        return pltpu.make_async_copy(xb_hbm.at[pl.ds(row0, MOE_SB)], xstage.at[s], xsem.at[s])

    def y_copy(gblk, r0):
        return pltpu.make_async_copy(acc.at[pl.ds(r0, MOE_SB)], yb_hbm.at[pl.ds(gblk * MOE_SB, MOE_SB)], ysem)

    def wait_y_copies(count):
        def body(b, c):
            y_copy(0, 0).wait()
            return c
        lax.fori_loop(0, count, body, 0)

    @pl.when((f == 0) & (nb > 0))
    def _():
        x_copy(0, 0).start()

    @pl.when((f == 0) & (i > 0))
    def _():
        wait_y_copies(inb_ref[jnp.maximum(i - 1, 0)])

    @pl.when(nb > 0)
    def _():
        wgb[...] = wg_ref[0].astype(BF16)
        wub[...] = wu_ref[0].astype(BF16)
        wdb[...] = wd_ref[0].astype(BF16)

    bg = bg_ref[0]
    bu = bu_ref[0]
    bd = bd_ref[0]

    def run_blocks(first, final):
        def body(blk, c):
            r0 = pl.multiple_of(blk * MOE_SB, MOE_SB)
            if first:
                s = blk % 2
                x_copy(blk, s).wait()
                x_copy(jnp.minimum(blk + 1, nb - 1), 1 - s).start()
                x = xstage[s].astype(BF16)
                xbuf[pl.ds(r0, MOE_SB), :] = x
            else:
                x = xbuf[pl.ds(r0, MOE_SB), :]
            g = jnp.dot(x, wgb[...], preferred_element_type=F32) + bg
            u = jnp.dot(x, wub[...], preferred_element_type=F32) + bu
            xg = jnp.minimum(g, SWIGLU_LIMIT)
            xl = jnp.clip(u, -SWIGLU_LIMIT, SWIGLU_LIMIT)
            act = xg * _sigmoid(SWIGLU_ALPHA * xg) * (xl + 1.0)
            part = jnp.dot(act.astype(BF16), wdb[...], preferred_element_type=F32)
            if first:
                acc[pl.ds(r0, MOE_SB), :] = part + bd
            else:
                acc[pl.ds(r0, MOE_SB), :] += part
            if final:
                y_copy(ib0_ref[i] + blk, r0).start()
            return c
        lax.fori_loop(0, nb, body, 0)

    @pl.when(f == 0)
    def _():
        run_blocks(True, False)

        @pl.when(nb > 0)
        def _():
            x_copy(nb - 1, nb % 2).wait()

    @pl.when((f > 0) & (f < MOE_NF - 1))
    def _():
        run_blocks(False, False)

    @pl.when(f == MOE_NF - 1)
    def _():
        run_blocks(False, True)

    @pl.when((f == MOE_NF - 1) & (i == n_items - 1))
    def _():
        wait_y_copies(nb)
        acc[0:MOE_SB, :] = jnp.zeros((MOE_SB, D_MODEL), F32)

        def zstart(gblk, c):
            y_copy(gblk, 0).start()
            return c
        lax.fori_loop(nused_ref[0], MOE_NBLK, zstart, 0)
        wait_y_copies(MOE_NBLK - nused_ref[0])


def _moe(item_e, item_b0, item_nb, item_valid, nused, xb, w_gu, b_gu3, w_down, b_down3):
    def f_eff(f, ival, i):
        return jnp.where(ival[i] > 0, f, MOE_NF - 1)

    in_specs = [
        pl.BlockSpec(memory_space=pl.ANY),
        pl.BlockSpec((1, D_MODEL, MOE_TF), lambda i, f, ie, ib0, inb, iv, nu: (ie[i], 0, f_eff(f, iv, i))),
        pl.BlockSpec((1, D_MODEL, MOE_TF),
                     lambda i, f, ie, ib0, inb, iv, nu: (ie[i], 0, MOE_NF + f_eff(f, iv, i))),
        pl.BlockSpec((1, MOE_TF, D_MODEL), lambda i, f, ie, ib0, inb, iv, nu: (ie[i], f_eff(f, iv, i), 0)),
        pl.BlockSpec((1, 1, MOE_TF), lambda i, f, ie, ib0, inb, iv, nu: (ie[i], 0, f_eff(f, iv, i))),
        pl.BlockSpec((1, 1, MOE_TF),
                     lambda i, f, ie, ib0, inb, iv, nu: (ie[i], 0, MOE_NF + f_eff(f, iv, i))),
        pl.BlockSpec((1, 1, D_MODEL), lambda i, f, ie, ib0, inb, iv, nu: (ie[i], 0, 0)),
    ]
    return pl.pallas_call(
        _moe_kernel,
        grid_spec=pltpu.PrefetchScalarGridSpec(
            num_scalar_prefetch=5,
            grid=(MOE_NITEMS, MOE_NF),
            in_specs=in_specs,
            out_specs=pl.BlockSpec(memory_space=pl.ANY),
            scratch_shapes=[pltpu.VMEM((MOE_MC, D_MODEL), BF16),
                            pltpu.VMEM((2, MOE_SB, D_MODEL), F32),
                            pltpu.VMEM((MOE_MC, D_MODEL), F32),
                            pltpu.VMEM((D_MODEL, MOE_TF), BF16),
                            pltpu.VMEM((D_MODEL, MOE_TF), BF16),
                            pltpu.VMEM((MOE_TF, D_MODEL), BF16),
                            pltpu.SemaphoreType.DMA((2,)),
                            pltpu.SemaphoreType.DMA(())]),
        out_shape=jax.ShapeDtypeStruct((MOE_ROWS, D_MODEL), F32),
        compiler_params=_cparams(("arbitrary", "arbitrary")),
        name="moe",
    )(item_e, item_b0, item_nb, item_valid, nused, xb, w_gu, w_gu, w_down, b_gu3, b_gu3, b_down3)


FIN_TOK = 128
FIN_ROWS = FIN_TOK * TOP_K


FIN_TILES_P = T_PROMPT // FIN_TOK


def _final_kernel(dest_ref, yb_hbm, x1_ref, tg_ref, gt2p_ref, gt2s_ref, gf_ref, yp_ref, ys_ref, stage_ref, sem):
    i = pl.program_id(0)
    n = pl.num_programs(0)

    def start_tile(tile):
        slot = lax.rem(tile, GATHER_RING)
        _start_row_gather(FIN_ROWS, lambda j: dest_ref[tile * FIN_ROWS + j], yb_hbm, stage_ref.at[slot],
                          sem.at[slot],
                          dst_row=lambda g, u: (u % TOP_K) * FIN_TOK + g * (GATHER_UNROLL // TOP_K) + u // TOP_K)

    @pl.when(i == 0)
    def _():
        for b in range(GATHER_RING - 1):
            start_tile(b)

    @pl.when(i + GATHER_RING - 1 < n)
    def _():
        start_tile(i + GATHER_RING - 1)

    slot = lax.rem(i, GATHER_RING)
    _wait_row_gather(FIN_ROWS, yb_hbm, stage_ref.at[slot], sem.at[slot])
    tg = tg_ref[...]
    f = jnp.zeros((FIN_TOK, D_MODEL), F32)
    for k in range(TOP_K):
        f = f + tg[:, k:k + 1] * stage_ref[slot, k * FIN_TOK:(k + 1) * FIN_TOK, :]

    def out(gt2_ref):
        return _rms(x1_ref[...] + (1.0 + _modval(gt2_ref)) * f, gf_ref[...])

    @pl.when(i < FIN_TILES_P)
    def _():
        yp_ref[...] = out(gt2p_ref)

    @pl.when(i == FIN_TILES_P)
    def _():
        ys_ref[...] = out(gt2s_ref)


def _final(dest, yb, x1_all, tg_all, mod_p3, mod_s, gf):
    def ptile(i):
        return jnp.minimum(i, FIN_TILES_P - 1)

    return pl.pallas_call(
        _final_kernel,
        grid_spec=pltpu.PrefetchScalarGridSpec(
            num_scalar_prefetch=1,
            grid=(FIN_TILES_P + 1,),
            in_specs=[pl.BlockSpec(memory_space=pl.ANY),
                      pl.BlockSpec((FIN_TOK, D_MODEL), lambda i, d: (i, 0)),
                      pl.BlockSpec((FIN_TOK, LANES), lambda i, d: (i, 0)),
                      pl.BlockSpec((1, 1, D_MODEL), lambda i, d: (ptile(i) // (SEQ // FIN_TOK), 0, 5)),
                      pl.BlockSpec((DEC_BATCH, D_MODEL), lambda i, d: (0, 5)),
                      pl.BlockSpec((1, D_MODEL), lambda i, d: (0, 0))],
            out_specs=[pl.BlockSpec((FIN_TOK, D_MODEL), lambda i, d: (ptile(i), 0)),
                       pl.BlockSpec((DEC_BATCH, D_MODEL), lambda i, d: (0, 0))],
            scratch_shapes=[pltpu.VMEM((GATHER_RING, FIN_ROWS, D_MODEL), F32),
                            pltpu.SemaphoreType.DMA((GATHER_RING,))]),
        out_shape=[jax.ShapeDtypeStruct((T_PROMPT, D_MODEL), F32),
                   jax.ShapeDtypeStruct((DEC_BATCH, D_MODEL), F32)],
        compiler_params=_cparams(("arbitrary",)),
        name="final",
    )(dest, yb, x1_all, tg_all, mod_p3, mod_s, gf)


def _work_items(counts, blk_start):
    nblk = (counts + MOE_SB - 1) // MOE_SB
    n_items_e = (nblk + MOE_ITEM_BLOCKS - 1) // MOE_ITEM_BLOCKS
    item_end = jnp.cumsum(n_items_e)
    item_start = item_end - n_items_e
    total_items = item_end[-1]
    it = jnp.arange(MOE_NITEMS, dtype=jnp.int32)
    it_c = jnp.minimum(it, total_items - 1)
    item_e = jnp.minimum(jnp.sum((item_end[None, :] <= it_c[:, None]).astype(jnp.int32), axis=1),
                         N_EXPERTS - 1).astype(jnp.int32)
    jj = it_c - item_start[item_e]
    item_valid = (it < total_items).astype(jnp.int32)
    item_b0 = (blk_start[item_e] + jj * MOE_ITEM_BLOCKS).astype(jnp.int32)
    item_nb = jnp.where(item_valid > 0,
                        jnp.minimum(MOE_ITEM_BLOCKS, nblk[item_e] - jj * MOE_ITEM_BLOCKS), 0).astype(jnp.int32)
    nused = (blk_start[-1] + nblk[-1]).reshape(1).astype(jnp.int32)
    return item_e, item_b0, item_nb, item_valid, nused


def kernel(x_prompt, x_sample, c_prompt, c_sample, state_pool, state_gla, w_mod, b_mod, g_norm1, w_in,
           w_alpha, b_alpha, w_pool, pool_scale, g_head, w_out, g_norm2, w_router, b_router, w_gu, b_gu,
           w_down, b_down, g_final):
    x_p = x_prompt.reshape(T_PROMPT, D_MODEL)
    x_s = x_sample.reshape(DEC_BATCH, D_MODEL)

    c_all = jnp.concatenate([c_prompt, c_sample, jnp.zeros((4, D_MODEL), F32)], axis=0)
    mod = _mod(c_all, w_mod[0], b_mod[0])
    mod_p3 = mod[:BATCH].reshape(BATCH, 1, 6 * D_MODEL)
    mod_s = mod[BATCH:BATCH + DEC_BATCH]

    h_all = _prenorm(x_p, x_s, g_norm1, mod_p3, mod_s)

    wt = jnp.swapaxes(w_in, 1, 2)[0]
    w_a = jnp.pad(wt[COL_A:COL_G], ((0, LANES - ALPHA_RANK), (0, 0)))
    w_g = wt[COL_G:]
    u_all = _inproj(h_all, wt, COL_U, D_POOL, F32, 512, "inproj_u")
    qkvr = _inproj(h_all, wt, COL_Q, COL_A - COL_Q, BF16, 512, "inproj_qkvr")
    acode = _inproj(h_all, w_a, 0, LANES, F32, LANES, "inproj_a")
    gates = _inproj(h_all, w_g, 0, 2 * D_MODEL, BF16, 512, "inproj_g")

    wal = jnp.pad(w_alpha[0], ((0, LANES - ALPHA_RANK), (0, 0))).astype(BF16)
    bal = b_alpha[0].reshape(1, D_K)
    wpool = w_pool[0].astype(BF16)
    ghead = g_head[0].reshape(1, DV_HEAD)

    m_p, new_gla_p = _mix_prompt(u_all, qkvr, acode, gates, wal, bal, wpool, pool_scale, ghead)

    sp2 = jnp.transpose(state_pool[0], (1, 0, 2))
    a_s, g_s = _sample_pre(u_all, sp2, acode, wal, bal, wpool, pool_scale)
    zs = qkvr[T_PROMPT:].astype(F32)
    q_s, k_s = zs[:, 0:D_K], zs[:, D_K:2 * D_K]
    gkq = jnp.stack([g_s, k_s, q_s] + [jnp.zeros_like(g_s)] * 5, axis=1)
    v3 = zs[:, 2 * D_K:2 * D_K + D_V].reshape(DEC_BATCH, 1, D_V)
    r3 = zs[:, 2 * D_K + D_V:].reshape(DEC_BATCH, 1, D_V)
    gates3 = gates[T_PROMPT:].astype(F32).reshape(DEC_BATCH, 1, 2 * D_MODEL)
    new_gla_s, m_s3 = _sample_state(gkq, v3, r3, a_s.reshape(DEC_BATCH, 1, D_MODEL), gates3,
                                    state_gla[0], ghead)
    m_s = m_s3.reshape(DEC_BATCH, D_MODEL).astype(BF16)

    wo = w_out[0].astype(BF16)
    wr = jnp.pad(w_router[0], ((0, 0), (0, LANES - N_EXPERTS))).astype(BF16)
    br = jnp.concatenate([b_router[0], jnp.full((LANES - N_EXPERTS,), -1e30, F32)]).reshape(1, LANES)
    x1_all, h2_all, ti_all, tg_all = _outproj(m_p, m_s, x_p, x_s, wo, mod_p3, mod_s, g_norm2, wr, br)

    dest_t, counts_t, bstart_t = _route(ti_all)
    dest = dest_t[:, :TOP_K].reshape(-1)
    counts, blk_start = counts_t[0, :N_EXPERTS], bstart_t[0, :N_EXPERTS]
    item_e, item_b0, item_nb, item_valid, nused = _work_items(counts, blk_start)

    xb = _dispatch(dest, counts, blk_start, nused, h2_all)
    yb = _moe(item_e, item_b0, item_nb, item_valid, nused, xb, w_gu[0],
              b_gu[0].reshape(N_EXPERTS, 1, 2 * D_FF), w_down[0], b_down[0].reshape(N_EXPERTS, 1, D_MODEL))

    y_p, y_s = _final(dest, yb, x1_all, tg_all, mod_p3, mod_s, g_final.reshape(1, D_MODEL))

    u_p = u_all[:T_PROMPT].reshape(BATCH, SEQ, D_POOL)
    new_pool_p = u_p[:, SEQ - POOL_BUF:, :][None]
    new_pool_s = jnp.concatenate([state_pool[0][:, 1:, :], u_all[T_PROMPT:][:, None, :]], axis=1)[None]
    return (y_p.reshape(BATCH, SEQ, D_MODEL), y_s.reshape(DEC_BATCH, 1, D_MODEL),
            new_pool_p, new_gla_p[None], new_pool_s, new_gla_s[None])
```

```python
import functools

import jax
import jax.numpy as jnp
from jax import lax
from jax.experimental import pallas as pl
from jax.experimental.pallas import tpu as pltpu

F32 = jnp.float32
BF16 = jnp.bfloat16

D_MODEL = 2048
BATCH = 4
SEQ = 2048
DEC_BATCH = 128
PAST_LEN = 16384
POOL_WINDOWS = (2, 4, 8, 16)
D_POOL = D_MODEL // 2
POOL_GROUP_IN = D_POOL // 4
POOL_GROUP_OUT = D_MODEL // 4
POOL_BUF = 15
GLA_HEADS = 4
D_K = D_MODEL // 2
D_V = D_MODEL
DK_HEAD = D_K // GLA_HEADS
DV_HEAD = D_V // GLA_HEADS
ALPHA_RANK = 16
GATE_TAU = 16.0
N_EXPERTS = 32
TOP_K = 4
D_FF = D_MODEL
SWIGLU_LIMIT = 7.0
SWIGLU_ALPHA = 1.702
EPS = 1e-6

LANES = 128
T_PROMPT = BATCH * SEQ
T_ALL = T_PROMPT + DEC_BATCH
N_ASSIGN = T_ALL * TOP_K

COL_U, COL_Q, COL_K, COL_V, COL_R = 0, 1024, 2048, 3072, 5120
COL_A = 7168
COL_G = COL_A + ALPHA_RANK

GLA_CHUNK = 128
GLA_SAFE_LOG_DECAY = -60.0
GLA_SUB = 16
MIX_ROWS = 256

MOE_SB = 256
MOE_ITEM_BLOCKS = 5
MOE_MC = MOE_SB * MOE_ITEM_BLOCKS
MOE_TF = 512
MOE_NF = D_FF // MOE_TF
MOE_NBLK = -(-(N_ASSIGN + N_EXPERTS * (MOE_SB - 1)) // MOE_SB)
MOE_ROWS = MOE_NBLK * MOE_SB
MOE_NITEMS = (MOE_NBLK + N_EXPERTS * (MOE_ITEM_BLOCKS - 1)) // MOE_ITEM_BLOCKS

VMEM_LIMIT = 56 * 1024 * 1024


def _cparams(sem, vmem=VMEM_LIMIT):
    return pltpu.CompilerParams(dimension_semantics=sem, vmem_limit_bytes=vmem)


def _rms(x, g):
    return x * lax.rsqrt(jnp.mean(x * x, axis=-1, keepdims=True) + EPS) * g


def _sigmoid(x):
    return 1.0 / (1.0 + jnp.exp(-x))


def _mod_kernel(c_ref, w_ref, b_ref, o_ref):
    c = c_ref[...]
    s = (c * _sigmoid(c)).astype(BF16)
    o_ref[...] = jnp.dot(s, w_ref[...].astype(BF16), preferred_element_type=F32) + b_ref[...]


def _mod(c_all, w_mod, b_mod):
    m = c_all.shape[0]
    tn = 1024
    return pl.pallas_call(
        _mod_kernel,
        grid=(6 * D_MODEL // tn,),
        in_specs=[pl.BlockSpec((m, D_MODEL), lambda j: (0, 0)),
                  pl.BlockSpec((D_MODEL, tn), lambda j: (0, j)),
                  pl.BlockSpec((1, tn), lambda j: (0, j))],
        out_specs=pl.BlockSpec((m, tn), lambda j: (0, j)),
        out_shape=jax.ShapeDtypeStruct((m, 6 * D_MODEL), F32),
        compiler_params=_cparams(("arbitrary",)),
        name="mod",
    )(c_all, w_mod, b_mod.reshape(1, -1))


ROW_TILE = 512
ROW_TILES_P = T_PROMPT // ROW_TILE
_TILES_PER_BATCH = SEQ // ROW_TILE


def _prompt_tile(i):
    return jnp.minimum(i, ROW_TILES_P - 1)


def _prompt_rows_spec(width):
    return pl.BlockSpec((ROW_TILE, width), lambda i: (_prompt_tile(i), 0))


def _sample_rows_spec(width):
    return pl.BlockSpec((DEC_BATCH, width), lambda i: (0, 0))


def _mod_spec_prompt(chunk):
    return pl.BlockSpec((1, 1, D_MODEL), lambda i: (_prompt_tile(i) // _TILES_PER_BATCH, 0, chunk))


def _mod_spec_sample(chunk):
    return pl.BlockSpec((DEC_BATCH, D_MODEL), lambda i: (0, chunk))


def _modval(ref):
    v = ref[...]
    return v.reshape(v.shape[-2], v.shape[-1])


def _prenorm_kernel(xp_ref, xs_ref, g_ref, shp_ref, scp_ref, shs_ref, scs_ref, o_ref):
    i = pl.program_id(0)

    def body(x_ref, sh_ref, sc_ref):
        y = _rms(x_ref[...], g_ref[...])
        return (y * (1.0 + _modval(sc_ref)) + _modval(sh_ref)).astype(BF16)

    @pl.when(i < ROW_TILES_P)
    def _():
        o_ref[...] = body(xp_ref, shp_ref, scp_ref)

    @pl.when(i == ROW_TILES_P)
    def _():
        o_ref[0:DEC_BATCH, :] = body(xs_ref, shs_ref, scs_ref)


def _prenorm(x_p, x_s, g, mod_p3, mod_s):
    return pl.pallas_call(
        _prenorm_kernel,
        grid=(ROW_TILES_P + 1,),
        in_specs=[_prompt_rows_spec(D_MODEL), _sample_rows_spec(D_MODEL),
                  pl.BlockSpec((1, D_MODEL), lambda i: (0, 0)),
                  _mod_spec_prompt(0), _mod_spec_prompt(1), _mod_spec_sample(0), _mod_spec_sample(1)],
        out_specs=pl.BlockSpec((ROW_TILE, D_MODEL), lambda i: (i, 0)),
        out_shape=jax.ShapeDtypeStruct((T_ALL, D_MODEL), BF16),
        compiler_params=_cparams(("arbitrary",)),
        name="prenorm",
    )(x_p, x_s, g, mod_p3, mod_p3, mod_s, mod_s)


def _inproj_kernel(h_ref, wt_ref, o_ref, wb_ref):
    @pl.when(pl.program_id(1) == 0)
    def _():
        wb_ref[...] = wt_ref[...].astype(BF16)

    o_ref[...] = lax.dot_general(h_ref[...], wb_ref[...], (((1,), (1,)), ((), ())),
                                 preferred_element_type=F32).astype(o_ref.dtype)


def _inproj(h_all, wt, row0, nrows, out_dtype, tn, name):
    tm = 1664
    joff = row0 // tn
    return pl.pallas_call(
        _inproj_kernel,
        grid=(nrows // tn, T_ALL // tm),
        in_specs=[pl.BlockSpec((tm, D_MODEL), lambda j, i: (i, 0)),
                  pl.BlockSpec((tn, D_MODEL), lambda j, i: (j + joff, 0))],
        out_specs=pl.BlockSpec((tm, tn), lambda j, i: (i, j)),
        out_shape=jax.ShapeDtypeStruct((T_ALL, nrows), out_dtype),
        scratch_shapes=[pltpu.VMEM((tn, D_MODEL), BF16)],
        compiler_params=_cparams(("arbitrary", "arbitrary")),
        name=name,
    )(h_all, wt)


def _log_decay(ac, wal, bal):
    x = jnp.dot(ac.astype(BF16), wal, preferred_element_type=F32) + bal
    return (jnp.minimum(x, 0.0) - jnp.log1p(jnp.exp(-jnp.abs(x)))) / GATE_TAU


def _mixp_kernel(u_ref, q_ref, k_ref, v_ref, r_ref, ac_ref, ga_ref, gb_ref, wal_ref, bal_ref,
                 wpool_ref, pscale_ref, ghead_ref, m_ref, st_ref,
                 ext_ref, state_ref, b_ref, q32_ref, k32_ref, v32_ref, o_ref):
    R = MIX_ROWS
    h = pl.program_id(1)
    t = pl.program_id(2)

    @pl.when(t == 0)
    def _():
        ext_ref[0:16, :] = jnp.zeros((16, POOL_GROUP_IN), F32)
        state_ref[...] = jnp.zeros_like(state_ref)

    u = u_ref[...]
    ext_ref[16:16 + R, :] = u
    e = ext_ref[...]
    s2 = e + pltpu.roll(e, 1, axis=0)
    s4 = s2 + pltpu.roll(s2, 2, axis=0)
    s8 = s4 + pltpu.roll(s4, 4, axis=0)
    s16 = s8 + pltpu.roll(s8, 8, axis=0)
    s = jnp.where(h == 0, s2, jnp.where(h == 1, s4, jnp.where(h == 2, s8, s16)))[16:, :]
    window = jnp.left_shift(2, h)
    pos = t * R + lax.broadcasted_iota(jnp.int32, (R, 1), 0)
    cnt = jnp.minimum(pos + 1, window).astype(F32)
    p = s / cnt - u
    a_out = jnp.dot(p.astype(BF16), wpool_ref[0], preferred_element_type=F32) * pscale_ref[...]
    ext_ref[0:16, :] = ext_ref[R:R + 16, :]

    C = GLA_CHUNK
    g = _log_decay(ac_ref[...], wal_ref[...], bal_ref[...])
    ri = lax.broadcasted_iota(jnp.int32, (C, C), 0)
    ci = lax.broadcasted_iota(jnp.int32, (C, C), 1)
    causal = ri >= ci
    tri = jnp.where(causal, 1.0, 0.0).astype(BF16)
    g_hi = g.astype(BF16)
    g_lo = (g - g_hi.astype(F32)).astype(BF16)
    for c in range(R // C):
        b_ref[c * C:(c + 1) * C, :] = (
            jnp.dot(tri, g_hi[c * C:(c + 1) * C, :], preferred_element_type=F32)
            + jnp.dot(tri, g_lo[c * C:(c + 1) * C, :], preferred_element_type=F32))
    q32_ref[...] = q_ref[...].astype(F32) * (DK_HEAD ** -0.5)
    k32_ref[...] = k_ref[...].astype(F32)
    v32_ref[...] = v_ref[...].astype(F32)

    nt_dims = (((1,), (1,)), ((), ()))
    tn_dims = (((0,), (0,)), ((), ()))
    sub_iota = lax.broadcasted_iota(jnp.int32, (GLA_SUB, 1), 0)

    for c in range(R // C):
        r0 = c * C
        bc = b_ref[r0:r0 + C, :]
        qc = q32_ref[r0:r0 + C, :]
        kc = k32_ref[r0:r0 + C, :]
        vcb = v32_ref[r0:r0 + C, :].astype(BF16)
        st = state_ref[...]
        b_last = bc[C - 1:C, :]
        e_last = jnp.exp(b_last)
        safe = jnp.min(b_last) >= GLA_SAFE_LOG_DECAY
        qt = (qc * jnp.exp(bc)).astype(BF16)
        o_inter = lax.dot_general(qt, st.astype(BF16), nt_dims, preferred_element_type=F32)

        @pl.when(safe)
        def _(r0=r0, bc=bc, kc=kc, vcb=vcb, st=st, e_last=e_last, qt=qt, o_inter=o_inter):
            kt = kc * jnp.exp(-bc)
            att = lax.dot_general(qt, kt.astype(BF16), nt_dims, preferred_element_type=F32)
            att = jnp.where(causal, att, 0.0).astype(BF16)
            o_ref[r0:r0 + C, :] = o_inter + jnp.dot(att, vcb, preferred_element_type=F32)
            kd = (kt * e_last).astype(BF16)
            state_ref[...] = st * e_last + lax.dot_general(vcb, kd, tn_dims, preferred_element_type=F32)

        @pl.when(jnp.logical_not(safe))
        def _(r0=r0, bc=bc, qc=qc, kc=kc, vcb=vcb, st=st, b_last=b_last, e_last=e_last, o_inter=o_inter):
            o_ref[r0:r0 + C, :] = o_inter
            for i in range(C // GLA_SUB):
                lo = i * GLA_SUB
                bsub = bc[lo:lo + GLA_SUB, :]
                qsub = qc[lo:lo + GLA_SUB, :]
                acc = jnp.zeros((GLA_SUB, DV_HEAD), F32)
                if i > 0:
                    ref_row = bc[lo:lo + 1, :]
                    qi = (qsub * jnp.exp(bsub - ref_row)).astype(BF16)
                    ki = (kc[0:lo, :] * jnp.exp(ref_row - bc[0:lo, :])).astype(BF16)
                    att = lax.dot_general(qi, ki, nt_dims, preferred_element_type=F32)
                    acc = jnp.dot(att.astype(BF16), vcb[0:lo, :], preferred_element_type=F32)

                def diag_body(j, acc, lo=lo, bsub=bsub, qsub=qsub):
                    row = r0 + lo + j
                    bs = b_ref[pl.ds(row, 1), :]
                    ks = k32_ref[pl.ds(row, 1), :]
                    vs = v32_ref[pl.ds(row, 1), :]
                    dec = jnp.exp(jnp.where(sub_iota >= j, bsub - bs, -jnp.inf))
                    col = jnp.sum(qsub * dec * ks, axis=-1, keepdims=True)
                    return acc + col * vs

                acc = lax.fori_loop(0, GLA_SUB, diag_body, acc)
                o_ref[r0 + lo:r0 + lo + GLA_SUB, :] += acc
            kd = (kc * jnp.exp(b_last - bc)).astype(BF16)
            state_ref[...] = st * e_last + lax.dot_general(vcb, kd, tn_dims, preferred_element_type=F32)

    o = _rms(o_ref[...], ghead_ref[...])
    r = r_ref[...].astype(F32)
    b_out = o * (r * _sigmoid(r))
    m = _sigmoid(ga_ref[...].astype(F32)) * a_out + _sigmoid(gb_ref[...].astype(F32)) * b_out
    m_ref[...] = m.astype(BF16)

    @pl.when(t == pl.num_programs(2) - 1)
    def _():
        st_ref[0, 0] = state_ref[...].T


def _mix_prompt(u_all, qkvr, acode, gates, wal, bal, wpool, pscale, ghead):
    R = MIX_ROWS
    nt = SEQ // R

    def rows(b, h, t):
        return b * nt + t

    in_specs = [
        pl.BlockSpec((R, POOL_GROUP_IN), lambda b, h, t: (rows(b, h, t), h)),
        pl.BlockSpec((R, DK_HEAD), lambda b, h, t: (rows(b, h, t), h)),
        pl.BlockSpec((R, DK_HEAD), lambda b, h, t: (rows(b, h, t), GLA_HEADS + h)),
        pl.BlockSpec((R, DV_HEAD), lambda b, h, t: (rows(b, h, t), GLA_HEADS + h)),
        pl.BlockSpec((R, DV_HEAD), lambda b, h, t: (rows(b, h, t), 2 * GLA_HEADS + h)),
        pl.BlockSpec((R, LANES), lambda b, h, t: (rows(b, h, t), 0)),
        pl.BlockSpec((R, DV_HEAD), lambda b, h, t: (rows(b, h, t), h)),
        pl.BlockSpec((R, DV_HEAD), lambda b, h, t: (rows(b, h, t), GLA_HEADS + h)),
        pl.BlockSpec((LANES, DK_HEAD), lambda b, h, t: (0, h)),
        pl.BlockSpec((1, DK_HEAD), lambda b, h, t: (0, h)),
        pl.BlockSpec((1, POOL_GROUP_IN, POOL_GROUP_OUT), lambda b, h, t: (h, 0, 0)),
        pl.BlockSpec((1, POOL_GROUP_OUT), lambda b, h, t: (0, h)),
        pl.BlockSpec((1, DV_HEAD), lambda b, h, t: (0, 0)),
    ]
    out_specs = [
        pl.BlockSpec((R, DV_HEAD), lambda b, h, t: (rows(b, h, t), h)),
        pl.BlockSpec((1, 1, DK_HEAD, DV_HEAD), lambda b, h, t: (b, h, 0, 0)),
    ]
    return pl.pallas_call(
        _mixp_kernel,
        grid=(BATCH, GLA_HEADS, nt),
        in_specs=in_specs,
        out_specs=out_specs,
        out_shape=[jax.ShapeDtypeStruct((T_PROMPT, D_MODEL), BF16),
                   jax.ShapeDtypeStruct((BATCH, GLA_HEADS, DK_HEAD, DV_HEAD), F32)],
        scratch_shapes=[pltpu.VMEM((16 + R, POOL_GROUP_IN), F32),
                        pltpu.VMEM((DV_HEAD, DK_HEAD), F32),
                        pltpu.VMEM((R, DK_HEAD), F32),
                        pltpu.VMEM((R, DK_HEAD), F32),
                        pltpu.VMEM((R, DK_HEAD), F32),
                        pltpu.VMEM((R, DV_HEAD), F32),
                        pltpu.VMEM((R, DV_HEAD), F32)],
        compiler_params=_cparams(("arbitrary", "arbitrary", "arbitrary")),
        name="mix_prompt",
    )(u_all, qkvr, qkvr, qkvr, qkvr, acode, gates, gates, wal, bal, wpool, pscale, ghead)


def _spre_kernel(u_ref, sp_ref, ac_ref, wal_ref, bal_ref, wpool_ref, pscale_ref, a_ref, g_ref):
    u = u_ref[...]
    for gi, w in enumerate(POOL_WINDOWS):
        lo = gi * POOL_GROUP_IN
        s = u[:, lo:lo + POOL_GROUP_IN]
        for j in range(POOL_BUF - (w - 1), POOL_BUF):
            s = s + sp_ref[j, :, lo:lo + POOL_GROUP_IN]
        cnt = float(min(PAST_LEN + 1, w))
        p = s / cnt - u[:, lo:lo + POOL_GROUP_IN]
        a = jnp.dot(p.astype(BF16), wpool_ref[gi], preferred_element_type=F32)
        olo = gi * POOL_GROUP_OUT
        a_ref[:, olo:olo + POOL_GROUP_OUT] = a * pscale_ref[:, olo:olo + POOL_GROUP_OUT]
    g_ref[...] = _log_decay(ac_ref[...], wal_ref[...], bal_ref[...])


def _sample_pre(u_all, sp2, acode, wal, bal, wpool, pscale):
    blk = T_PROMPT // DEC_BATCH
    return pl.pallas_call(
        _spre_kernel,
        grid=(1,),
        in_specs=[pl.BlockSpec((DEC_BATCH, D_POOL), lambda i: (blk, 0)),
                  pl.BlockSpec((POOL_BUF, DEC_BATCH, D_POOL), lambda i: (0, 0, 0)),
                  pl.BlockSpec((DEC_BATCH, LANES), lambda i: (blk, 0)),
                  pl.BlockSpec((LANES, D_K), lambda i: (0, 0)),
                  pl.BlockSpec((1, D_K), lambda i: (0, 0)),
                  pl.BlockSpec((4, POOL_GROUP_IN, POOL_GROUP_OUT), lambda i: (0, 0, 0)),
                  pl.BlockSpec((1, D_MODEL), lambda i: (0, 0))],
        out_specs=[pl.BlockSpec((DEC_BATCH, D_MODEL), lambda i: (0, 0)),
                   pl.BlockSpec((DEC_BATCH, D_K), lambda i: (0, 0))],
        out_shape=[jax.ShapeDtypeStruct((DEC_BATCH, D_MODEL), F32),
                   jax.ShapeDtypeStruct((DEC_BATCH, D_K), F32)],
        compiler_params=_cparams(("arbitrary",)),
        name="sample_pre",
    )(u_all, sp2, acode, wal, bal, wpool, pscale)


SSTATE_SAMPLES = 2


def _sstate_kernel(gkq_ref, v_ref, r_ref, a_ref, gt_ref, s_ref, ghead_ref, so_ref, m_ref):
    for j in range(SSTATE_SAMPLES):
        x = gkq_ref[j]
        rowi = lax.broadcasted_iota(jnp.int32, x.shape, 0)
        x = jnp.where(rowi == 0, jnp.exp(x), jnp.where(rowi == 2, x * (DK_HEAD ** -0.5), x))
        xt = x.T
        v = v_ref[j]
        r = r_ref[j]
        a_out = a_ref[j]
        gates = gt_ref[j]
        for h in range(GLA_HEADS):
            cols = xt[h * DK_HEAD:(h + 1) * DK_HEAD, :]
            dec, kcol, qcol = cols[:, 0:1], cols[:, 1:2], cols[:, 2:3]
            lo = h * DV_HEAD
            vrow = v[:, lo:lo + DV_HEAD]
            s_new = dec * s_ref[j, h] + kcol * vrow
            so_ref[j, h] = s_new
            o = jnp.sum(qcol * s_new, axis=0, keepdims=True)
            o = _rms(o, ghead_ref[...])
            rr = r[:, lo:lo + DV_HEAD]
            b_out = o * (rr * _sigmoid(rr))
            m_ref[j, :, lo:lo + DV_HEAD] = (
                _sigmoid(gates[:, lo:lo + DV_HEAD]) * a_out[:, lo:lo + DV_HEAD]
                + _sigmoid(gates[:, D_MODEL + lo:D_MODEL + lo + DV_HEAD]) * b_out)


def _sample_state(gkq, v3, r3, a3, gates3, state, ghead):
    ns = SSTATE_SAMPLES

    def row3(width):
        return pl.BlockSpec((ns, 1, width), lambda i: (i, 0, 0))

    sspec = pl.BlockSpec((ns, GLA_HEADS, DK_HEAD, DV_HEAD), lambda i: (i, 0, 0, 0))
    return pl.pallas_call(
        _sstate_kernel,
        grid=(DEC_BATCH // ns,),
        in_specs=[pl.BlockSpec((ns, 8, D_K), lambda i: (i, 0, 0)),
                  row3(D_V), row3(D_V), row3(D_MODEL), row3(2 * D_MODEL), sspec,
                  pl.BlockSpec((1, DV_HEAD), lambda i: (0, 0))],
        out_specs=[sspec, row3(D_MODEL)],
        out_shape=[jax.ShapeDtypeStruct((DEC_BATCH, GLA_HEADS, DK_HEAD, DV_HEAD), F32),
                   jax.ShapeDtypeStruct((DEC_BATCH, 1, D_MODEL), F32)],
        compiler_params=_cparams(("arbitrary",)),
        name="sample_state",
    )(gkq, v3, r3, a3, gates3, state, ghead)


def _outproj_kernel(mp_ref, ms_ref, xp_ref, xs_ref, wo_ref, gt1p_ref, sh2p_ref, sc2p_ref,
                    gt1s_ref, sh2s_ref, sc2s_ref, g2_ref, wr_ref, br_ref,
                    x1_ref, h2_ref, ti_ref, tg_ref):
    i = pl.program_id(0)

    def body(nrows, m_ref, x_ref, gt1_ref, sh2_ref, sc2_ref):
        y = jnp.dot(m_ref[...], wo_ref[...], preferred_element_type=F32)
        x1 = x_ref[...] + (1.0 + _modval(gt1_ref)) * y
        x1_ref[0:nrows, :] = x1
        h2 = _rms(x1, g2_ref[...]) * (1.0 + _modval(sc2_ref)) + _modval(sh2_ref)
        h2_ref[0:nrows, :] = h2
        logits = jnp.dot(h2.astype(BF16), wr_ref[...], preferred_element_type=F32) + br_ref[...]
        lane = lax.broadcasted_iota(jnp.int32, logits.shape, 1)
        vals, idxs = [], []
        cur = logits
        for _ in range(TOP_K):
            mx = jnp.max(cur, axis=-1, keepdims=True)
            ix = jnp.min(jnp.where(cur == mx, lane, LANES), axis=-1, keepdims=True)
            vals.append(mx)
            idxs.append(ix)
            cur = jnp.where(lane == ix, -jnp.inf, cur)
        exps = [jnp.exp(v - vals[0]) for v in vals]
        den = exps[0] + exps[1] + exps[2] + exps[3]
        ti = jnp.full(logits.shape, -1, jnp.int32)
        tg = jnp.zeros(logits.shape, F32)
        for k in range(TOP_K):
            ti = jnp.where(lane == k, idxs[k], ti)
            tg = jnp.where(lane == k, exps[k] / den, tg)
        ti_ref[0:nrows, :] = ti
        tg_ref[0:nrows, :] = tg

    @pl.when(i < ROW_TILES_P)
    def _():
        body(ROW_TILE, mp_ref, xp_ref, gt1p_ref, sh2p_ref, sc2p_ref)

    @pl.when(i == ROW_TILES_P)
    def _():
        body(DEC_BATCH, ms_ref, xs_ref, gt1s_ref, sh2s_ref, sc2s_ref)


def _outproj(m_p, m_s, x_p, x_s, wo, mod_p3, mod_s, g2, wr, br):
    return pl.pallas_call(
        _outproj_kernel,
        grid=(ROW_TILES_P + 1,),
        in_specs=[_prompt_rows_spec(D_MODEL), _sample_rows_spec(D_MODEL),
                  _prompt_rows_spec(D_MODEL), _sample_rows_spec(D_MODEL),
                  pl.BlockSpec((D_MODEL, D_MODEL), lambda i: (0, 0)),
                  _mod_spec_prompt(2), _mod_spec_prompt(3), _mod_spec_prompt(4),
                  _mod_spec_sample(2), _mod_spec_sample(3), _mod_spec_sample(4),
                  pl.BlockSpec((1, D_MODEL), lambda i: (0, 0)),
                  pl.BlockSpec((D_MODEL, LANES), lambda i: (0, 0)),
                  pl.BlockSpec((1, LANES), lambda i: (0, 0))],
        out_specs=[pl.BlockSpec((ROW_TILE, D_MODEL), lambda i: (i, 0)),
                   pl.BlockSpec((ROW_TILE, D_MODEL), lambda i: (i, 0)),
                   pl.BlockSpec((ROW_TILE, LANES), lambda i: (i, 0)),
                   pl.BlockSpec((ROW_TILE, LANES), lambda i: (i, 0))],
        out_shape=[jax.ShapeDtypeStruct((T_ALL, D_MODEL), F32),
                   jax.ShapeDtypeStruct((T_ALL, D_MODEL), F32),
                   jax.ShapeDtypeStruct((T_ALL, LANES), jnp.int32),
                   jax.ShapeDtypeStruct((T_ALL, LANES), F32)],
        compiler_params=_cparams(("arbitrary",)),
        name="outproj",
    )(m_p, m_s, x_p, x_s, wo, mod_p3, mod_p3, mod_p3, mod_s, mod_s, mod_s, g2, wr, br)


ROUTE_TILE = 640
ROUTE_TILES = T_ALL // ROUTE_TILE


def _route_kernel(ti_ref, dest_ref, cnt_ref, bstart_ref, run_ref, base_ref):
    phase = pl.program_id(0)
    t = pl.program_id(1)
    ti = ti_ref[...]
    lane = lax.broadcasted_iota(jnp.int32, (ROUTE_TILE, LANES), 1)
    onehots = [lane == ti[:, k:k + 1] for k in range(TOP_K)]
    per_tok = jnp.zeros((ROUTE_TILE, LANES), F32)
    for oh in onehots:
        per_tok = per_tok + jnp.where(oh, 1.0, 0.0)

    @pl.when((phase == 0) & (t == 0))
    def _():
        run_ref[...] = jnp.zeros_like(run_ref)

    @pl.when(phase == 0)
    def _():
        run_ref[...] += jnp.sum(per_tok, axis=0, keepdims=True)

    @pl.when((phase == 1) & (t == 0))
    def _():
        counts = run_ref[...]
        nblk = jnp.floor((counts + (MOE_SB - 1)) * (1.0 / MOE_SB))
        ri = lax.broadcasted_iota(jnp.int32, (LANES, LANES), 0)
        ci = lax.broadcasted_iota(jnp.int32, (LANES, LANES), 1)
        before = jnp.where(ri < ci, 1.0, 0.0).astype(BF16)
        nb8 = jnp.broadcast_to(nblk, (8, LANES)).astype(BF16)
        bstart = jnp.dot(nb8, before, preferred_element_type=F32)[0:1, :]
        cnt_ref[...] = counts.astype(jnp.int32)
        bstart_ref[...] = bstart.astype(jnp.int32)
        base_ref[...] = bstart * MOE_SB
        run_ref[...] = jnp.zeros_like(run_ref)

    @pl.when(phase == 1)
    def _():
        ri = lax.broadcasted_iota(jnp.int32, (ROUTE_TILE, ROUTE_TILE), 0)
        ci = lax.broadcasted_iota(jnp.int32, (ROUTE_TILE, ROUTE_TILE), 1)
        earlier = jnp.where(ri > ci, 1.0, 0.0).astype(BF16)
        prior = jnp.dot(earlier, per_tok.astype(BF16), preferred_element_type=F32)
        pos = prior + (run_ref[...] + base_ref[...])
        dest = jnp.full((ROUTE_TILE, LANES), -1, jnp.int32)
        for k, oh in enumerate(onehots):
            d = jnp.sum(jnp.where(oh, pos, 0.0), axis=-1, keepdims=True).astype(jnp.int32)
            dest = jnp.where(lane == k, d, dest)
        dest_ref[...] = dest
        run_ref[...] += jnp.sum(per_tok, axis=0, keepdims=True)


def _route(ti_all):
    return pl.pallas_call(
        _route_kernel,
        grid=(2, ROUTE_TILES),
        in_specs=[pl.BlockSpec((ROUTE_TILE, LANES), lambda p, t: (t, 0))],
        out_specs=[pl.BlockSpec((ROUTE_TILE, LANES), lambda p, t: (t * p, 0)),
                   pl.BlockSpec((1, LANES), lambda p, t: (0, 0)),
                   pl.BlockSpec((1, LANES), lambda p, t: (0, 0))],
        out_shape=[jax.ShapeDtypeStruct((T_ALL, LANES), jnp.int32),
                   jax.ShapeDtypeStruct((1, LANES), jnp.int32),
                   jax.ShapeDtypeStruct((1, LANES), jnp.int32)],
        scratch_shapes=[pltpu.VMEM((1, LANES), F32), pltpu.VMEM((1, LANES), F32)],
        compiler_params=_cparams(("arbitrary", "arbitrary")),
        name="route",
    )(ti_all)


GATHER_UNROLL = 8
GATHER_RING = 3


def _start_row_gather(n_rows, src_row, hbm_ref, dst_ref, sem, dst_row=lambda g, u: g * GATHER_UNROLL + u):
    def body(g, c):
        for u in range(GATHER_UNROLL):
            pltpu.make_async_copy(hbm_ref.at[pl.ds(src_row(g * GATHER_UNROLL + u), 1)],
                                  dst_ref.at[pl.ds(dst_row(g, u), 1)], sem).start(priority=u % 2)
        return c
    lax.fori_loop(0, n_rows // GATHER_UNROLL, body, 0)


def _wait_row_gather(n_rows, hbm_ref, dst_ref, sem):
    pltpu.make_async_copy(hbm_ref.at[pl.ds(0, n_rows)], dst_ref, sem).wait()


DISP_TOK = 640
DISP_ROWS = DISP_TOK * TOP_K


def _dispatch_kernel(dest_ref, cnt_ref, bstart_ref, nused_ref, h2_ref, xb_hbm, zero_ref, sem, zsem):
    i = pl.program_id(0)

    def body(g, c):
        for u in range(GATHER_UNROLL):
            row = dest_ref[i * DISP_ROWS + g * GATHER_UNROLL + u]
            tok = g * (GATHER_UNROLL // TOP_K) + u // TOP_K
            pltpu.make_async_copy(h2_ref.at[pl.ds(tok, 1)], xb_hbm.at[pl.ds(row, 1)], sem).start(priority=u % 2)
        return c
    lax.fori_loop(0, DISP_ROWS // GATHER_UNROLL, body, 0)

    def pad_range(e):
        nblk = (cnt_ref[e] + MOE_SB - 1) // MOE_SB
        return bstart_ref[e] * MOE_SB + cnt_ref[e], (bstart_ref[e] + nblk) * MOE_SB

    def zero_row(row):
        return pltpu.make_async_copy(zero_ref.at[pl.ds(0, 1)], xb_hbm.at[pl.ds(row, 1)], zsem)

    def zero_block(blk):
        return pltpu.make_async_copy(zero_ref, xb_hbm.at[pl.ds(blk * MOE_SB, MOE_SB)], zsem)

    def for_each_unowned(row_fn, block_fn):
        def per_expert(e, c):
            lo, hi = pad_range(e)
            lax.fori_loop(lo, hi, lambda r, cc: (row_fn(r), cc)[1], 0)
            return c
        lax.fori_loop(0, N_EXPERTS, per_expert, 0)
        lax.fori_loop(nused_ref[0], MOE_NBLK, lambda b, cc: (block_fn(b), cc)[1], 0)

    @pl.when(i == 0)
    def _():
        zero_ref[...] = jnp.zeros_like(zero_ref)
        for_each_unowned(lambda r: zero_row(r).start(), lambda b: zero_block(b).start())

    for _ in range(TOP_K):
        pltpu.make_async_copy(h2_ref, xb_hbm.at[pl.ds(0, DISP_TOK)], sem).wait()

    @pl.when(i == 0)
    def _():
        for_each_unowned(lambda r: zero_row(r).wait(), lambda b: zero_block(b).wait())


def _dispatch(dest, counts, blk_start, nused, h2_all):
    return pl.pallas_call(
        _dispatch_kernel,
        grid_spec=pltpu.PrefetchScalarGridSpec(
            num_scalar_prefetch=4,
            grid=(T_ALL // DISP_TOK,),
            in_specs=[pl.BlockSpec((DISP_TOK, D_MODEL), lambda i, d, c, b, n: (i, 0))],
            out_specs=pl.BlockSpec(memory_space=pl.ANY),
            scratch_shapes=[pltpu.VMEM((MOE_SB, D_MODEL), F32),
                            pltpu.SemaphoreType.DMA(()),
                            pltpu.SemaphoreType.DMA(())]),
        out_shape=jax.ShapeDtypeStruct((MOE_ROWS, D_MODEL), F32),
        compiler_params=_cparams(("arbitrary",)),
        name="dispatch",
    )(dest, counts, blk_start, nused, h2_all)


def _moe_kernel(ie_ref, ib0_ref, inb_ref, ival_ref, nused_ref,
                xb_hbm, wg_ref, wu_ref, wd_ref, bg_ref, bu_ref, bd_ref, yb_hbm,
                xbuf, xstage, acc, wgb, wub, wdb, xsem, ysem):
    i = pl.program_id(0)
    f = pl.program_id(1)
    n_items = pl.num_programs(0)
    nb = inb_ref[i]

    def x_copy(blk, s):
        row0 = (ib0_ref[i] + blk) * MOE_SB
        return pltpu.make_async_copy(xb_hbm.at[pl.ds(row0, MOE_SB)], xstage.at[s], xsem.at[s])

    def y_copy(gblk, r0):
        return pltpu.make_async_copy(acc.at[pl.ds(r0, MOE_SB)], yb_hbm.at[pl.ds(gblk * MOE_SB, MOE_SB)], ysem)

    def wait_y_copies(count):
        def body(b, c):
            y_copy(0, 0).wait()
            return c
        lax.fori_loop(0, count, body, 0)

    @pl.when((f == 0) & (nb > 0))
    def _():
        x_copy(0, 0).start()

    @pl.when((f == 0) & (i > 0))
    def _():
        wait_y_copies(inb_ref[jnp.maximum(i - 1, 0)])

    @pl.when(nb > 0)
    def _():
        wgb[...] = wg_ref[0].astype(BF16)
        wub[...] = wu_ref[0].astype(BF16)
        wdb[...] = wd_ref[0].astype(BF16)

    bg = bg_ref[0]
    bu = bu_ref[0]
    bd = bd_ref[0]

    def run_blocks(first, final):
        def body(blk, c):
            r0 = pl.multiple_of(blk * MOE_SB, MOE_SB)
            if first:
                s = blk % 2
                x_copy(blk, s).wait()
                x_copy(jnp.minimum(blk + 1, nb - 1), 1 - s).start()
                x = xstage[s].astype(BF16)
                xbuf[pl.ds(r0, MOE_SB), :] = x
            else:
                x = xbuf[pl.ds(r0, MOE_SB), :]
            g = jnp.dot(x, wgb[...], preferred_element_type=F32) + bg
            u = jnp.dot(x, wub[...], preferred_element_type=F32) + bu
            xg = jnp.minimum(g, SWIGLU_LIMIT)
            xl = jnp.clip(u, -SWIGLU_LIMIT, SWIGLU_LIMIT)
            act = xg * _sigmoid(SWIGLU_ALPHA * xg) * (xl + 1.0)
            part = jnp.dot(act.astype(BF16), wdb[...], preferred_element_type=F32)
            if first:
                acc[pl.ds(r0, MOE_SB), :] = part + bd
            else:
                acc[pl.ds(r0, MOE_SB), :] += part
            if final:
                y_copy(ib0_ref[i] + blk, r0).start()
            return c
        lax.fori_loop(0, nb, body, 0)

    @pl.when(f == 0)
    def _():
        run_blocks(True, False)

        @pl.when(nb > 0)
        def _():
            x_copy(nb - 1, nb % 2).wait()

    @pl.when((f > 0) & (f < MOE_NF - 1))
    def _():
        run_blocks(False, False)

    @pl.when(f == MOE_NF - 1)
    def _():
        run_blocks(False, True)

    @pl.when((f == MOE_NF - 1) & (i == n_items - 1))
    def _():
        wait_y_copies(nb)
        acc[0:MOE_SB, :] = jnp.zeros((MOE_SB, D_MODEL), F32)

        def zstart(gblk, c):
            y_copy(gblk, 0).start()
            return c
        lax.fori_loop(nused_ref[0], MOE_NBLK, zstart, 0)
        wait_y_copies(MOE_NBLK - nused_ref[0])


def _moe(item_e, item_b0, item_nb, item_valid, nused, xb, w_gu, b_gu3, w_down, b_down3):
    def f_eff(f, ival, i):
        return jnp.where(ival[i] > 0, f, MOE_NF - 1)

    in_specs = [
        pl.BlockSpec(memory_space=pl.ANY),
        pl.BlockSpec((1, D_MODEL, MOE_TF), lambda i, f, ie, ib0, inb, iv, nu: (ie[i], 0, f_eff(f, iv, i))),
        pl.BlockSpec((1, D_MODEL, MOE_TF),
                     lambda i, f, ie, ib0, inb, iv, nu: (ie[i], 0, MOE_NF + f_eff(f, iv, i))),
        pl.BlockSpec((1, MOE_TF, D_MODEL), lambda i, f, ie, ib0, inb, iv, nu: (ie[i], f_eff(f, iv, i), 0)),
        pl.BlockSpec((1, 1, MOE_TF), lambda i, f, ie, ib0, inb, iv, nu: (ie[i], 0, f_eff(f, iv, i))),
        pl.BlockSpec((1, 1, MOE_TF),
                     lambda i, f, ie, ib0, inb, iv, nu: (ie[i], 0, MOE_NF + f_eff(f, iv, i))),
        pl.BlockSpec((1, 1, D_MODEL), lambda i, f, ie, ib0, inb, iv, nu: (ie[i], 0, 0)),
    ]
    return pl.pallas_call(
        _moe_kernel,
        grid_spec=pltpu.PrefetchScalarGridSpec(
            num_scalar_prefetch=5,
            grid=(MOE_NITEMS, MOE_NF),
            in_specs=in_specs,
            out_specs=pl.BlockSpec(memory_space=pl.ANY),
            scratch_shapes=[pltpu.VMEM((MOE_MC, D_MODEL), BF16),
                            pltpu.VMEM((2, MOE_SB, D_MODEL), F32),
                            pltpu.VMEM((MOE_MC, D_MODEL), F32),
                            pltpu.VMEM((D_MODEL, MOE_TF), BF16),
                            pltpu.VMEM((D_MODEL, MOE_TF), BF16),
                            pltpu.VMEM((MOE_TF, D_MODEL), BF16),
                            pltpu.SemaphoreType.DMA((2,)),
                            pltpu.SemaphoreType.DMA(())]),
        out_shape=jax.ShapeDtypeStruct((MOE_ROWS, D_MODEL), F32),
        compiler_params=_cparams(("arbitrary", "arbitrary")),
        name="moe",
    )(item_e, item_b0, item_nb, item_valid, nused, xb, w_gu, w_gu, w_down, b_gu3, b_gu3, b_down3)


FIN_TOK = 128
FIN_ROWS = FIN_TOK * TOP_K


FIN_TILES_P = T_PROMPT // FIN_TOK


def _final_kernel(dest_ref, yb_hbm, x1_ref, tg_ref, gt2p_ref, gt2s_ref, gf_ref, yp_ref, ys_ref, stage_ref, sem):
    i = pl.program_id(0)
    n = pl.num_programs(0)

    def start_tile(tile):
        slot = lax.rem(tile, GATHER_RING)
        _start_row_gather(FIN_ROWS, lambda j: dest_ref[tile * FIN_ROWS + j], yb_hbm, stage_ref.at[slot],
                          sem.at[slot],
                          dst_row=lambda g, u: (u % TOP_K) * FIN_TOK + g * (GATHER_UNROLL // TOP_K) + u // TOP_K)

    @pl.when(i == 0)
    def _():
        for b in range(GATHER_RING - 1):
            start_tile(b)

    @pl.when(i + GATHER_RING - 1 < n)
    def _():
        start_tile(i + GATHER_RING - 1)

    slot = lax.rem(i, GATHER_RING)
    _wait_row_gather(FIN_ROWS, yb_hbm, stage_ref.at[slot], sem.at[slot])
    tg = tg_ref[...]
    f = jnp.zeros((FIN_TOK, D_MODEL), F32)
    for k in range(TOP_K):
        f = f + tg[:, k:k + 1] * stage_ref[slot, k * FIN_TOK:(k + 1) * FIN_TOK, :]

    def out(gt2_ref):
        return _rms(x1_ref[...] + (1.0 + _modval(gt2_ref)) * f, gf_ref[...])

    @pl.when(i < FIN_TILES_P)
    def _():
        yp_ref[...] = out(gt2p_ref)

    @pl.when(i == FIN_TILES_P)
    def _():
        ys_ref[...] = out(gt2s_ref)


def _final(dest, yb, x1_all, tg_all, mod_p3, mod_s, gf):
    def ptile(i):
        return jnp.minimum(i, FIN_TILES_P - 1)

    return pl.pallas_call(
        _final_kernel,
        grid_spec=pltpu.PrefetchScalarGridSpec(
            num_scalar_prefetch=1,
            grid=(FIN_TILES_P + 1,),
            in_specs=[pl.BlockSpec(memory_space=pl.ANY),
                      pl.BlockSpec((FIN_TOK, D_MODEL), lambda i, d: (i, 0)),
                      pl.BlockSpec((FIN_TOK, LANES), lambda i, d: (i, 0)),
                      pl.BlockSpec((1, 1, D_MODEL), lambda i, d: (ptile(i) // (SEQ // FIN_TOK), 0, 5)),
                      pl.BlockSpec((DEC_BATCH, D_MODEL), lambda i, d: (0, 5)),
                      pl.BlockSpec((1, D_MODEL), lambda i, d: (0, 0))],
            out_specs=[pl.BlockSpec((FIN_TOK, D_MODEL), lambda i, d: (ptile(i), 0)),
                       pl.BlockSpec((DEC_BATCH, D_MODEL), lambda i, d: (0, 0))],
            scratch_shapes=[pltpu.VMEM((GATHER_RING, FIN_ROWS, D_MODEL), F32),
                            pltpu.SemaphoreType.DMA((GATHER_RING,))]),
        out_shape=[jax.ShapeDtypeStruct((T_PROMPT, D_MODEL), F32),
                   jax.ShapeDtypeStruct((DEC_BATCH, D_MODEL), F32)],
        compiler_params=_cparams(("arbitrary",)),
        name="final",
    )(dest, yb, x1_all, tg_all, mod_p3, mod_s, gf)


def _work_items(counts, blk_start):
    nblk = (counts + MOE_SB - 1) // MOE_SB
    n_items_e = (nblk + MOE_ITEM_BLOCKS - 1) // MOE_ITEM_BLOCKS
    item_end = jnp.cumsum(n_items_e)
    item_start = item_end - n_items_e
    total_items = item_end[-1]
    it = jnp.arange(MOE_NITEMS, dtype=jnp.int32)
    it_c = jnp.minimum(it, total_items - 1)
    item_e = jnp.minimum(jnp.sum((item_end[None, :] <= it_c[:, None]).astype(jnp.int32), axis=1),
                         N_EXPERTS - 1).astype(jnp.int32)
    jj = it_c - item_start[item_e]
    item_valid = (it < total_items).astype(jnp.int32)
    item_b0 = (blk_start[item_e] + jj * MOE_ITEM_BLOCKS).astype(jnp.int32)
    item_nb = jnp.where(item_valid > 0,
                        jnp.minimum(MOE_ITEM_BLOCKS, nblk[item_e] - jj * MOE_ITEM_BLOCKS), 0).astype(jnp.int32)
    nused = (blk_start[-1] + nblk[-1]).reshape(1).astype(jnp.int32)
    return item_e, item_b0, item_nb, item_valid, nused


def kernel(x_prompt, x_sample, c_prompt, c_sample, state_pool, state_gla, w_mod, b_mod, g_norm1, w_in,
           w_alpha, b_alpha, w_pool, pool_scale, g_head, w_out, g_norm2, w_router, b_router, w_gu, b_gu,
           w_down, b_down, g_final):
    x_p = x_prompt.reshape(T_PROMPT, D_MODEL)
    x_s = x_sample.reshape(DEC_BATCH, D_MODEL)

    c_all = jnp.concatenate([c_prompt, c_sample, jnp.zeros((4, D_MODEL), F32)], axis=0)
    mod = _mod(c_all, w_mod[0], b_mod[0])
    mod_p3 = mod[:BATCH].reshape(BATCH, 1, 6 * D_MODEL)
    mod_s = mod[BATCH:BATCH + DEC_BATCH]

    h_all = _prenorm(x_p, x_s, g_norm1, mod_p3, mod_s)

    wt = jnp.swapaxes(w_in, 1, 2)[0]
    w_a = jnp.pad(wt[COL_A:COL_G], ((0, LANES - ALPHA_RANK), (0, 0)))
    w_g = wt[COL_G:]
    u_all = _inproj(h_all, wt, COL_U, D_POOL, F32, 512, "inproj_u")
    qkvr = _inproj(h_all, wt, COL_Q, COL_A - COL_Q, BF16, 512, "inproj_qkvr")
    acode = _inproj(h_all, w_a, 0, LANES, F32, LANES, "inproj_a")
    gates = _inproj(h_all, w_g, 0, 2 * D_MODEL, BF16, 512, "inproj_g")

    wal = jnp.pad(w_alpha[0], ((0, LANES - ALPHA_RANK), (0, 0))).astype(BF16)
    bal = b_alpha[0].reshape(1, D_K)
    wpool = w_pool[0].astype(BF16)
    ghead = g_head[0].reshape(1, DV_HEAD)

    m_p, new_gla_p = _mix_prompt(u_all, qkvr, acode, gates, wal, bal, wpool, pool_scale, ghead)

    sp2 = jnp.transpose(state_pool[0], (1, 0, 2))
    a_s, g_s = _sample_pre(u_all, sp2, acode, wal, bal, wpool, pool_scale)
    zs = qkvr[T_PROMPT:].astype(F32)
    q_s, k_s = zs[:, 0:D_K], zs[:, D_K:2 * D_K]
    gkq = jnp.stack([g_s, k_s, q_s] + [jnp.zeros_like(g_s)] * 5, axis=1)
    v3 = zs[:, 2 * D_K:2 * D_K + D_V].reshape(DEC_BATCH, 1, D_V)
    r3 = zs[:, 2 * D_K + D_V:].reshape(DEC_BATCH, 1, D_V)
    gates3 = gates[T_PROMPT:].astype(F32).reshape(DEC_BATCH, 1, 2 * D_MODEL)
    new_gla_s, m_s3 = _sample_state(gkq, v3, r3, a_s.reshape(DEC_BATCH, 1, D_MODEL), gates3,
                                    state_gla[0], ghead)
    m_s = m_s3.reshape(DEC_BATCH, D_MODEL).astype(BF16)

    wo = w_out[0].astype(BF16)
    wr = jnp.pad(w_router[0], ((0, 0), (0, LANES - N_EXPERTS))).astype(BF16)
    br = jnp.concatenate([b_router[0], jnp.full((LANES - N_EXPERTS,), -1e30, F32)]).reshape(1, LANES)
    x1_all, h2_all, ti_all, tg_all = _outproj(m_p, m_s, x_p, x_s, wo, mod_p3, mod_s, g_norm2, wr, br)

    dest_t, counts_t, bstart_t = _route(ti_all)
    dest = dest_t[:, :TOP_K].reshape(-1)
    counts, blk_start = counts_t[0, :N_EXPERTS], bstart_t[0, :N_EXPERTS]
    item_e, item_b0, item_nb, item_valid, nused = _work_items(counts, blk_start)

    xb = _dispatch(dest, counts, blk_start, nused, h2_all)
    yb = _moe(item_e, item_b0, item_nb, item_valid, nused, xb, w_gu[0],
              b_gu[0].reshape(N_EXPERTS, 1, 2 * D_FF), w_down[0], b_down[0].reshape(N_EXPERTS, 1, D_MODEL))

    y_p, y_s = _final(dest, yb, x1_all, tg_all, mod_p3, mod_s, g_final.reshape(1, D_MODEL))

    u_p = u_all[:T_PROMPT].reshape(BATCH, SEQ, D_POOL)
    new_pool_p = u_p[:, SEQ - POOL_BUF:, :][None]
    new_pool_s = jnp.concatenate([state_pool[0][:, 1:, :], u_all[T_PROMPT:][:, None, :]], axis=1)[None]
    return (y_p.reshape(BATCH, SEQ, D_MODEL), y_s.reshape(DEC_BATCH, 1, D_MODEL),
            new_pool_p, new_gla_p[None], new_pool_s, new_gla_s[None])
```

```python
import functools

import jax
import jax.numpy as jnp
from jax import lax
from jax.experimental import pallas as pl
from jax.experimental.pallas import tpu as pltpu

F32 = jnp.float32
BF16 = jnp.bfloat16

D_MODEL = 2048
BATCH = 4
SEQ = 2048
DEC_BATCH = 128
PAST_LEN = 16384
POOL_WINDOWS = (2, 4, 8, 16)
D_POOL = D_MODEL // 2
POOL_GROUP_IN = D_POOL // 4
POOL_GROUP_OUT = D_MODEL // 4
POOL_BUF = 15
GLA_HEADS = 4
D_K = D_MODEL // 2
D_V = D_MODEL
DK_HEAD = D_K // GLA_HEADS
DV_HEAD = D_V // GLA_HEADS
ALPHA_RANK = 16
GATE_TAU = 16.0
N_EXPERTS = 32
TOP_K = 4
D_FF = D_MODEL
SWIGLU_LIMIT = 7.0
SWIGLU_ALPHA = 1.702
EPS = 1e-6

LANES = 128
T_PROMPT = BATCH * SEQ
T_ALL = T_PROMPT + DEC_BATCH
N_ASSIGN = T_ALL * TOP_K

COL_U, COL_Q, COL_K, COL_V, COL_R = 0, 1024, 2048, 3072, 5120
COL_A = 7168
COL_G = COL_A + ALPHA_RANK

GLA_CHUNK = 128
GLA_SAFE_LOG_DECAY = -60.0
GLA_SUB = 16
MIX_ROWS = 512

MOE_SB = 256
MOE_ITEM_BLOCKS = 5
MOE_MC = MOE_SB * MOE_ITEM_BLOCKS
MOE_TF = 512
MOE_NF = D_FF // MOE_TF
MOE_NBLK = -(-(N_ASSIGN + N_EXPERTS * (MOE_SB - 1)) // MOE_SB)
MOE_ROWS = MOE_NBLK * MOE_SB
MOE_NITEMS = (MOE_NBLK + N_EXPERTS * (MOE_ITEM_BLOCKS - 1)) // MOE_ITEM_BLOCKS

VMEM_LIMIT = 56 * 1024 * 1024


def _cparams(sem, vmem=VMEM_LIMIT):
    return pltpu.CompilerParams(dimension_semantics=sem, vmem_limit_bytes=vmem)


def _rms(x, g):
    return x * lax.rsqrt(jnp.mean(x * x, axis=-1, keepdims=True) + EPS) * g


def _sigmoid(x):
    return 1.0 / (1.0 + jnp.exp(-x))


def _mod_kernel(c_ref, w_ref, b_ref, o_ref):
    c = c_ref[...]
    s = (c * _sigmoid(c)).astype(BF16)
    o_ref[...] = jnp.dot(s, w_ref[...].astype(BF16), preferred_element_type=F32) + b_ref[...]


def _mod(c_all, w_mod, b_mod):
    m = c_all.shape[0]
    tn = 1024
    return pl.pallas_call(
        _mod_kernel,
        grid=(6 * D_MODEL // tn,),
        in_specs=[pl.BlockSpec((m, D_MODEL), lambda j: (0, 0)),
                  pl.BlockSpec((D_MODEL, tn), lambda j: (0, j)),
                  pl.BlockSpec((1, tn), lambda j: (0, j))],
        out_specs=pl.BlockSpec((m, tn), lambda j: (0, j)),
        out_shape=jax.ShapeDtypeStruct((m, 6 * D_MODEL), F32),
        compiler_params=_cparams(("arbitrary",)),
        name="mod",
    )(c_all, w_mod, b_mod.reshape(1, -1))


ROW_TILE = 512
ROW_TILES_P = T_PROMPT // ROW_TILE
_TILES_PER_BATCH = SEQ // ROW_TILE


def _prompt_tile(i):
    return jnp.minimum(i, ROW_TILES_P - 1)


def _prompt_rows_spec(width):
    return pl.BlockSpec((ROW_TILE, width), lambda i: (_prompt_tile(i), 0))


def _sample_rows_spec(width):
    return pl.BlockSpec((DEC_BATCH, width), lambda i: (0, 0))


def _mod_spec_prompt(chunk):
    return pl.BlockSpec((1, 1, D_MODEL), lambda i: (_prompt_tile(i) // _TILES_PER_BATCH, 0, chunk))


def _mod_spec_sample(chunk):
    return pl.BlockSpec((DEC_BATCH, D_MODEL), lambda i: (0, chunk))


def _modval(ref):
    v = ref[...]
    return v.reshape(v.shape[-2], v.shape[-1])


def _prenorm_kernel(xp_ref, xs_ref, g_ref, shp_ref, scp_ref, shs_ref, scs_ref, o_ref):
    i = pl.program_id(0)

    def body(x_ref, sh_ref, sc_ref):
        y = _rms(x_ref[...], g_ref[...])
        return (y * (1.0 + _modval(sc_ref)) + _modval(sh_ref)).astype(BF16)

    @pl.when(i < ROW_TILES_P)
    def _():
        o_ref[...] = body(xp_ref, shp_ref, scp_ref)

    @pl.when(i == ROW_TILES_P)
    def _():
        o_ref[0:DEC_BATCH, :] = body(xs_ref, shs_ref, scs_ref)


def _prenorm(x_p, x_s, g, mod_p3, mod_s):
    return pl.pallas_call(
        _prenorm_kernel,
        grid=(ROW_TILES_P + 1,),
        in_specs=[_prompt_rows_spec(D_MODEL), _sample_rows_spec(D_MODEL),
                  pl.BlockSpec((1, D_MODEL), lambda i: (0, 0)),
                  _mod_spec_prompt(0), _mod_spec_prompt(1), _mod_spec_sample(0), _mod_spec_sample(1)],
        out_specs=pl.BlockSpec((ROW_TILE, D_MODEL), lambda i: (i, 0)),
        out_shape=jax.ShapeDtypeStruct((T_ALL, D_MODEL), BF16),
        compiler_params=_cparams(("arbitrary",)),
        name="prenorm",
    )(x_p, x_s, g, mod_p3, mod_p3, mod_s, mod_s)


def _inproj_kernel(h_ref, wt_ref, o_ref, wb_ref):
    @pl.when(pl.program_id(1) == 0)
    def _():
        wb_ref[...] = wt_ref[...].astype(BF16)

    o_ref[...] = lax.dot_general(h_ref[...], wb_ref[...], (((1,), (1,)), ((), ())),
                                 preferred_element_type=F32).astype(o_ref.dtype)


def _inproj(h_all, wt, row0, nrows, out_dtype, tn, name):
    tm = 1664
    joff = row0 // tn
    return pl.pallas_call(
        _inproj_kernel,
        grid=(nrows // tn, T_ALL // tm),
        in_specs=[pl.BlockSpec((tm, D_MODEL), lambda j, i: (i, 0)),
                  pl.BlockSpec((tn, D_MODEL), lambda j, i: (j + joff, 0))],
        out_specs=pl.BlockSpec((tm, tn), lambda j, i: (i, j)),
        out_shape=jax.ShapeDtypeStruct((T_ALL, nrows), out_dtype),
        scratch_shapes=[pltpu.VMEM((tn, D_MODEL), BF16)],
        compiler_params=_cparams(("arbitrary", "arbitrary")),
        name=name,
    )(h_all, wt)


def _log_decay(ac, wal, bal):
    x = jnp.dot(ac.astype(BF16), wal, preferred_element_type=F32) + bal
    return (jnp.minimum(x, 0.0) - jnp.log1p(jnp.exp(-jnp.abs(x)))) / GATE_TAU


def _mixp_kernel(u_ref, q_ref, k_ref, v_ref, r_ref, ac_ref, ga_ref, gb_ref, wal_ref, bal_ref,
                 wpool_ref, pscale_ref, ghead_ref, m_ref, st_ref,
                 ext_ref, state_ref, b_ref, q32_ref, k32_ref, v32_ref, o_ref):
    R = MIX_ROWS
    h = pl.program_id(1)
    t = pl.program_id(2)

    @pl.when(t == 0)
    def _():
        ext_ref[0:16, :] = jnp.zeros((16, POOL_GROUP_IN), F32)
        state_ref[...] = jnp.zeros_like(state_ref)

    u = u_ref[...]
    ext_ref[16:16 + R, :] = u
    e = ext_ref[...]
    s2 = e + pltpu.roll(e, 1, axis=0)
    s4 = s2 + pltpu.roll(s2, 2, axis=0)
    s8 = s4 + pltpu.roll(s4, 4, axis=0)
    s16 = s8 + pltpu.roll(s8, 8, axis=0)
    s = jnp.where(h == 0, s2, jnp.where(h == 1, s4, jnp.where(h == 2, s8, s16)))[16:, :]
    window = jnp.left_shift(2, h)
    pos = t * R + lax.broadcasted_iota(jnp.int32, (R, 1), 0)
    cnt = jnp.minimum(pos + 1, window).astype(F32)
    p = s / cnt - u
    a_out = jnp.dot(p.astype(BF16), wpool_ref[0], preferred_element_type=F32) * pscale_ref[...]
    ext_ref[0:16, :] = ext_ref[R:R + 16, :]

    C = GLA_CHUNK
    g = _log_decay(ac_ref[...], wal_ref[...], bal_ref[...])
    ri = lax.broadcasted_iota(jnp.int32, (C, C), 0)
    ci = lax.broadcasted_iota(jnp.int32, (C, C), 1)
    causal = ri >= ci
    tri = jnp.where(causal, 1.0, 0.0).astype(BF16)
    g_hi = g.astype(BF16)
    g_lo = (g - g_hi.astype(F32)).astype(BF16)
    for c in range(R // C):
        b_ref[c * C:(c + 1) * C, :] = (
            jnp.dot(tri, g_hi[c * C:(c + 1) * C, :], preferred_element_type=F32)
            + jnp.dot(tri, g_lo[c * C:(c + 1) * C, :], preferred_element_type=F32))
    q32_ref[...] = q_ref[...].astype(F32) * (DK_HEAD ** -0.5)
    k32_ref[...] = k_ref[...].astype(F32)
    v32_ref[...] = v_ref[...].astype(F32)

    nt_dims = (((1,), (1,)), ((), ()))
    tn_dims = (((0,), (0,)), ((), ()))
    sub_iota = lax.broadcasted_iota(jnp.int32, (GLA_SUB, 1), 0)

    for c in range(R // C):
        r0 = c * C
        bc = b_ref[r0:r0 + C, :]
        qc = q32_ref[r0:r0 + C, :]
        kc = k32_ref[r0:r0 + C, :]
        vcb = v32_ref[r0:r0 + C, :].astype(BF16)
        st = state_ref[...]
        b_last = bc[C - 1:C, :]
        e_last = jnp.exp(b_last)
        safe = jnp.min(b_last) >= GLA_SAFE_LOG_DECAY
        qt = (qc * jnp.exp(bc)).astype(BF16)
        o_inter = lax.dot_general(qt, st.astype(BF16), nt_dims, preferred_element_type=F32)

        @pl.when(safe)
        def _(r0=r0, bc=bc, kc=kc, vcb=vcb, st=st, e_last=e_last, qt=qt, o_inter=o_inter):
            kt = kc * jnp.exp(-bc)
            att = lax.dot_general(qt, kt.astype(BF16), nt_dims, preferred_element_type=F32)
            att = jnp.where(causal, att, 0.0).astype(BF16)
            o_ref[r0:r0 + C, :] = o_inter + jnp.dot(att, vcb, preferred_element_type=F32)
            kd = (kt * e_last).astype(BF16)
            state_ref[...] = st * e_last + lax.dot_general(vcb, kd, tn_dims, preferred_element_type=F32)

        @pl.when(jnp.logical_not(safe))
        def _(r0=r0, bc=bc, qc=qc, kc=kc, vcb=vcb, st=st, b_last=b_last, e_last=e_last, o_inter=o_inter):
            o_ref[r0:r0 + C, :] = o_inter
            for i in range(C // GLA_SUB):
                lo = i * GLA_SUB
                bsub = bc[lo:lo + GLA_SUB, :]
                qsub = qc[lo:lo + GLA_SUB, :]
                acc = jnp.zeros((GLA_SUB, DV_HEAD), F32)
                if i > 0:
                    ref_row = bc[lo:lo + 1, :]
                    qi = (qsub * jnp.exp(bsub - ref_row)).astype(BF16)
                    ki = (kc[0:lo, :] * jnp.exp(ref_row - bc[0:lo, :])).astype(BF16)
                    att = lax.dot_general(qi, ki, nt_dims, preferred_element_type=F32)
                    acc = jnp.dot(att.astype(BF16), vcb[0:lo, :], preferred_element_type=F32)

                def diag_body(j, acc, lo=lo, bsub=bsub, qsub=qsub):
                    row = r0 + lo + j
                    bs = b_ref[pl.ds(row, 1), :]
                    ks = k32_ref[pl.ds(row, 1), :]
                    vs = v32_ref[pl.ds(row, 1), :]
                    dec = jnp.exp(jnp.where(sub_iota >= j, bsub - bs, -jnp.inf))
                    col = jnp.sum(qsub * dec * ks, axis=-1, keepdims=True)
                    return acc + col * vs

                acc = lax.fori_loop(0, GLA_SUB, diag_body, acc)
                o_ref[r0 + lo:r0 + lo + GLA_SUB, :] += acc
            kd = (kc * jnp.exp(b_last - bc)).astype(BF16)
            state_ref[...] = st * e_last + lax.dot_general(vcb, kd, tn_dims, preferred_element_type=F32)

    o = _rms(o_ref[...], ghead_ref[...])
    r = r_ref[...].astype(F32)
    b_out = o * (r * _sigmoid(r))
    m = _sigmoid(ga_ref[...].astype(F32)) * a_out + _sigmoid(gb_ref[...].astype(F32)) * b_out
    m_ref[...] = m.astype(BF16)

    @pl.when(t == pl.num_programs(2) - 1)
    def _():
        st_ref[0, 0] = state_ref[...].T


def _mix_prompt(u_all, qkvr, acode, gates, wal, bal, wpool, pscale, ghead):
    R = MIX_ROWS
    nt = SEQ // R

    def rows(b, h, t):
        return b * nt + t

    in_specs = [
        pl.BlockSpec((R, POOL_GROUP_IN), lambda b, h, t: (rows(b, h, t), h)),
        pl.BlockSpec((R, DK_HEAD), lambda b, h, t: (rows(b, h, t), h)),
        pl.BlockSpec((R, DK_HEAD), lambda b, h, t: (rows(b, h, t), GLA_HEADS + h)),
        pl.BlockSpec((R, DV_HEAD), lambda b, h, t: (rows(b, h, t), GLA_HEADS + h)),
        pl.BlockSpec((R, DV_HEAD), lambda b, h, t: (rows(b, h, t), 2 * GLA_HEADS + h)),
        pl.BlockSpec((R, LANES), lambda b, h, t: (rows(b, h, t), 0)),
        pl.BlockSpec((R, DV_HEAD), lambda b, h, t: (rows(b, h, t), h)),
        pl.BlockSpec((R, DV_HEAD), lambda b, h, t: (rows(b, h, t), GLA_HEADS + h)),
        pl.BlockSpec((LANES, DK_HEAD), lambda b, h, t: (0, h)),
        pl.BlockSpec((1, DK_HEAD), lambda b, h, t: (0, h)),
        pl.BlockSpec((1, POOL_GROUP_IN, POOL_GROUP_OUT), lambda b, h, t: (h, 0, 0)),
        pl.BlockSpec((1, POOL_GROUP_OUT), lambda b, h, t: (0, h)),
        pl.BlockSpec((1, DV_HEAD), lambda b, h, t: (0, 0)),
    ]
    out_specs = [
        pl.BlockSpec((R, DV_HEAD), lambda b, h, t: (rows(b, h, t), h)),
        pl.BlockSpec((1, 1, DK_HEAD, DV_HEAD), lambda b, h, t: (b, h, 0, 0)),
    ]
    return pl.pallas_call(
        _mixp_kernel,
        grid=(BATCH, GLA_HEADS, nt),
        in_specs=in_specs,
        out_specs=out_specs,
        out_shape=[jax.ShapeDtypeStruct((T_PROMPT, D_MODEL), BF16),
                   jax.ShapeDtypeStruct((BATCH, GLA_HEADS, DK_HEAD, DV_HEAD), F32)],
        scratch_shapes=[pltpu.VMEM((16 + R, POOL_GROUP_IN), F32),
                        pltpu.VMEM((DV_HEAD, DK_HEAD), F32),
                        pltpu.VMEM((R, DK_HEAD), F32),
                        pltpu.VMEM((R, DK_HEAD), F32),
                        pltpu.VMEM((R, DK_HEAD), F32),
                        pltpu.VMEM((R, DV_HEAD), F32),
                        pltpu.VMEM((R, DV_HEAD), F32)],
        compiler_params=_cparams(("arbitrary", "arbitrary", "arbitrary")),
        name="mix_prompt",
    )(u_all, qkvr, qkvr, qkvr, qkvr, acode, gates, gates, wal, bal, wpool, pscale, ghead)


def _spre_kernel(u_ref, sp_ref, ac_ref, wal_ref, bal_ref, wpool_ref, pscale_ref, a_ref, g_ref):
    u = u_ref[...]
    for gi, w in enumerate(POOL_WINDOWS):
        lo = gi * POOL_GROUP_IN
        s = u[:, lo:lo + POOL_GROUP_IN]
        for j in range(POOL_BUF - (w - 1), POOL_BUF):
            s = s + sp_ref[j, :, lo:lo + POOL_GROUP_IN]
        cnt = float(min(PAST_LEN + 1, w))
        p = s / cnt - u[:, lo:lo + POOL_GROUP_IN]
        a = jnp.dot(p.astype(BF16), wpool_ref[gi], preferred_element_type=F32)
        olo = gi * POOL_GROUP_OUT
        a_ref[:, olo:olo + POOL_GROUP_OUT] = a * pscale_ref[:, olo:olo + POOL_GROUP_OUT]
    g_ref[...] = _log_decay(ac_ref[...], wal_ref[...], bal_ref[...])


def _sample_pre(u_all, sp2, acode, wal, bal, wpool, pscale):
    blk = T_PROMPT // DEC_BATCH
    return pl.pallas_call(
        _spre_kernel,
        grid=(1,),
        in_specs=[pl.BlockSpec((DEC_BATCH, D_POOL), lambda i: (blk, 0)),
                  pl.BlockSpec((POOL_BUF, DEC_BATCH, D_POOL), lambda i: (0, 0, 0)),
                  pl.BlockSpec((DEC_BATCH, LANES), lambda i: (blk, 0)),
                  pl.BlockSpec((LANES, D_K), lambda i: (0, 0)),
                  pl.BlockSpec((1, D_K), lambda i: (0, 0)),
                  pl.BlockSpec((4, POOL_GROUP_IN, POOL_GROUP_OUT), lambda i: (0, 0, 0)),
                  pl.BlockSpec((1, D_MODEL), lambda i: (0, 0))],
        out_specs=[pl.BlockSpec((DEC_BATCH, D_MODEL), lambda i: (0, 0)),
                   pl.BlockSpec((DEC_BATCH, D_K), lambda i: (0, 0))],
        out_shape=[jax.ShapeDtypeStruct((DEC_BATCH, D_MODEL), F32),
                   jax.ShapeDtypeStruct((DEC_BATCH, D_K), F32)],
        compiler_params=_cparams(("arbitrary",)),
        name="sample_pre",
    )(u_all, sp2, acode, wal, bal, wpool, pscale)


SSTATE_SAMPLES = 4


def _sstate_kernel(gkq_ref, v_ref, r_ref, a_ref, gt_ref, s_ref, ghead_ref, so_ref, m_ref):
    for j in range(SSTATE_SAMPLES):
        x = gkq_ref[j]
        rowi = lax.broadcasted_iota(jnp.int32, x.shape, 0)
        x = jnp.where(rowi == 0, jnp.exp(x), jnp.where(rowi == 2, x * (DK_HEAD ** -0.5), x))
        xt = x.T
        v = v_ref[j]
        r = r_ref[j]
        a_out = a_ref[j]
        gates = gt_ref[j]
        for h in range(GLA_HEADS):
            cols = xt[h * DK_HEAD:(h + 1) * DK_HEAD, :]
            dec, kcol, qcol = cols[:, 0:1], cols[:, 1:2], cols[:, 2:3]
            lo = h * DV_HEAD
            vrow = v[:, lo:lo + DV_HEAD]
            s_new = dec * s_ref[j, h] + kcol * vrow
            so_ref[j, h] = s_new
            o = jnp.sum(qcol * s_new, axis=0, keepdims=True)
            o = _rms(o, ghead_ref[...])
            rr = r[:, lo:lo + DV_HEAD]
            b_out = o * (rr * _sigmoid(rr))
            m_ref[j, :, lo:lo + DV_HEAD] = (
                _sigmoid(gates[:, lo:lo + DV_HEAD]) * a_out[:, lo:lo + DV_HEAD]
                + _sigmoid(gates[:, D_MODEL + lo:D_MODEL + lo + DV_HEAD]) * b_out)


def _sample_state(gkq, v3, r3, a3, gates3, state, ghead):
    ns = SSTATE_SAMPLES

    def row3(width):
        return pl.BlockSpec((ns, 1, width), lambda i: (i, 0, 0))

    sspec = pl.BlockSpec((ns, GLA_HEADS, DK_HEAD, DV_HEAD), lambda i: (i, 0, 0, 0))
    return pl.pallas_call(
        _sstate_kernel,
        grid=(DEC_BATCH // ns,),
        in_specs=[pl.BlockSpec((ns, 8, D_K), lambda i: (i, 0, 0)),
                  row3(D_V), row3(D_V), row3(D_MODEL), row3(2 * D_MODEL), sspec,
                  pl.BlockSpec((1, DV_HEAD), lambda i: (0, 0))],
        out_specs=[sspec, row3(D_MODEL)],
        out_shape=[jax.ShapeDtypeStruct((DEC_BATCH, GLA_HEADS, DK_HEAD, DV_HEAD), F32),
                   jax.ShapeDtypeStruct((DEC_BATCH, 1, D_MODEL), F32)],
        compiler_params=_cparams(("arbitrary",)),
        name="sample_state",
    )(gkq, v3, r3, a3, gates3, state, ghead)


def _outproj_kernel(mp_ref, ms_ref, xp_ref, xs_ref, wo_ref, gt1p_ref, sh2p_ref, sc2p_ref,
                    gt1s_ref, sh2s_ref, sc2s_ref, g2_ref, wr_ref, br_ref,
                    x1_ref, h2_ref, ti_ref, tg_ref):
    i = pl.program_id(0)

    def body(nrows, m_ref, x_ref, gt1_ref, sh2_ref, sc2_ref):
        y = jnp.dot(m_ref[...], wo_ref[...], preferred_element_type=F32)
        x1 = x_ref[...] + (1.0 + _modval(gt1_ref)) * y
        x1_ref[0:nrows, :] = x1
        h2 = _rms(x1, g2_ref[...]) * (1.0 + _modval(sc2_ref)) + _modval(sh2_ref)
        h2_ref[0:nrows, :] = h2
        logits = jnp.dot(h2.astype(BF16), wr_ref[...], preferred_element_type=F32) + br_ref[...]
        lane = lax.broadcasted_iota(jnp.int32, logits.shape, 1)
        vals, idxs = [], []
        cur = logits
        for _ in range(TOP_K):
            mx = jnp.max(cur, axis=-1, keepdims=True)
            ix = jnp.min(jnp.where(cur == mx, lane, LANES), axis=-1, keepdims=True)
            vals.append(mx)
            idxs.append(ix)
            cur = jnp.where(lane == ix, -jnp.inf, cur)
        exps = [jnp.exp(v - vals[0]) for v in vals]
        den = exps[0] + exps[1] + exps[2] + exps[3]
        ti = jnp.full(logits.shape, -1, jnp.int32)
        tg = jnp.zeros(logits.shape, F32)
        for k in range(TOP_K):
            ti = jnp.where(lane == k, idxs[k], ti)
            tg = jnp.where(lane == k, exps[k] / den, tg)
        ti_ref[0:nrows, :] = ti
        tg_ref[0:nrows, :] = tg

    @pl.when(i < ROW_TILES_P)
    def _():
        body(ROW_TILE, mp_ref, xp_ref, gt1p_ref, sh2p_ref, sc2p_ref)

    @pl.when(i == ROW_TILES_P)
    def _():
        body(DEC_BATCH, ms_ref, xs_ref, gt1s_ref, sh2s_ref, sc2s_ref)


def _outproj(m_p, m_s, x_p, x_s, wo, mod_p3, mod_s, g2, wr, br):
    return pl.pallas_call(
        _outproj_kernel,
        grid=(ROW_TILES_P + 1,),
        in_specs=[_prompt_rows_spec(D_MODEL), _sample_rows_spec(D_MODEL),
                  _prompt_rows_spec(D_MODEL), _sample_rows_spec(D_MODEL),
                  pl.BlockSpec((D_MODEL, D_MODEL), lambda i: (0, 0)),
                  _mod_spec_prompt(2), _mod_spec_prompt(3), _mod_spec_prompt(4),
                  _mod_spec_sample(2), _mod_spec_sample(3), _mod_spec_sample(4),
                  pl.BlockSpec((1, D_MODEL), lambda i: (0, 0)),
                  pl.BlockSpec((D_MODEL, LANES), lambda i: (0, 0)),
                  pl.BlockSpec((1, LANES), lambda i: (0, 0))],
        out_specs=[pl.BlockSpec((ROW_TILE, D_MODEL), lambda i: (i, 0)),
                   pl.BlockSpec((ROW_TILE, D_MODEL), lambda i: (i, 0)),
                   pl.BlockSpec((ROW_TILE, LANES), lambda i: (i, 0)),
                   pl.BlockSpec((ROW_TILE, LANES), lambda i: (i, 0))],
        out_shape=[jax.ShapeDtypeStruct((T_ALL, D_MODEL), F32),
                   jax.ShapeDtypeStruct((T_ALL, D_MODEL), F32),
                   jax.ShapeDtypeStruct((T_ALL, LANES), jnp.int32),
                   jax.ShapeDtypeStruct((T_ALL, LANES), F32)],
        compiler_params=_cparams(("arbitrary",)),
        name="outproj",
    )(m_p, m_s, x_p, x_s, wo, mod_p3, mod_p3, mod_p3, mod_s, mod_s, mod_s, g2, wr, br)


ROUTE_TILE = 640
ROUTE_TILES = T_ALL // ROUTE_TILE


def _route_kernel(ti_ref, dest_ref, cnt_ref, bstart_ref, run_ref, base_ref):
    phase = pl.program_id(0)
    t = pl.program_id(1)
    ti = ti_ref[...]
    lane = lax.broadcasted_iota(jnp.int32, (ROUTE_TILE, LANES), 1)
    onehots = [lane == ti[:, k:k + 1] for k in range(TOP_K)]
    per_tok = jnp.zeros((ROUTE_TILE, LANES), F32)
    for oh in onehots:
        per_tok = per_tok + jnp.where(oh, 1.0, 0.0)

    @pl.when((phase == 0) & (t == 0))
    def _():
        run_ref[...] = jnp.zeros_like(run_ref)

    @pl.when(phase == 0)
    def _():
        run_ref[...] += jnp.sum(per_tok, axis=0, keepdims=True)

    @pl.when((phase == 1) & (t == 0))
    def _():
        counts = run_ref[...]
        nblk = jnp.floor((counts + (MOE_SB - 1)) * (1.0 / MOE_SB))
        ri = lax.broadcasted_iota(jnp.int32, (LANES, LANES), 0)
        ci = lax.broadcasted_iota(jnp.int32, (LANES, LANES), 1)
        before = jnp.where(ri < ci, 1.0, 0.0).astype(BF16)
        nb8 = jnp.broadcast_to(nblk, (8, LANES)).astype(BF16)
        bstart = jnp.dot(nb8, before, preferred_element_type=F32)[0:1, :]
        cnt_ref[...] = counts.astype(jnp.int32)
        bstart_ref[...] = bstart.astype(jnp.int32)
        base_ref[...] = bstart * MOE_SB
        run_ref[...] = jnp.zeros_like(run_ref)

    @pl.when(phase == 1)
    def _():
        ri = lax.broadcasted_iota(jnp.int32, (ROUTE_TILE, ROUTE_TILE), 0)
        ci = lax.broadcasted_iota(jnp.int32, (ROUTE_TILE, ROUTE_TILE), 1)
        earlier = jnp.where(ri > ci, 1.0, 0.0).astype(BF16)
        prior = jnp.dot(earlier, per_tok.astype(BF16), preferred_element_type=F32)
        pos = prior + (run_ref[...] + base_ref[...])
        dest = jnp.full((ROUTE_TILE, LANES), -1, jnp.int32)
        for k, oh in enumerate(onehots):
            d = jnp.sum(jnp.where(oh, pos, 0.0), axis=-1, keepdims=True).astype(jnp.int32)
            dest = jnp.where(lane == k, d, dest)
        dest_ref[...] = dest
        run_ref[...] += jnp.sum(per_tok, axis=0, keepdims=True)


def _route(ti_all):
    return pl.pallas_call(
        _route_kernel,
        grid=(2, ROUTE_TILES),
        in_specs=[pl.BlockSpec((ROUTE_TILE, LANES), lambda p, t: (t, 0))],
        out_specs=[pl.BlockSpec((ROUTE_TILE, LANES), lambda p, t: (t * p, 0)),
                   pl.BlockSpec((1, LANES), lambda p, t: (0, 0)),
                   pl.BlockSpec((1, LANES), lambda p, t: (0, 0))],
        out_shape=[jax.ShapeDtypeStruct((T_ALL, LANES), jnp.int32),
                   jax.ShapeDtypeStruct((1, LANES), jnp.int32),
                   jax.ShapeDtypeStruct((1, LANES), jnp.int32)],
        scratch_shapes=[pltpu.VMEM((1, LANES), F32), pltpu.VMEM((1, LANES), F32)],
        compiler_params=_cparams(("arbitrary", "arbitrary")),
        name="route",
    )(ti_all)


GATHER_UNROLL = 8
DISP_TOK = 640
DISP_ROWS = DISP_TOK * TOP_K


def _dispatch_kernel(dest_ref, cnt_ref, bstart_ref, nused_ref, h2_ref, xb_hbm, zero_ref, sem, zsem):
    i = pl.program_id(0)

    def body(g, c):
        for u in range(GATHER_UNROLL):
            row = dest_ref[i * DISP_ROWS + g * GATHER_UNROLL + u]
            tok = g * (GATHER_UNROLL // TOP_K) + u // TOP_K
            pltpu.make_async_copy(h2_ref.at[pl.ds(tok, 1)], xb_hbm.at[pl.ds(row, 1)], sem).start(priority=u % 2)
        return c
    lax.fori_loop(0, DISP_ROWS // GATHER_UNROLL, body, 0)

    def pad_range(e):
        nblk = (cnt_ref[e] + MOE_SB - 1) // MOE_SB
        return bstart_ref[e] * MOE_SB + cnt_ref[e], (bstart_ref[e] + nblk) * MOE_SB

    def zero_row(row):
        return pltpu.make_async_copy(zero_ref.at[pl.ds(0, 1)], xb_hbm.at[pl.ds(row, 1)], zsem)

    def zero_block(blk):
        return pltpu.make_async_copy(zero_ref, xb_hbm.at[pl.ds(blk * MOE_SB, MOE_SB)], zsem)

    def for_each_unowned(row_fn, block_fn):
        def per_expert(e, c):
            lo, hi = pad_range(e)
            lax.fori_loop(lo, hi, lambda r, cc: (row_fn(r), cc)[1], 0)
            return c
        lax.fori_loop(0, N_EXPERTS, per_expert, 0)
        lax.fori_loop(nused_ref[0], MOE_NBLK, lambda b, cc: (block_fn(b), cc)[1], 0)

    @pl.when(i == 0)
    def _():
        zero_ref[...] = jnp.zeros_like(zero_ref)
        for_each_unowned(lambda r: zero_row(r).start(), lambda b: zero_block(b).start())

    for _ in range(TOP_K):
        pltpu.make_async_copy(h2_ref, xb_hbm.at[pl.ds(0, DISP_TOK)], sem).wait()

    @pl.when(i == 0)
    def _():
        for_each_unowned(lambda r: zero_row(r).wait(), lambda b: zero_block(b).wait())


def _dispatch(dest, counts, blk_start, nused, h2_all):
    return pl.pallas_call(
        _dispatch_kernel,
        grid_spec=pltpu.PrefetchScalarGridSpec(
            num_scalar_prefetch=4,
            grid=(T_ALL // DISP_TOK,),
            in_specs=[pl.BlockSpec((DISP_TOK, D_MODEL), lambda i, d, c, b, n: (i, 0))],
            out_specs=pl.BlockSpec(memory_space=pl.ANY),
            scratch_shapes=[pltpu.VMEM((MOE_SB, D_MODEL), F32),
                            pltpu.SemaphoreType.DMA(()),
                            pltpu.SemaphoreType.DMA(())]),
        out_shape=jax.ShapeDtypeStruct((MOE_ROWS, D_MODEL), F32),
        compiler_params=_cparams(("arbitrary",)),
        name="dispatch",
    )(dest, counts, blk_start, nused, h2_all)


def _moe_kernel(ie_ref, ib0_ref, inb_ref, ival_ref, nused_ref,
                xb_hbm, wg_ref, wu_ref, wd_ref, bg_ref, bu_ref, bd_ref, yb_hbm,
                xbuf, xstage, acc, wgb, wub, wdb, xsem, ysem):
    i = pl.program_id(0)
    f = pl.program_id(1)
    n_items = pl.num_programs(0)
    nb = inb_ref[i]

    def x_copy(blk, s):
        row0 = (ib0_ref[i] + blk) * MOE_SB
        return pltpu.make_async_copy(xb_hbm.at[pl.ds(row0, MOE_SB)], xstage.at[s], xsem.at[s])

    def y_copy(gblk, r0):
        return pltpu.make_async_copy(acc.at[pl.ds(r0, MOE_SB)], yb_hbm.at[pl.ds(gblk * MOE_SB, MOE_SB)], ysem)

    def wait_y_copies(count):
        def body(b, c):
            y_copy(0, 0).wait()
            return c
        lax.fori_loop(0, count, body, 0)

    @pl.when((f == 0) & (nb > 0))
    def _():
        x_copy(0, 0).start()

    @pl.when((f == 0) & (i > 0))
    def _():
        wait_y_copies(inb_ref[jnp.maximum(i - 1, 0)])

    bg = bg_ref[0]
    bu = bu_ref[0]
    bd = bd_ref[0]

    def run_blocks(first, final):
        def body(blk, c):
            r0 = blk * MOE_SB if isinstance(blk, int) else pl.multiple_of(blk * MOE_SB, MOE_SB)
            if first:
                s = blk % 2
                x_copy(blk, s).wait()
                x_copy(jnp.minimum(blk + 1, nb - 1), 1 - s).start()
                x = xstage[s].astype(BF16)
                xbuf[pl.ds(r0, MOE_SB), :] = x
            else:
                x = xbuf[pl.ds(r0, MOE_SB), :]
            g = jnp.dot(x, wgb[...], preferred_element_type=F32) + bg
            u = jnp.dot(x, wub[...], preferred_element_type=F32) + bu
            xg = jnp.minimum(g, SWIGLU_LIMIT)
            xl = jnp.clip(u, -SWIGLU_LIMIT, SWIGLU_LIMIT)
            act = xg * _sigmoid(SWIGLU_ALPHA * xg) * (xl + 1.0)
            part = jnp.dot(act.astype(BF16), wdb[...], preferred_element_type=F32)
            if first:
                acc[pl.ds(r0, MOE_SB), :] = part + bd
            else:
                acc[pl.ds(r0, MOE_SB), :] += part
            if final:
                y_copy(ib0_ref[i] + blk, r0).start()
            return c

        wgb[...] = wg_ref[0].astype(BF16)
        wub[...] = wu_ref[0].astype(BF16)
        wdb[...] = wd_ref[0].astype(BF16)
        body(0, 0)
        lax.fori_loop(1, nb, body, 0)

    @pl.when((f == 0) & (nb > 0))
    def _():
        run_blocks(True, False)
        x_copy(nb - 1, nb % 2).wait()

    @pl.when((f > 0) & (f < MOE_NF - 1) & (nb > 0))
    def _():
        run_blocks(False, False)

    @pl.when((f == MOE_NF - 1) & (nb > 0))
    def _():
        run_blocks(False, True)

    @pl.when((f == MOE_NF - 1) & (i == n_items - 1))
    def _():
        wait_y_copies(nb)
        acc[0:MOE_SB, :] = jnp.zeros((MOE_SB, D_MODEL), F32)

        def zstart(gblk, c):
            y_copy(gblk, 0).start()
            return c
        lax.fori_loop(nused_ref[0], MOE_NBLK, zstart, 0)
        wait_y_copies(MOE_NBLK - nused_ref[0])


def _moe(item_e, item_b0, item_nb, item_valid, nused, xb, w_gu, b_gu3, w_down, b_down3):
    def f_eff(f, ival, i):
        return jnp.where(ival[i] > 0, f, MOE_NF - 1)

    in_specs = [
        pl.BlockSpec(memory_space=pl.ANY),
        pl.BlockSpec((1, D_MODEL, MOE_TF), lambda i, f, ie, ib0, inb, iv, nu: (ie[i], 0, f_eff(f, iv, i))),
        pl.BlockSpec((1, D_MODEL, MOE_TF),
                     lambda i, f, ie, ib0, inb, iv, nu: (ie[i], 0, MOE_NF + f_eff(f, iv, i))),
        pl.BlockSpec((1, MOE_TF, D_MODEL), lambda i, f, ie, ib0, inb, iv, nu: (ie[i], f_eff(f, iv, i), 0)),
        pl.BlockSpec((1, 1, MOE_TF), lambda i, f, ie, ib0, inb, iv, nu: (ie[i], 0, f_eff(f, iv, i))),
        pl.BlockSpec((1, 1, MOE_TF),
                     lambda i, f, ie, ib0, inb, iv, nu: (ie[i], 0, MOE_NF + f_eff(f, iv, i))),
        pl.BlockSpec((1, 1, D_MODEL), lambda i, f, ie, ib0, inb, iv, nu: (ie[i], 0, 0)),
    ]
    return pl.pallas_call(
        _moe_kernel,
        grid_spec=pltpu.PrefetchScalarGridSpec(
            num_scalar_prefetch=5,
            grid=(MOE_NITEMS, MOE_NF),
            in_specs=in_specs,
            out_specs=pl.BlockSpec(memory_space=pl.ANY),
            scratch_shapes=[pltpu.VMEM((MOE_MC, D_MODEL), BF16),
                            pltpu.VMEM((2, MOE_SB, D_MODEL), F32),
                            pltpu.VMEM((MOE_MC, D_MODEL), F32),
                            pltpu.VMEM((D_MODEL, MOE_TF), BF16),
                            pltpu.VMEM((D_MODEL, MOE_TF), BF16),
                            pltpu.VMEM((MOE_TF, D_MODEL), BF16),
                            pltpu.SemaphoreType.DMA((2,)),
                            pltpu.SemaphoreType.DMA(())]),
        out_shape=jax.ShapeDtypeStruct((MOE_ROWS, D_MODEL), F32),
        compiler_params=_cparams(("arbitrary", "arbitrary")),
        name="moe",
    )(item_e, item_b0, item_nb, item_valid, nused, xb, w_gu, w_gu, w_down, b_gu3, b_gu3, b_down3)


FIN_TOK = 128
FIN_ROWS = FIN_TOK * TOP_K


FIN_TILES_P = T_PROMPT // FIN_TOK
GATHER_RING = 3


def _final_kernel(dest_ref, yb_hbm, x1_ref, tg_ref, gt2p_ref, gt2s_ref, gf_ref, yp_ref, ys_ref, stage_ref, sem):
    i = pl.program_id(0)
    n = pl.num_programs(0)

    def start_tile(tile):
        slot = lax.rem(tile, GATHER_RING)

        def body(g, c):
            for u in range(GATHER_UNROLL):
                row = dest_ref[tile * FIN_ROWS + g * GATHER_UNROLL + u]
                dst = (u % TOP_K) * FIN_TOK + g * (GATHER_UNROLL // TOP_K) + u // TOP_K
                pltpu.make_async_copy(yb_hbm.at[pl.ds(row, 1)], stage_ref.at[slot, pl.ds(dst, 1)],
                                      sem.at[slot]).start(priority=u % 2)
            return c
        lax.fori_loop(0, FIN_ROWS // GATHER_UNROLL, body, 0)

    @pl.when(i == 0)
    def _():
        for b in range(GATHER_RING - 1):
            start_tile(b)

    @pl.when(i + GATHER_RING - 1 < n)
    def _():
        start_tile(i + GATHER_RING - 1)

    slot = lax.rem(i, GATHER_RING)
    pltpu.make_async_copy(yb_hbm.at[pl.ds(0, FIN_ROWS)], stage_ref.at[slot], sem.at[slot]).wait()
    tg = tg_ref[...]
    f = jnp.zeros((FIN_TOK, D_MODEL), F32)
    for k in range(TOP_K):
        f = f + tg[:, k:k + 1] * stage_ref[slot, k * FIN_TOK:(k + 1) * FIN_TOK, :]

    def out(gt2_ref):
        return _rms(x1_ref[...] + (1.0 + _modval(gt2_ref)) * f, gf_ref[...])

    @pl.when(i < FIN_TILES_P)
    def _():
        yp_ref[...] = out(gt2p_ref)

    @pl.when(i == FIN_TILES_P)
    def _():
        ys_ref[...] = out(gt2s_ref)


def _final(dest, yb, x1_all, tg_all, mod_p3, mod_s, gf):
    def ptile(i):
        return jnp.minimum(i, FIN_TILES_P - 1)

    return pl.pallas_call(
        _final_kernel,
        grid_spec=pltpu.PrefetchScalarGridSpec(
            num_scalar_prefetch=1,
            grid=(FIN_TILES_P + 1,),
            in_specs=[pl.BlockSpec(memory_space=pl.ANY),
                      pl.BlockSpec((FIN_TOK, D_MODEL), lambda i, d: (i, 0)),
                      pl.BlockSpec((FIN_TOK, LANES), lambda i, d: (i, 0)),
                      pl.BlockSpec((1, 1, D_MODEL), lambda i, d: (ptile(i) // (SEQ // FIN_TOK), 0, 5)),
                      pl.BlockSpec((DEC_BATCH, D_MODEL), lambda i, d: (0, 5)),
                      pl.BlockSpec((1, D_MODEL), lambda i, d: (0, 0))],
            out_specs=[pl.BlockSpec((FIN_TOK, D_MODEL), lambda i, d: (ptile(i), 0)),
                       pl.BlockSpec((DEC_BATCH, D_MODEL), lambda i, d: (0, 0))],
            scratch_shapes=[pltpu.VMEM((GATHER_RING, FIN_ROWS, D_MODEL), F32),
                            pltpu.SemaphoreType.DMA((GATHER_RING,))]),
        out_shape=[jax.ShapeDtypeStruct((T_PROMPT, D_MODEL), F32),
                   jax.ShapeDtypeStruct((DEC_BATCH, D_MODEL), F32)],
        compiler_params=_cparams(("arbitrary",)),
        name="final",
    )(dest, yb, x1_all, tg_all, mod_p3, mod_s, gf)


def _work_items(counts, blk_start):
    nblk = (counts + MOE_SB - 1) // MOE_SB
    n_items_e = (nblk + MOE_ITEM_BLOCKS - 1) // MOE_ITEM_BLOCKS
    item_end = jnp.cumsum(n_items_e)
    item_start = item_end - n_items_e
    total_items = item_end[-1]
    it = jnp.arange(MOE_NITEMS, dtype=jnp.int32)
    it_c = jnp.minimum(it, total_items - 1)
    item_e = jnp.minimum(jnp.sum((item_end[None, :] <= it_c[:, None]).astype(jnp.int32), axis=1),
                         N_EXPERTS - 1).astype(jnp.int32)
    jj = it_c - item_start[item_e]
    item_valid = (it < total_items).astype(jnp.int32)
    item_b0 = (blk_start[item_e] + jj * MOE_ITEM_BLOCKS).astype(jnp.int32)
    item_nb = jnp.where(item_valid > 0,
                        jnp.minimum(MOE_ITEM_BLOCKS, nblk[item_e] - jj * MOE_ITEM_BLOCKS), 0).astype(jnp.int32)
    nused = (blk_start[-1] + nblk[-1]).reshape(1).astype(jnp.int32)
    return item_e, item_b0, item_nb, item_valid, nused


def kernel(x_prompt, x_sample, c_prompt, c_sample, state_pool, state_gla, w_mod, b_mod, g_norm1, w_in,
           w_alpha, b_alpha, w_pool, pool_scale, g_head, w_out, g_norm2, w_router, b_router, w_gu, b_gu,
           w_down, b_down, g_final):
    x_p = x_prompt.reshape(T_PROMPT, D_MODEL)
    x_s = x_sample.reshape(DEC_BATCH, D_MODEL)

    c_all = jnp.concatenate([c_prompt, c_sample, jnp.zeros((4, D_MODEL), F32)], axis=0)
    mod = _mod(c_all, w_mod[0], b_mod[0])
    mod_p3 = mod[:BATCH].reshape(BATCH, 1, 6 * D_MODEL)
    mod_s = mod[BATCH:BATCH + DEC_BATCH]

    h_all = _prenorm(x_p, x_s, g_norm1, mod_p3, mod_s)

    wt = jnp.swapaxes(w_in, 1, 2)[0]
    w_a = jnp.pad(wt[COL_A:COL_G], ((0, LANES - ALPHA_RANK), (0, 0)))
    w_g = wt[COL_G:]
    u_all = _inproj(h_all, wt, COL_U, D_POOL, F32, 512, "inproj_u")
    qkvr = _inproj(h_all, wt, COL_Q, COL_A - COL_Q, BF16, 1024, "inproj_qkvr")
    acode = _inproj(h_all, w_a, 0, LANES, F32, LANES, "inproj_a")
    gates = _inproj(h_all, w_g, 0, 2 * D_MODEL, BF16, 1024, "inproj_g")

    wal = jnp.pad(w_alpha[0], ((0, LANES - ALPHA_RANK), (0, 0))).astype(BF16)
    bal = b_alpha[0].reshape(1, D_K)
    wpool = w_pool[0].astype(BF16)
    ghead = g_head[0].reshape(1, DV_HEAD)

    m_p, new_gla_p = _mix_prompt(u_all, qkvr, acode, gates, wal, bal, wpool, pool_scale, ghead)

    sp2 = jnp.transpose(state_pool[0], (1, 0, 2))
    a_s, g_s = _sample_pre(u_all, sp2, acode, wal, bal, wpool, pool_scale)
    zs = qkvr[T_PROMPT:].astype(F32)
    q_s, k_s = zs[:, 0:D_K], zs[:, D_K:2 * D_K]
    gkq = jnp.stack([g_s, k_s, q_s] + [jnp.zeros_like(g_s)] * 5, axis=1)
    v3 = zs[:, 2 * D_K:2 * D_K + D_V].reshape(DEC_BATCH, 1, D_V)
    r3 = zs[:, 2 * D_K + D_V:].reshape(DEC_BATCH, 1, D_V)
    gates3 = gates[T_PROMPT:].astype(F32).reshape(DEC_BATCH, 1, 2 * D_MODEL)
    new_gla_s, m_s3 = _sample_state(gkq, v3, r3, a_s.reshape(DEC_BATCH, 1, D_MODEL), gates3,
                                    state_gla[0], ghead)
    m_s = m_s3.reshape(DEC_BATCH, D_MODEL).astype(BF16)

    wo = w_out[0].astype(BF16)
    wr = jnp.pad(w_router[0], ((0, 0), (0, LANES - N_EXPERTS))).astype(BF16)
    br = jnp.concatenate([b_router[0], jnp.full((LANES - N_EXPERTS,), -1e30, F32)]).reshape(1, LANES)
    x1_all, h2_all, ti_all, tg_all = _outproj(m_p, m_s, x_p, x_s, wo, mod_p3, mod_s, g_norm2, wr, br)

    dest_t, counts_t, bstart_t = _route(ti_all)
    dest = dest_t[:, :TOP_K].reshape(-1)
    counts, blk_start = counts_t[0, :N_EXPERTS], bstart_t[0, :N_EXPERTS]
    item_e, item_b0, item_nb, item_valid, nused = _work_items(counts, blk_start)

    xb = _dispatch(dest, counts, blk_start, nused, h2_all)
    yb = _moe(item_e, item_b0, item_nb, item_valid, nused, xb, w_gu[0],
              b_gu[0].reshape(N_EXPERTS, 1, 2 * D_FF), w_down[0], b_down[0].reshape(N_EXPERTS, 1, D_MODEL))

    y_p, y_s = _final(dest, yb, x1_all, tg_all, mod_p3, mod_s, g_final.reshape(1, D_MODEL))

    u_p = u_all[:T_PROMPT].reshape(BATCH, SEQ, D_POOL)
    new_pool_p = u_p[:, SEQ - POOL_BUF:, :][None]
    new_pool_s = jnp.concatenate([state_pool[0][:, 1:, :], u_all[T_PROMPT:][:, None, :]], axis=1)[None]
    return (y_p.reshape(BATCH, SEQ, D_MODEL), y_s.reshape(DEC_BATCH, 1, D_MODEL),
            new_pool_p, new_gla_p[None], new_pool_s, new_gla_s[None])
```

```python
import functools

import jax
import jax.numpy as jnp
from jax import lax
from jax.experimental import pallas as pl
from jax.experimental.pallas import tpu as pltpu

F32 = jnp.float32
BF16 = jnp.bfloat16

D_MODEL = 2048
BATCH = 4
SEQ = 2048
DEC_BATCH = 128
PAST_LEN = 16384
POOL_WINDOWS = (2, 4, 8, 16)
D_POOL = D_MODEL // 2
POOL_GROUP_IN = D_POOL // 4
POOL_GROUP_OUT = D_MODEL // 4
POOL_BUF = 15
GLA_HEADS = 4
D_K = D_MODEL // 2
D_V = D_MODEL
DK_HEAD = D_K // GLA_HEADS
DV_HEAD = D_V // GLA_HEADS
ALPHA_RANK = 16
GATE_TAU = 16.0
N_EXPERTS = 32
TOP_K = 4
D_FF = D_MODEL
SWIGLU_LIMIT = 7.0
SWIGLU_ALPHA = 1.702
EPS = 1e-6

LANES = 128
T_PROMPT = BATCH * SEQ
T_ALL = T_PROMPT + DEC_BATCH
N_ASSIGN = T_ALL * TOP_K

COL_U, COL_Q, COL_K, COL_V, COL_R = 0, 1024, 2048, 3072, 5120
COL_A = 7168
COL_G = COL_A + ALPHA_RANK

GLA_CHUNK = 128
GLA_SAFE_LOG_DECAY = -60.0
GLA_SUB = 16
MIX_ROWS = 512

MOE_SB = 256
MOE_ITEM_BLOCKS = 5
MOE_MC = MOE_SB * MOE_ITEM_BLOCKS
MOE_TF = 512
MOE_NF = D_FF // MOE_TF
MOE_NBLK = -(-(N_ASSIGN + N_EXPERTS * (MOE_SB - 1)) // MOE_SB)
MOE_ROWS = MOE_NBLK * MOE_SB
MOE_NITEMS = (MOE_NBLK + N_EXPERTS * (MOE_ITEM_BLOCKS - 1)) // MOE_ITEM_BLOCKS

VMEM_LIMIT = 56 * 1024 * 1024


def _cparams(sem, vmem=VMEM_LIMIT):
    return pltpu.CompilerParams(dimension_semantics=sem, vmem_limit_bytes=vmem)


def _rms(x, g):
    return x * lax.rsqrt(jnp.mean(x * x, axis=-1, keepdims=True) + EPS) * g


def _sigmoid(x):
    return 1.0 / (1.0 + jnp.exp(-x))


def _mod_kernel(c_ref, w_ref, b_ref, o_ref):
    c = c_ref[...]
    s = (c * _sigmoid(c)).astype(BF16)
    o_ref[...] = jnp.dot(s, w_ref[...].astype(BF16), preferred_element_type=F32) + b_ref[...]


def _mod(c_all, w_mod, b_mod):
    m = c_all.shape[0]
    tn = 1024
    return pl.pallas_call(
        _mod_kernel,
        grid=(6 * D_MODEL // tn,),
        in_specs=[pl.BlockSpec((m, D_MODEL), lambda j: (0, 0)),
                  pl.BlockSpec((D_MODEL, tn), lambda j: (0, j)),
                  pl.BlockSpec((1, tn), lambda j: (0, j))],
        out_specs=pl.BlockSpec((m, tn), lambda j: (0, j)),
        out_shape=jax.ShapeDtypeStruct((m, 6 * D_MODEL), F32),
        compiler_params=_cparams(("arbitrary",)),
        name="mod",
    )(c_all, w_mod, b_mod.reshape(1, -1))


ROW_TILE = 512
ROW_TILES_P = T_PROMPT // ROW_TILE
_TILES_PER_BATCH = SEQ // ROW_TILE


def _prompt_tile(i):
    return jnp.minimum(i, ROW_TILES_P - 1)


def _prompt_rows_spec(width):
    return pl.BlockSpec((ROW_TILE, width), lambda i: (_prompt_tile(i), 0))


def _sample_rows_spec(width):
    return pl.BlockSpec((DEC_BATCH, width), lambda i: (0, 0))


def _mod_spec_prompt(chunk):
    return pl.BlockSpec((1, 1, D_MODEL), lambda i: (_prompt_tile(i) // _TILES_PER_BATCH, 0, chunk))


def _mod_spec_sample(chunk):
    return pl.BlockSpec((DEC_BATCH, D_MODEL), lambda i: (0, chunk))


def _modval(ref):
    v = ref[...]
    return v.reshape(v.shape[-2], v.shape[-1])


def _prenorm_kernel(xp_ref, xs_ref, g_ref, shp_ref, scp_ref, shs_ref, scs_ref, o_ref):
    i = pl.program_id(0)

    def body(x_ref, sh_ref, sc_ref):
        y = _rms(x_ref[...], g_ref[...])
        return (y * (1.0 + _modval(sc_ref)) + _modval(sh_ref)).astype(BF16)

    @pl.when(i < ROW_TILES_P)
    def _():
        o_ref[...] = body(xp_ref, shp_ref, scp_ref)

    @pl.when(i == ROW_TILES_P)
    def _():
        o_ref[0:DEC_BATCH, :] = body(xs_ref, shs_ref, scs_ref)


def _prenorm(x_p, x_s, g, mod_p3, mod_s):
    return pl.pallas_call(
        _prenorm_kernel,
        grid=(ROW_TILES_P + 1,),
        in_specs=[_prompt_rows_spec(D_MODEL), _sample_rows_spec(D_MODEL),
                  pl.BlockSpec((1, D_MODEL), lambda i: (0, 0)),
                  _mod_spec_prompt(0), _mod_spec_prompt(1), _mod_spec_sample(0), _mod_spec_sample(1)],
        out_specs=pl.BlockSpec((ROW_TILE, D_MODEL), lambda i: (i, 0)),
        out_shape=jax.ShapeDtypeStruct((T_ALL, D_MODEL), BF16),
        compiler_params=_cparams(("arbitrary",)),
        name="prenorm",
    )(x_p, x_s, g, mod_p3, mod_p3, mod_s, mod_s)


def _inproj_kernel(h_ref, wt_ref, o_ref, wb_ref):
    @pl.when(pl.program_id(1) == 0)
    def _():
        wb_ref[...] = wt_ref[...].astype(BF16)

    o_ref[...] = lax.dot_general(h_ref[...], wb_ref[...], (((1,), (1,)), ((), ())),
                                 preferred_element_type=F32).astype(o_ref.dtype)


def _inproj(h_all, wt, row0, nrows, out_dtype, tn, name):
    tm = 1664
    if row0 % tn == 0:
        w_spec = pl.BlockSpec((tn, D_MODEL), lambda j, i: (j + row0 // tn, 0))
    else:
        assert row0 % 16 == 0
        w_spec = pl.BlockSpec((pl.Element(tn), pl.Element(D_MODEL)),
                              lambda j, i: (pl.multiple_of(row0 + j * tn, 16), 0))
    return pl.pallas_call(
        _inproj_kernel,
        grid=(nrows // tn, T_ALL // tm),
        in_specs=[pl.BlockSpec((tm, D_MODEL), lambda j, i: (i, 0)), w_spec],
        out_specs=pl.BlockSpec((tm, tn), lambda j, i: (i, j)),
        out_shape=jax.ShapeDtypeStruct((T_ALL, nrows), out_dtype),
        scratch_shapes=[pltpu.VMEM((tn, D_MODEL), BF16)],
        compiler_params=_cparams(("arbitrary", "arbitrary")),
        name=name,
    )(h_all, wt)


def _log_decay(ac, wal, bal):
    x = jnp.dot(ac.astype(BF16), wal, preferred_element_type=F32) + bal
    return (jnp.minimum(x, 0.0) - jnp.log1p(jnp.exp(-jnp.abs(x)))) / GATE_TAU


def _mixp_kernel(u_ref, q_ref, k_ref, v_ref, r_ref, ac_ref, ga_ref, gb_ref, wal_ref, bal_ref,
                 wpool_ref, pscale_ref, ghead_ref, m_ref, st_ref,
                 ext_ref, state_ref, b_ref, q32_ref, k32_ref, v32_ref, o_ref):
    R = MIX_ROWS
    h = pl.program_id(1)
    t = pl.program_id(2)

    @pl.when(t == 0)
    def _():
        ext_ref[0:16, :] = jnp.zeros((16, POOL_GROUP_IN), F32)
        state_ref[...] = jnp.zeros_like(state_ref)

    u = u_ref[...]
    ext_ref[16:16 + R, :] = u
    e = ext_ref[...]
    s2 = e + pltpu.roll(e, 1, axis=0)
    s4 = s2 + pltpu.roll(s2, 2, axis=0)
    s8 = s4 + pltpu.roll(s4, 4, axis=0)
    s16 = s8 + pltpu.roll(s8, 8, axis=0)
    s = jnp.where(h == 0, s2, jnp.where(h == 1, s4, jnp.where(h == 2, s8, s16)))[16:, :]
    window = jnp.left_shift(2, h)
    pos = t * R + lax.broadcasted_iota(jnp.int32, (R, 1), 0)
    cnt = jnp.minimum(pos + 1, window).astype(F32)
    p = s / cnt - u
    a_out = jnp.dot(p.astype(BF16), wpool_ref[0], preferred_element_type=F32) * pscale_ref[...]
    ext_ref[0:16, :] = ext_ref[R:R + 16, :]

    C = GLA_CHUNK
    g = _log_decay(ac_ref[...], wal_ref[...], bal_ref[...])
    ri = lax.broadcasted_iota(jnp.int32, (C, C), 0)
    ci = lax.broadcasted_iota(jnp.int32, (C, C), 1)
    causal = ri >= ci
    tri = jnp.where(causal, 1.0, 0.0).astype(BF16)
    g_hi = g.astype(BF16)
    g_lo = (g - g_hi.astype(F32)).astype(BF16)
    for c in range(R // C):
        b_ref[c * C:(c + 1) * C, :] = (
            jnp.dot(tri, g_hi[c * C:(c + 1) * C, :], preferred_element_type=F32)
            + jnp.dot(tri, g_lo[c * C:(c + 1) * C, :], preferred_element_type=F32))
    q32_ref[...] = q_ref[...].astype(F32) * (DK_HEAD ** -0.5)
    k32_ref[...] = k_ref[...].astype(F32)
    v32_ref[...] = v_ref[...].astype(F32)

    nt_dims = (((1,), (1,)), ((), ()))
    tn_dims = (((0,), (0,)), ((), ()))
    sub_iota = lax.broadcasted_iota(jnp.int32, (GLA_SUB, 1), 0)

    for c in range(R // C):
        r0 = c * C
        bc = b_ref[r0:r0 + C, :]
        qc = q32_ref[r0:r0 + C, :]
        kc = k32_ref[r0:r0 + C, :]
        vcb = v32_ref[r0:r0 + C, :].astype(BF16)
        st = state_ref[...]
        b_last = bc[C - 1:C, :]
        e_last = jnp.exp(b_last)
        safe = jnp.min(b_last) >= GLA_SAFE_LOG_DECAY
        qt = (qc * jnp.exp(bc)).astype(BF16)
        o_inter = lax.dot_general(qt, st.astype(BF16), nt_dims, preferred_element_type=F32)

        @pl.when(safe)
        def _(r0=r0, bc=bc, kc=kc, vcb=vcb, st=st, e_last=e_last, qt=qt, o_inter=o_inter):
            kt = kc * jnp.exp(-bc)
            att = lax.dot_general(qt, kt.astype(BF16), nt_dims, preferred_element_type=F32)
            att = jnp.where(causal, att, 0.0).astype(BF16)
            o_ref[r0:r0 + C, :] = o_inter + jnp.dot(att, vcb, preferred_element_type=F32)
            kd = (kt * e_last).astype(BF16)
            state_ref[...] = st * e_last + lax.dot_general(vcb, kd, tn_dims, preferred_element_type=F32)

        @pl.when(jnp.logical_not(safe))
        def _(r0=r0, bc=bc, qc=qc, kc=kc, vcb=vcb, st=st, b_last=b_last, e_last=e_last, o_inter=o_inter):
            o_ref[r0:r0 + C, :] = o_inter
            for i in range(C // GLA_SUB):
                lo = i * GLA_SUB
                bsub = bc[lo:lo + GLA_SUB, :]
                qsub = qc[lo:lo + GLA_SUB, :]
                acc = jnp.zeros((GLA_SUB, DV_HEAD), F32)
                if i > 0:
                    ref_row = bc[lo:lo + 1, :]
                    qi = (qsub * jnp.exp(bsub - ref_row)).astype(BF16)
                    ki = (kc[0:lo, :] * jnp.exp(ref_row - bc[0:lo, :])).astype(BF16)
                    att = lax.dot_general(qi, ki, nt_dims, preferred_element_type=F32)
                    acc = jnp.dot(att.astype(BF16), vcb[0:lo, :], preferred_element_type=F32)

                def diag_body(j, acc, lo=lo, bsub=bsub, qsub=qsub):
                    row = r0 + lo + j
                    bs = b_ref[pl.ds(row, 1), :]
                    ks = k32_ref[pl.ds(row, 1), :]
                    vs = v32_ref[pl.ds(row, 1), :]
                    dec = jnp.exp(jnp.where(sub_iota >= j, bsub - bs, -jnp.inf))
                    col = jnp.sum(qsub * dec * ks, axis=-1, keepdims=True)
                    return acc + col * vs

                acc = lax.fori_loop(0, GLA_SUB, diag_body, acc)
                o_ref[r0 + lo:r0 + lo + GLA_SUB, :] += acc
            kd = (kc * jnp.exp(b_last - bc)).astype(BF16)
            state_ref[...] = st * e_last + lax.dot_general(vcb, kd, tn_dims, preferred_element_type=F32)

    o = _rms(o_ref[...], ghead_ref[...])
    r = r_ref[...].astype(F32)
    b_out = o * (r * _sigmoid(r))
    m = _sigmoid(ga_ref[...].astype(F32)) * a_out + _sigmoid(gb_ref[...].astype(F32)) * b_out
    m_ref[...] = m.astype(BF16)

    @pl.when(t == pl.num_programs(2) - 1)
    def _():
        st_ref[0, 0] = state_ref[...].T


def _mix_prompt(u_all, qkvr, acode, gates, wal, bal, wpool, pscale, ghead):
    R = MIX_ROWS
    nt = SEQ // R

    def rows(b, h, t):
        return b * nt + t

    in_specs = [
        pl.BlockSpec((R, POOL_GROUP_IN), lambda b, h, t: (rows(b, h, t), h)),
        pl.BlockSpec((R, DK_HEAD), lambda b, h, t: (rows(b, h, t), h)),
        pl.BlockSpec((R, DK_HEAD), lambda b, h, t: (rows(b, h, t), GLA_HEADS + h)),
        pl.BlockSpec((R, DV_HEAD), lambda b, h, t: (rows(b, h, t), GLA_HEADS + h)),
        pl.BlockSpec((R, DV_HEAD), lambda b, h, t: (rows(b, h, t), 2 * GLA_HEADS + h)),
        pl.BlockSpec((R, LANES), lambda b, h, t: (rows(b, h, t), 0)),
        pl.BlockSpec((R, DV_HEAD), lambda b, h, t: (rows(b, h, t), h)),
        pl.BlockSpec((R, DV_HEAD), lambda b, h, t: (rows(b, h, t), GLA_HEADS + h)),
        pl.BlockSpec((LANES, DK_HEAD), lambda b, h, t: (0, h)),
        pl.BlockSpec((1, DK_HEAD), lambda b, h, t: (0, h)),
        pl.BlockSpec((1, POOL_GROUP_IN, POOL_GROUP_OUT), lambda b, h, t: (h, 0, 0)),
        pl.BlockSpec((1, POOL_GROUP_OUT), lambda b, h, t: (0, h)),
        pl.BlockSpec((1, DV_HEAD), lambda b, h, t: (0, 0)),
    ]
    out_specs = [
        pl.BlockSpec((R, DV_HEAD), lambda b, h, t: (rows(b, h, t), h)),
        pl.BlockSpec((1, 1, DK_HEAD, DV_HEAD), lambda b, h, t: (b, h, 0, 0)),
    ]
    return pl.pallas_call(
        _mixp_kernel,
        grid=(BATCH, GLA_HEADS, nt),
        in_specs=in_specs,
        out_specs=out_specs,
        out_shape=[jax.ShapeDtypeStruct((T_PROMPT, D_MODEL), BF16),
                   jax.ShapeDtypeStruct((BATCH, GLA_HEADS, DK_HEAD, DV_HEAD), F32)],
        scratch_shapes=[pltpu.VMEM((16 + R, POOL_GROUP_IN), F32),
                        pltpu.VMEM((DV_HEAD, DK_HEAD), F32),
                        pltpu.VMEM((R, DK_HEAD), F32),
                        pltpu.VMEM((R, DK_HEAD), F32),
                        pltpu.VMEM((R, DK_HEAD), F32),
                        pltpu.VMEM((R, DV_HEAD), F32),
                        pltpu.VMEM((R, DV_HEAD), F32)],
        compiler_params=_cparams(("arbitrary", "arbitrary", "arbitrary")),
        name="mix_prompt",
    )(u_all, qkvr, qkvr, qkvr, qkvr, acode, gates, gates, wal, bal, wpool, pscale, ghead)


def _spre_kernel(u_ref, sp_ref, ac_ref, wal_ref, bal_ref, wpool_ref, pscale_ref, a_ref, g_ref):
    u = u_ref[...]
    for gi, w in enumerate(POOL_WINDOWS):
        lo = gi * POOL_GROUP_IN
        s = u[:, lo:lo + POOL_GROUP_IN]
        for j in range(POOL_BUF - (w - 1), POOL_BUF):
            s = s + sp_ref[j, :, lo:lo + POOL_GROUP_IN]
        cnt = float(min(PAST_LEN + 1, w))
        p = s / cnt - u[:, lo:lo + POOL_GROUP_IN]
        a = jnp.dot(p.astype(BF16), wpool_ref[gi], preferred_element_type=F32)
        olo = gi * POOL_GROUP_OUT
        a_ref[:, olo:olo + POOL_GROUP_OUT] = a * pscale_ref[:, olo:olo + POOL_GROUP_OUT]
    g_ref[...] = _log_decay(ac_ref[...], wal_ref[...], bal_ref[...])


def _sample_pre(u_all, sp2, acode, wal, bal, wpool, pscale):
    blk = T_PROMPT // DEC_BATCH
    return pl.pallas_call(
        _spre_kernel,
        grid=(1,),
        in_specs=[pl.BlockSpec((DEC_BATCH, D_POOL), lambda i: (blk, 0)),
                  pl.BlockSpec((POOL_BUF, DEC_BATCH, D_POOL), lambda i: (0, 0, 0)),
                  pl.BlockSpec((DEC_BATCH, LANES), lambda i: (blk, 0)),
                  pl.BlockSpec((LANES, D_K), lambda i: (0, 0)),
                  pl.BlockSpec((1, D_K), lambda i: (0, 0)),
                  pl.BlockSpec((4, POOL_GROUP_IN, POOL_GROUP_OUT), lambda i: (0, 0, 0)),
                  pl.BlockSpec((1, D_MODEL), lambda i: (0, 0))],
        out_specs=[pl.BlockSpec((DEC_BATCH, D_MODEL), lambda i: (0, 0)),
                   pl.BlockSpec((DEC_BATCH, D_K), lambda i: (0, 0))],
        out_shape=[jax.ShapeDtypeStruct((DEC_BATCH, D_MODEL), F32),
                   jax.ShapeDtypeStruct((DEC_BATCH, D_K), F32)],
        compiler_params=_cparams(("arbitrary",)),
        name="sample_pre",
    )(u_all, sp2, acode, wal, bal, wpool, pscale)


SSTATE_SAMPLES = 4


def _sstate_kernel(gkq_ref, v_ref, r_ref, a_ref, gt_ref, s_ref, ghead_ref, so_ref, m_ref):
    for j in range(SSTATE_SAMPLES):
        x = gkq_ref[j]
        rowi = lax.broadcasted_iota(jnp.int32, x.shape, 0)
        x = jnp.where(rowi == 0, jnp.exp(x), jnp.where(rowi == 2, x * (DK_HEAD ** -0.5), x))
        xt = x.T
        v = v_ref[j]
        r = r_ref[j]
        a_out = a_ref[j]
        gates = gt_ref[j]
        for h in range(GLA_HEADS):
            cols = xt[h * DK_HEAD:(h + 1) * DK_HEAD, :]
            dec, kcol, qcol = cols[:, 0:1], cols[:, 1:2], cols[:, 2:3]
            lo = h * DV_HEAD
            vrow = v[:, lo:lo + DV_HEAD]
            s_new = dec * s_ref[j, h] + kcol * vrow
            so_ref[j, h] = s_new
            o = jnp.sum(qcol * s_new, axis=0, keepdims=True)
            o = _rms(o, ghead_ref[...])
            rr = r[:, lo:lo + DV_HEAD]
            b_out = o * (rr * _sigmoid(rr))
            m_ref[j, :, lo:lo + DV_HEAD] = (
                _sigmoid(gates[:, lo:lo + DV_HEAD]) * a_out[:, lo:lo + DV_HEAD]
                + _sigmoid(gates[:, D_MODEL + lo:D_MODEL + lo + DV_HEAD]) * b_out)


def _sample_state(gkq, v3, r3, a3, gates3, state, ghead):
    ns = SSTATE_SAMPLES

    def row3(width):
        return pl.BlockSpec((ns, 1, width), lambda i: (i, 0, 0))

    sspec = pl.BlockSpec((ns, GLA_HEADS, DK_HEAD, DV_HEAD), lambda i: (i, 0, 0, 0))
    return pl.pallas_call(
        _sstate_kernel,
        grid=(DEC_BATCH // ns,),
        in_specs=[pl.BlockSpec((ns, 8, D_K), lambda i: (i, 0, 0)),
                  row3(D_V), row3(D_V), row3(D_MODEL), row3(2 * D_MODEL), sspec,
                  pl.BlockSpec((1, DV_HEAD), lambda i: (0, 0))],
        out_specs=[sspec, row3(D_MODEL)],
        out_shape=[jax.ShapeDtypeStruct((DEC_BATCH, GLA_HEADS, DK_HEAD, DV_HEAD), F32),
                   jax.ShapeDtypeStruct((DEC_BATCH, 1, D_MODEL), F32)],
        compiler_params=_cparams(("arbitrary",)),
        name="sample_state",
    )(gkq, v3, r3, a3, gates3, state, ghead)


def _outproj_kernel(mp_ref, ms_ref, xp_ref, xs_ref, wo_ref, gt1p_ref, sh2p_ref, sc2p_ref,
                    gt1s_ref, sh2s_ref, sc2s_ref, g2_ref, wr_ref, br_ref,
                    x1_ref, h2_ref, ti_ref, tg_ref):
    i = pl.program_id(0)

    def body(nrows, m_ref, x_ref, gt1_ref, sh2_ref, sc2_ref):
        y = jnp.dot(m_ref[...], wo_ref[...], preferred_element_type=F32)
        x1 = x_ref[...] + (1.0 + _modval(gt1_ref)) * y
        x1_ref[0:nrows, :] = x1
        h2 = _rms(x1, g2_ref[...]) * (1.0 + _modval(sc2_ref)) + _modval(sh2_ref)
        h2_ref[0:nrows, :] = h2
        h_hi = h2.astype(BF16)
        h_lo = (h2 - h_hi.astype(F32)).astype(BF16)
        p_hi = jnp.dot(h_hi, wr_ref[...], preferred_element_type=F32)
        p_lo = jnp.dot(h_lo, wr_ref[:, 0:LANES], preferred_element_type=F32)
        logits = p_hi[:, 0:LANES] + (p_hi[:, LANES:] + p_lo) + br_ref[...]
        lane = lax.broadcasted_iota(jnp.int32, logits.shape, 1)
        vals, idxs = [], []
        cur = logits
        for _ in range(TOP_K):
            mx = jnp.max(cur, axis=-1, keepdims=True)
            ix = jnp.min(jnp.where(cur == mx, lane, LANES), axis=-1, keepdims=True)
            vals.append(mx)
            idxs.append(ix)
            cur = jnp.where(lane == ix, -jnp.inf, cur)
        exps = [jnp.exp(v - vals[0]) for v in vals]
        den = exps[0] + exps[1] + exps[2] + exps[3]
        ti = jnp.full(logits.shape, -1, jnp.int32)
        tg = jnp.zeros(logits.shape, F32)
        for k in range(TOP_K):
            ti = jnp.where(lane == k, idxs[k], ti)
            tg = jnp.where(lane == k, exps[k] / den, tg)
        ti_ref[0:nrows, :] = ti
        tg_ref[0:nrows, :] = tg

    @pl.when(i < ROW_TILES_P)
    def _():
        body(ROW_TILE, mp_ref, xp_ref, gt1p_ref, sh2p_ref, sc2p_ref)

    @pl.when(i == ROW_TILES_P)
    def _():
        body(DEC_BATCH, ms_ref, xs_ref, gt1s_ref, sh2s_ref, sc2s_ref)


def _outproj(m_p, m_s, x_p, x_s, wo, mod_p3, mod_s, g2, wr, br):
    return pl.pallas_call(
        _outproj_kernel,
        grid=(ROW_TILES_P + 1,),
        in_specs=[_prompt_rows_spec(D_MODEL), _sample_rows_spec(D_MODEL),
                  _prompt_rows_spec(D_MODEL), _sample_rows_spec(D_MODEL),
                  pl.BlockSpec((D_MODEL, D_MODEL), lambda i: (0, 0)),
                  _mod_spec_prompt(2), _mod_spec_prompt(3), _mod_spec_prompt(4),
                  _mod_spec_sample(2), _mod_spec_sample(3), _mod_spec_sample(4),
                  pl.BlockSpec((1, D_MODEL), lambda i: (0, 0)),
                  pl.BlockSpec((D_MODEL, 2 * LANES), lambda i: (0, 0)),
                  pl.BlockSpec((1, LANES), lambda i: (0, 0))],
        out_specs=[pl.BlockSpec((ROW_TILE, D_MODEL), lambda i: (i, 0)),
                   pl.BlockSpec((ROW_TILE, D_MODEL), lambda i: (i, 0)),
                   pl.BlockSpec((ROW_TILE, LANES), lambda i: (i, 0)),
                   pl.BlockSpec((ROW_TILE, LANES), lambda i: (i, 0))],
        out_shape=[jax.ShapeDtypeStruct((T_ALL, D_MODEL), F32),
                   jax.ShapeDtypeStruct((T_ALL, D_MODEL), F32),
                   jax.ShapeDtypeStruct((T_ALL, LANES), jnp.int32),
                   jax.ShapeDtypeStruct((T_ALL, LANES), F32)],
        compiler_params=_cparams(("arbitrary",)),
        name="outproj",
    )(m_p, m_s, x_p, x_s, wo, mod_p3, mod_p3, mod_p3, mod_s, mod_s, mod_s, g2, wr, br)


ROUTE_TILE = 640
ROUTE_TILES = T_ALL // ROUTE_TILE


def _route_kernel(ti_ref, dest_ref, cnt_ref, bstart_ref, run_ref, base_ref):
    phase = pl.program_id(0)
    t = pl.program_id(1)
    ti = ti_ref[...]
    lane = lax.broadcasted_iota(jnp.int32, (ROUTE_TILE, LANES), 1)
    onehots = [lane == ti[:, k:k + 1] for k in range(TOP_K)]
    per_tok = jnp.zeros((ROUTE_TILE, LANES), F32)
    for oh in onehots:
        per_tok = per_tok + jnp.where(oh, 1.0, 0.0)

    @pl.when((phase == 0) & (t == 0))
    def _():
        run_ref[...] = jnp.zeros_like(run_ref)

    @pl.when(phase == 0)
    def _():
        run_ref[...] += jnp.sum(per_tok, axis=0, keepdims=True)

    @pl.when((phase == 1) & (t == 0))
    def _():
        counts = run_ref[...]
        nblk = jnp.floor((counts + (MOE_SB - 1)) * (1.0 / MOE_SB))
        ri = lax.broadcasted_iota(jnp.int32, (LANES, LANES), 0)
        ci = lax.broadcasted_iota(jnp.int32, (LANES, LANES), 1)
        before = jnp.where(ri < ci, 1.0, 0.0).astype(BF16)
        nb8 = jnp.broadcast_to(nblk, (8, LANES)).astype(BF16)
        bstart = jnp.dot(nb8, before, preferred_element_type=F32)[0:1, :]
        cnt_ref[...] = counts.astype(jnp.int32)
        bstart_ref[...] = bstart.astype(jnp.int32)
        base_ref[...] = bstart * MOE_SB
        run_ref[...] = jnp.zeros_like(run_ref)

    @pl.when(phase == 1)
    def _():
        ri = lax.broadcasted_iota(jnp.int32, (ROUTE_TILE, ROUTE_TILE), 0)
        ci = lax.broadcasted_iota(jnp.int32, (ROUTE_TILE, ROUTE_TILE), 1)
        earlier = jnp.where(ri > ci, 1.0, 0.0).astype(BF16)
        prior = jnp.dot(earlier, per_tok.astype(BF16), preferred_element_type=F32)
        pos = prior + (run_ref[...] + base_ref[...])
        dest = jnp.full((ROUTE_TILE, LANES), -1, jnp.int32)
        for k, oh in enumerate(onehots):
            d = jnp.sum(jnp.where(oh, pos, 0.0), axis=-1, keepdims=True).astype(jnp.int32)
            dest = jnp.where(lane == k, d, dest)
        dest_ref[...] = dest
        run_ref[...] += jnp.sum(per_tok, axis=0, keepdims=True)


def _route(ti_all):
    return pl.pallas_call(
        _route_kernel,
        grid=(2, ROUTE_TILES),
        in_specs=[pl.BlockSpec((ROUTE_TILE, LANES), lambda p, t: (t, 0))],
        out_specs=[pl.BlockSpec((ROUTE_TILE, LANES), lambda p, t: (t * p, 0)),
                   pl.BlockSpec((1, LANES), lambda p, t: (0, 0)),
                   pl.BlockSpec((1, LANES), lambda p, t: (0, 0))],
        out_shape=[jax.ShapeDtypeStruct((T_ALL, LANES), jnp.int32),
                   jax.ShapeDtypeStruct((1, LANES), jnp.int32),
                   jax.ShapeDtypeStruct((1, LANES), jnp.int32)],
        scratch_shapes=[pltpu.VMEM((1, LANES), F32), pltpu.VMEM((1, LANES), F32)],
        compiler_params=_cparams(("arbitrary", "arbitrary")),
        name="route",
    )(ti_all)


GATHER_UNROLL = 8
DISP_TOK = 640
DISP_ROWS = DISP_TOK * TOP_K


def _dispatch_kernel(dest_ref, cnt_ref, bstart_ref, nused_ref, h2_ref, xb_hbm, zero_ref, sem, zsem):
    i = pl.program_id(0)

    def body(g, c):
        for u in range(GATHER_UNROLL):
            row = dest_ref[i * DISP_ROWS + g * GATHER_UNROLL + u]
            tok = g * (GATHER_UNROLL // TOP_K) + u // TOP_K
            pltpu.make_async_copy(h2_ref.at[pl.ds(tok, 1)], xb_hbm.at[pl.ds(row, 1)], sem).start(priority=u % 2)
        return c
    lax.fori_loop(0, DISP_ROWS // GATHER_UNROLL, body, 0)

    def pad_range(e):
        nblk = (cnt_ref[e] + MOE_SB - 1) // MOE_SB
        return bstart_ref[e] * MOE_SB + cnt_ref[e], (bstart_ref[e] + nblk) * MOE_SB

    def zero_row(row):
        return pltpu.make_async_copy(zero_ref.at[pl.ds(0, 1)], xb_hbm.at[pl.ds(row, 1)], zsem)

    def zero_block(blk):
        return pltpu.make_async_copy(zero_ref, xb_hbm.at[pl.ds(blk * MOE_SB, MOE_SB)], zsem)

    def for_each_unowned(row_fn, block_fn):
        def per_expert(e, c):
            lo, hi = pad_range(e)
            lax.fori_loop(lo, hi, lambda r, cc: (row_fn(r), cc)[1], 0)
            return c
        lax.fori_loop(0, N_EXPERTS, per_expert, 0)
        lax.fori_loop(nused_ref[0], MOE_NBLK, lambda b, cc: (block_fn(b), cc)[1], 0)

    @pl.when(i == 0)
    def _():
        zero_ref[...] = jnp.zeros_like(zero_ref)
        for_each_unowned(lambda r: zero_row(r).start(), lambda b: zero_block(b).start())

    for _ in range(TOP_K):
        pltpu.make_async_copy(h2_ref, xb_hbm.at[pl.ds(0, DISP_TOK)], sem).wait()

    @pl.when(i == 0)
    def _():
        for_each_unowned(lambda r: zero_row(r).wait(), lambda b: zero_block(b).wait())


def _dispatch(dest, counts, blk_start, nused, h2_all):
    return pl.pallas_call(
        _dispatch_kernel,
        grid_spec=pltpu.PrefetchScalarGridSpec(
            num_scalar_prefetch=4,
            grid=(T_ALL // DISP_TOK,),
            in_specs=[pl.BlockSpec((DISP_TOK, D_MODEL), lambda i, d, c, b, n: (i, 0))],
            out_specs=pl.BlockSpec(memory_space=pl.ANY),
            scratch_shapes=[pltpu.VMEM((MOE_SB, D_MODEL), F32),
                            pltpu.SemaphoreType.DMA(()),
                            pltpu.SemaphoreType.DMA(())]),
        out_shape=jax.ShapeDtypeStruct((MOE_ROWS, D_MODEL), F32),
        compiler_params=_cparams(("arbitrary",)),
        name="dispatch",
    )(dest, counts, blk_start, nused, h2_all)


def _moe_kernel(ie_ref, ib0_ref, inb_ref, ival_ref, nused_ref,
                xb_hbm, wg_ref, wu_ref, wd_ref, bg_ref, bu_ref, bd_ref, yb_hbm,
                xbuf, xstage, acc, wgub, wdb, xsem, ysem):
    i = pl.program_id(0)
    f = pl.program_id(1)
    n_items = pl.num_programs(0)
    nb = inb_ref[i]

    def x_copy(blk, s):
        row0 = (ib0_ref[i] + blk) * MOE_SB
        return pltpu.make_async_copy(xb_hbm.at[pl.ds(row0, MOE_SB)], xstage.at[s], xsem.at[s])

    def y_copy(gblk, r0):
        return pltpu.make_async_copy(acc.at[pl.ds(r0, MOE_SB)], yb_hbm.at[pl.ds(gblk * MOE_SB, MOE_SB)], ysem)

    def wait_y_copies(count):
        def body(b, c):
            y_copy(0, 0).wait()
            return c
        lax.fori_loop(0, count, body, 0)

    @pl.when((f == 0) & (nb > 0))
    def _():
        x_copy(0, 0).start()

    @pl.when((f == 0) & (i > 0))
    def _():
        wait_y_copies(inb_ref[jnp.maximum(i - 1, 0)])

    bg = bg_ref[0]
    bu = bu_ref[0]
    bd = bd_ref[0]

    def fetch_block(blk):
        s = blk % 2
        x_copy(blk, s).wait()
        x_copy(jnp.minimum(blk + 1, nb - 1), 1 - s).start()
        xbuf[pl.ds(pl.multiple_of(blk * MOE_SB, MOE_SB), MOE_SB), :] = xstage[s].astype(BF16)

    def run_blocks(first, final):
        def rows_body(blk, nblk):
            rows = nblk * MOE_SB
            r0 = pl.multiple_of(blk * MOE_SB, MOE_SB)
            if first:
                for b in range(nblk):
                    fetch_block(blk + b)
            x = xbuf[pl.ds(r0, rows), :]
            gu = jnp.dot(x, wgub[...], preferred_element_type=F32)
            g = gu[:, :MOE_TF] + bg
            u = gu[:, MOE_TF:] + bu
            xg = jnp.minimum(g, SWIGLU_LIMIT)
            xl = jnp.clip(u, -SWIGLU_LIMIT, SWIGLU_LIMIT)
            act = xg * _sigmoid(SWIGLU_ALPHA * xg) * (xl + 1.0)
            part = jnp.dot(act.astype(BF16), wdb[...], preferred_element_type=F32)
            if first:
                acc[pl.ds(r0, rows), :] = part + bd
            else:
                acc[pl.ds(r0, rows), :] += part
            if final:
                for b in range(nblk):
                    y_copy(ib0_ref[i] + blk + b, r0 + b * MOE_SB).start()

        wgub[:, :MOE_TF] = wg_ref[0].astype(BF16)
        wgub[:, MOE_TF:] = wu_ref[0].astype(BF16)
        wdb[...] = wd_ref[0].astype(BF16)

        def pair_body(p, c):
            rows_body(2 * p, 2)
            return c
        lax.fori_loop(0, nb // 2, pair_body, 0)

        @pl.when(nb % 2 == 1)
        def _():
            rows_body(nb - 1, 1)

    @pl.when((f == 0) & (nb > 0))
    def _():
        run_blocks(True, False)
        x_copy(nb - 1, nb % 2).wait()

    @pl.when((f > 0) & (f < MOE_NF - 1) & (nb > 0))
    def _():
        run_blocks(False, False)

    @pl.when((f == MOE_NF - 1) & (nb > 0))
    def _():
        run_blocks(False, True)

    @pl.when((f == MOE_NF - 1) & (i == n_items - 1))
    def _():
        wait_y_copies(nb)
        acc[0:MOE_SB, :] = jnp.zeros((MOE_SB, D_MODEL), F32)

        def zstart(gblk, c):
            y_copy(gblk, 0).start()
            return c
        lax.fori_loop(nused_ref[0], MOE_NBLK, zstart, 0)
        wait_y_copies(MOE_NBLK - nused_ref[0])


def _moe(item_e, item_b0, item_nb, item_valid, nused, xb, w_gu, b_gu3, w_down, b_down3):
    def f_eff(f, ival, i):
        return jnp.where(ival[i] > 0, f, MOE_NF - 1)

    in_specs = [
        pl.BlockSpec(memory_space=pl.ANY),
        pl.BlockSpec((1, D_MODEL, MOE_TF), lambda i, f, ie, ib0, inb, iv, nu: (ie[i], 0, f_eff(f, iv, i))),
        pl.BlockSpec((1, D_MODEL, MOE_TF),
                     lambda i, f, ie, ib0, inb, iv, nu: (ie[i], 0, MOE_NF + f_eff(f, iv, i))),
        pl.BlockSpec((1, MOE_TF, D_MODEL), lambda i, f, ie, ib0, inb, iv, nu: (ie[i], f_eff(f, iv, i), 0)),
        pl.BlockSpec((1, 1, MOE_TF), lambda i, f, ie, ib0, inb, iv, nu: (ie[i], 0, f_eff(f, iv, i))),
        pl.BlockSpec((1, 1, MOE_TF),
                     lambda i, f, ie, ib0, inb, iv, nu: (ie[i], 0, MOE_NF + f_eff(f, iv, i))),
        pl.BlockSpec((1, 1, D_MODEL), lambda i, f, ie, ib0, inb, iv, nu: (ie[i], 0, 0)),
    ]
    return pl.pallas_call(
        _moe_kernel,
        grid_spec=pltpu.PrefetchScalarGridSpec(
            num_scalar_prefetch=5,
            grid=(MOE_NITEMS, MOE_NF),
            in_specs=in_specs,
            out_specs=pl.BlockSpec(memory_space=pl.ANY),
            scratch_shapes=[pltpu.VMEM((MOE_MC, D_MODEL), BF16),
                            pltpu.VMEM((2, MOE_SB, D_MODEL), F32),
                            pltpu.VMEM((MOE_MC, D_MODEL), F32),
                            pltpu.VMEM((D_MODEL, 2 * MOE_TF), BF16),
                            pltpu.VMEM((MOE_TF, D_MODEL), BF16),
                            pltpu.SemaphoreType.DMA((2,)),
                            pltpu.SemaphoreType.DMA(())]),
        out_shape=jax.ShapeDtypeStruct((MOE_ROWS, D_MODEL), F32),
        compiler_params=_cparams(("arbitrary", "arbitrary")),
        name="moe",
    )(item_e, item_b0, item_nb, item_valid, nused, xb, w_gu, w_gu, w_down, b_gu3, b_gu3, b_down3)


FIN_TOK = 128
FIN_ROWS = FIN_TOK * TOP_K


FIN_TILES_P = T_PROMPT // FIN_TOK
GATHER_RING = 3


def _final_kernel(dest_ref, yb_hbm, x1_ref, tg_ref, gt2p_ref, gt2s_ref, gf_ref, yp_ref, ys_ref, stage_ref, sem):
    i = pl.program_id(0)
    n = pl.num_programs(0)

    def start_tile(tile):
        slot = lax.rem(tile, GATHER_RING)

        def body(g, c):
            for u in range(GATHER_UNROLL):
                row = dest_ref[tile * FIN_ROWS + g * GATHER_UNROLL + u]
                dst = (u % TOP_K) * FIN_TOK + g * (GATHER_UNROLL // TOP_K) + u // TOP_K
                pltpu.make_async_copy(yb_hbm.at[pl.ds(row, 1)], stage_ref.at[slot, pl.ds(dst, 1)],
                                      sem.at[slot]).start(priority=u % 2)
            return c
        lax.fori_loop(0, FIN_ROWS // GATHER_UNROLL, body, 0)

    @pl.when(i == 0)
    def _():
        for b in range(GATHER_RING - 1):
            start_tile(b)

    @pl.when(i + GATHER_RING - 1 < n)
    def _():
        start_tile(i + GATHER_RING - 1)

    slot = lax.rem(i, GATHER_RING)
    pltpu.make_async_copy(yb_hbm.at[pl.ds(0, FIN_ROWS)], stage_ref.at[slot], sem.at[slot]).wait()
    tg = tg_ref[...]
    f = jnp.zeros((FIN_TOK, D_MODEL), F32)
    for k in range(TOP_K):
        f = f + tg[:, k:k + 1] * stage_ref[slot, k * FIN_TOK:(k + 1) * FIN_TOK, :]

    def out(gt2_ref):
        return _rms(x1_ref[...] + (1.0 + _modval(gt2_ref)) * f, gf_ref[...])

    @pl.when(i < FIN_TILES_P)
    def _():
        yp_ref[...] = out(gt2p_ref)

    @pl.when(i == FIN_TILES_P)
    def _():
        ys_ref[...] = out(gt2s_ref)


def _final(dest, yb, x1_all, tg_all, mod_p3, mod_s, gf):
    def ptile(i):
        return jnp.minimum(i, FIN_TILES_P - 1)

    return pl.pallas_call(
        _final_kernel,
        grid_spec=pltpu.PrefetchScalarGridSpec(
            num_scalar_prefetch=1,
            grid=(FIN_TILES_P + 1,),
            in_specs=[pl.BlockSpec(memory_space=pl.ANY),
                      pl.BlockSpec((FIN_TOK, D_MODEL), lambda i, d: (i, 0)),
                      pl.BlockSpec((FIN_TOK, LANES), lambda i, d: (i, 0)),
                      pl.BlockSpec((1, 1, D_MODEL), lambda i, d: (ptile(i) // (SEQ // FIN_TOK), 0, 5)),
                      pl.BlockSpec((DEC_BATCH, D_MODEL), lambda i, d: (0, 5)),
                      pl.BlockSpec((1, D_MODEL), lambda i, d: (0, 0))],
            out_specs=[pl.BlockSpec((FIN_TOK, D_MODEL), lambda i, d: (ptile(i), 0)),
                       pl.BlockSpec((DEC_BATCH, D_MODEL), lambda i, d: (0, 0))],
            scratch_shapes=[pltpu.VMEM((GATHER_RING, FIN_ROWS, D_MODEL), F32),
                            pltpu.SemaphoreType.DMA((GATHER_RING,))]),
        out_shape=[jax.ShapeDtypeStruct((T_PROMPT, D_MODEL), F32),
                   jax.ShapeDtypeStruct((DEC_BATCH, D_MODEL), F32)],
        compiler_params=_cparams(("arbitrary",)),
        name="final",
    )(dest, yb, x1_all, tg_all, mod_p3, mod_s, gf)


def _work_items(counts, blk_start):
    nblk = (counts + MOE_SB - 1) // MOE_SB
    n_items_e = (nblk + MOE_ITEM_BLOCKS - 1) // MOE_ITEM_BLOCKS
    item_end = jnp.cumsum(n_items_e)
    item_start = item_end - n_items_e
    total_items = item_end[-1]
    it = jnp.arange(MOE_NITEMS, dtype=jnp.int32)
    it_c = jnp.minimum(it, total_items - 1)
    item_e = jnp.minimum(jnp.sum((item_end[None, :] <= it_c[:, None]).astype(jnp.int32), axis=1),
                         N_EXPERTS - 1).astype(jnp.int32)
    jj = it_c - item_start[item_e]
    item_valid = (it < total_items).astype(jnp.int32)
    item_b0 = (blk_start[item_e] + jj * MOE_ITEM_BLOCKS).astype(jnp.int32)
    item_nb = jnp.where(item_valid > 0,
                        jnp.minimum(MOE_ITEM_BLOCKS, nblk[item_e] - jj * MOE_ITEM_BLOCKS), 0).astype(jnp.int32)
    nused = (blk_start[-1] + nblk[-1]).reshape(1).astype(jnp.int32)
    return item_e, item_b0, item_nb, item_valid, nused


def kernel(x_prompt, x_sample, c_prompt, c_sample, state_pool, state_gla, w_mod, b_mod, g_norm1, w_in,
           w_alpha, b_alpha, w_pool, pool_scale, g_head, w_out, g_norm2, w_router, b_router, w_gu, b_gu,
           w_down, b_down, g_final):
    x_p = x_prompt.reshape(T_PROMPT, D_MODEL)
    x_s = x_sample.reshape(DEC_BATCH, D_MODEL)

    c_all = jnp.concatenate([c_prompt, c_sample, jnp.zeros((4, D_MODEL), F32)], axis=0)
    mod = _mod(c_all, w_mod[0], b_mod[0])
    mod_p3 = mod[:BATCH].reshape(BATCH, 1, 6 * D_MODEL)
    mod_s = mod[BATCH:BATCH + DEC_BATCH]

    h_all = _prenorm(x_p, x_s, g_norm1, mod_p3, mod_s)

    wt = jnp.swapaxes(w_in, 1, 2)[0]
    w_a = jnp.pad(wt[COL_A:COL_G], ((0, LANES - ALPHA_RANK), (0, 0)))
    u_all = _inproj(h_all, wt, COL_U, D_POOL, F32, 512, "inproj_u")
    qkvr = _inproj(h_all, wt, COL_Q, COL_A - COL_Q, BF16, 1024, "inproj_qkvr")
    acode = _inproj(h_all, w_a, 0, LANES, F32, LANES, "inproj_a")
    gates = _inproj(h_all, wt, COL_G, 2 * D_MODEL, BF16, 1024, "inproj_g")

    wal = jnp.pad(w_alpha[0], ((0, LANES - ALPHA_RANK), (0, 0))).astype(BF16)
    bal = b_alpha[0].reshape(1, D_K)
    wpool = w_pool[0].astype(BF16)
    ghead = g_head[0].reshape(1, DV_HEAD)

    m_p, new_gla_p = _mix_prompt(u_all, qkvr, acode, gates, wal, bal, wpool, pool_scale, ghead)

    sp2 = jnp.transpose(state_pool[0], (1, 0, 2))
    a_s, g_s = _sample_pre(u_all, sp2, acode, wal, bal, wpool, pool_scale)
    zs = qkvr[T_PROMPT:].astype(F32)
    q_s, k_s = zs[:, 0:D_K], zs[:, D_K:2 * D_K]
    gkq = jnp.stack([g_s, k_s, q_s] + [jnp.zeros_like(g_s)] * 5, axis=1)
    v3 = zs[:, 2 * D_K:2 * D_K + D_V].reshape(DEC_BATCH, 1, D_V)
    r3 = zs[:, 2 * D_K + D_V:].reshape(DEC_BATCH, 1, D_V)
    gates3 = gates[T_PROMPT:].astype(F32).reshape(DEC_BATCH, 1, 2 * D_MODEL)
    new_gla_s, m_s3 = _sample_state(gkq, v3, r3, a_s.reshape(DEC_BATCH, 1, D_MODEL), gates3,
                                    state_gla[0], ghead)
    m_s = m_s3.reshape(DEC_BATCH, D_MODEL).astype(BF16)

    wo = w_out[0].astype(BF16)
    wr32 = jnp.pad(w_router[0], ((0, 0), (0, LANES - N_EXPERTS)))
    wr_hi = wr32.astype(BF16)
    wr = jnp.concatenate([wr_hi, (wr32 - wr_hi.astype(F32)).astype(BF16)], axis=1)
    br = jnp.concatenate([b_router[0], jnp.full((LANES - N_EXPERTS,), -1e30, F32)]).reshape(1, LANES)
    x1_all, h2_all, ti_all, tg_all = _outproj(m_p, m_s, x_p, x_s, wo, mod_p3, mod_s, g_norm2, wr, br)

    dest_t, counts_t, bstart_t = _route(ti_all)
    dest = dest_t[:, :TOP_K].reshape(-1)
    counts, blk_start = counts_t[0, :N_EXPERTS], bstart_t[0, :N_EXPERTS]
    item_e, item_b0, item_nb, item_valid, nused = _work_items(counts, blk_start)

    xb = _dispatch(dest, counts, blk_start, nused, h2_all)
    yb = _moe(item_e, item_b0, item_nb, item_valid, nused, xb, w_gu[0],
              b_gu[0].reshape(N_EXPERTS, 1, 2 * D_FF), w_down[0], b_down[0].reshape(N_EXPERTS, 1, D_MODEL))

    y_p, y_s = _final(dest, yb, x1_all, tg_all, mod_p3, mod_s, g_final.reshape(1, D_MODEL))

    u_p = u_all[:T_PROMPT].reshape(BATCH, SEQ, D_POOL)
    new_pool_p = u_p[:, SEQ - POOL_BUF:, :][None]
    new_pool_s = jnp.concatenate([state_pool[0][:, 1:, :], u_all[T_PROMPT:][:, None, :]], axis=1)[None]
    return (y_p.reshape(BATCH, SEQ, D_MODEL), y_s.reshape(DEC_BATCH, 1, D_MODEL),
            new_pool_p, new_gla_p[None], new_pool_s, new_gla_s[None])
```

```python
import functools

import jax
import jax.numpy as jnp
from jax import lax
from jax.experimental import pallas as pl
from jax.experimental.pallas import tpu as pltpu

F32 = jnp.float32
BF16 = jnp.bfloat16

D_MODEL = 2048
BATCH = 4
SEQ = 2048
DEC_BATCH = 128
PAST_LEN = 16384
POOL_WINDOWS = (2, 4, 8, 16)
D_POOL = D_MODEL // 2
POOL_GROUP_IN = D_POOL // 4
POOL_GROUP_OUT = D_MODEL // 4
POOL_BUF = 15
GLA_HEADS = 4
D_K = D_MODEL // 2
D_V = D_MODEL
DK_HEAD = D_K // GLA_HEADS
DV_HEAD = D_V // GLA_HEADS
ALPHA_RANK = 16
GATE_TAU = 16.0
N_EXPERTS = 32
TOP_K = 4
D_FF = D_MODEL
SWIGLU_LIMIT = 7.0
SWIGLU_ALPHA = 1.702
EPS = 1e-6

LANES = 128
T_PROMPT = BATCH * SEQ
T_ALL = T_PROMPT + DEC_BATCH
N_ASSIGN = T_ALL * TOP_K

COL_U, COL_Q, COL_K, COL_V, COL_R = 0, 1024, 2048, 3072, 5120
COL_A = 7168
COL_G = COL_A + ALPHA_RANK

GLA_CHUNK = 128
GLA_SAFE_LOG_DECAY = -60.0
GLA_SUB = 16
MIX_ROWS = 512

MOE_SB = 256
MOE_ITEM_BLOCKS = 5
MOE_MC = MOE_SB * MOE_ITEM_BLOCKS
MOE_TF = 512
MOE_NF = D_FF // MOE_TF
MOE_NBLK = -(-(N_ASSIGN + N_EXPERTS * (MOE_SB - 1)) // MOE_SB)
MOE_ROWS = MOE_NBLK * MOE_SB
MOE_NITEMS = (MOE_NBLK + N_EXPERTS * (MOE_ITEM_BLOCKS - 1)) // MOE_ITEM_BLOCKS

VMEM_LIMIT = 56 * 1024 * 1024


def _cparams(sem, vmem=VMEM_LIMIT):
    return pltpu.CompilerParams(dimension_semantics=sem, vmem_limit_bytes=vmem)


def _rms(x, g):
    return x * lax.rsqrt(jnp.mean(x * x, axis=-1, keepdims=True) + EPS) * g


def _sigmoid(x):
    return 1.0 / (1.0 + jnp.exp(-x))


def _mod_kernel(c_ref, w_ref, b_ref, o_ref):
    c = c_ref[...]
    s = (c * _sigmoid(c)).astype(BF16)
    o_ref[...] = jnp.dot(s, w_ref[...].astype(BF16), preferred_element_type=F32) + b_ref[...]


def _mod(c_all, w_mod, b_mod):
    m = c_all.shape[0]
    tn = 1024
    return pl.pallas_call(
        _mod_kernel,
        grid=(6 * D_MODEL // tn,),
        in_specs=[pl.BlockSpec((m, D_MODEL), lambda j: (0, 0)),
                  pl.BlockSpec((D_MODEL, tn), lambda j: (0, j)),
                  pl.BlockSpec((1, tn), lambda j: (0, j))],
        out_specs=pl.BlockSpec((m, tn), lambda j: (0, j)),
        out_shape=jax.ShapeDtypeStruct((m, 6 * D_MODEL), F32),
        compiler_params=_cparams(("arbitrary",)),
        name="mod",
    )(c_all, w_mod, b_mod.reshape(1, -1))


ROW_TILE = 512
ROW_TILES_P = T_PROMPT // ROW_TILE
_TILES_PER_BATCH = SEQ // ROW_TILE


def _prompt_tile(i):
    return jnp.minimum(i, ROW_TILES_P - 1)


def _prompt_rows_spec(width):
    return pl.BlockSpec((ROW_TILE, width), lambda i: (_prompt_tile(i), 0))


def _sample_rows_spec(width):
    return pl.BlockSpec((DEC_BATCH, width), lambda i: (0, 0))


def _mod_spec_prompt(chunk):
    return pl.BlockSpec((1, 1, D_MODEL), lambda i: (_prompt_tile(i) // _TILES_PER_BATCH, 0, chunk))


def _mod_spec_sample(chunk):
    return pl.BlockSpec((DEC_BATCH, D_MODEL), lambda i: (0, chunk))


def _modval(ref):
    v = ref[...]
    return v.reshape(v.shape[-2], v.shape[-1])


def _prenorm_kernel(xp_ref, xs_ref, g_ref, shp_ref, scp_ref, shs_ref, scs_ref, o_ref):
    i = pl.program_id(0)

    def body(x_ref, sh_ref, sc_ref):
        y = _rms(x_ref[...], g_ref[...])
        return (y * (1.0 + _modval(sc_ref)) + _modval(sh_ref)).astype(BF16)

    @pl.when(i < ROW_TILES_P)
    def _():
        o_ref[...] = body(xp_ref, shp_ref, scp_ref)

    @pl.when(i == ROW_TILES_P)
    def _():
        o_ref[0:DEC_BATCH, :] = body(xs_ref, shs_ref, scs_ref)


def _prenorm(x_p, x_s, g, mod_p3, mod_s):
    return pl.pallas_call(
        _prenorm_kernel,
        grid=(ROW_TILES_P + 1,),
        in_specs=[_prompt_rows_spec(D_MODEL), _sample_rows_spec(D_MODEL),
                  pl.BlockSpec((1, D_MODEL), lambda i: (0, 0)),
                  _mod_spec_prompt(0), _mod_spec_prompt(1), _mod_spec_sample(0), _mod_spec_sample(1)],
        out_specs=pl.BlockSpec((ROW_TILE, D_MODEL), lambda i: (i, 0)),
        out_shape=jax.ShapeDtypeStruct((T_ALL, D_MODEL), BF16),
        compiler_params=_cparams(("arbitrary",)),
        name="prenorm",
    )(x_p, x_s, g, mod_p3, mod_p3, mod_s, mod_s)


def _inproj_kernel(h_ref, wt_ref, o_ref, wb_ref):
    @pl.when(pl.program_id(1) == 0)
    def _():
        wb_ref[...] = wt_ref[...].astype(BF16)

    o_ref[...] = lax.dot_general(h_ref[...], wb_ref[...], (((1,), (1,)), ((), ())),
                                 preferred_element_type=F32).astype(o_ref.dtype)


def _inproj(h_all, wt, row0, nrows, out_dtype, tn, name):
    tm = 1664
    if row0 % tn == 0:
        w_spec = pl.BlockSpec((tn, D_MODEL), lambda j, i: (j + row0 // tn, 0))
    else:
        assert row0 % 16 == 0
        w_spec = pl.BlockSpec((pl.Element(tn), pl.Element(D_MODEL)),
                              lambda j, i: (pl.multiple_of(row0 + j * tn, 16), 0))
    return pl.pallas_call(
        _inproj_kernel,
        grid=(nrows // tn, T_ALL // tm),
        in_specs=[pl.BlockSpec((tm, D_MODEL), lambda j, i: (i, 0)), w_spec],
        out_specs=pl.BlockSpec((tm, tn), lambda j, i: (i, j)),
        out_shape=jax.ShapeDtypeStruct((T_ALL, nrows), out_dtype),
        scratch_shapes=[pltpu.VMEM((tn, D_MODEL), BF16)],
        compiler_params=_cparams(("arbitrary", "arbitrary")),
        name=name,
    )(h_all, wt)


def _log_decay(ac, wal, bal):
    x = jnp.dot(ac.astype(BF16), wal, preferred_element_type=F32) + bal
    return (jnp.minimum(x, 0.0) - jnp.log1p(jnp.exp(-jnp.abs(x)))) / GATE_TAU


def _mixp_kernel(u_ref, q_ref, k_ref, v_ref, r_ref, ac_ref, ga_ref, gb_ref, wal_ref, bal_ref,
                 wpool_ref, pscale_ref, ghead_ref, m_ref, st_ref,
                 ext_ref, state_ref, b_ref, q32_ref, k32_ref, v32_ref, o_ref):
    R = MIX_ROWS
    h = pl.program_id(1)
    t = pl.program_id(2)

    @pl.when(t == 0)
    def _():
        ext_ref[0:16, :] = jnp.zeros((16, POOL_GROUP_IN), F32)
        state_ref[...] = jnp.zeros_like(state_ref)

    u = u_ref[...]
    ext_ref[16:16 + R, :] = u
    e = ext_ref[...]
    s2 = e + pltpu.roll(e, 1, axis=0)
    s4 = s2 + pltpu.roll(s2, 2, axis=0)
    s8 = s4 + pltpu.roll(s4, 4, axis=0)
    s16 = s8 + pltpu.roll(s8, 8, axis=0)
    s = jnp.where(h == 0, s2, jnp.where(h == 1, s4, jnp.where(h == 2, s8, s16)))[16:, :]
    window = jnp.left_shift(2, h)
    pos = t * R + lax.broadcasted_iota(jnp.int32, (R, 1), 0)
    cnt = jnp.minimum(pos + 1, window).astype(F32)
    p = s / cnt - u
    a_out = jnp.dot(p.astype(BF16), wpool_ref[0], preferred_element_type=F32) * pscale_ref[...]
    ext_ref[0:16, :] = ext_ref[R:R + 16, :]

    C = GLA_CHUNK
    g = _log_decay(ac_ref[...], wal_ref[...], bal_ref[...])
    ri = lax.broadcasted_iota(jnp.int32, (C, C), 0)
    ci = lax.broadcasted_iota(jnp.int32, (C, C), 1)
    causal = ri >= ci
    tri = jnp.where(causal, 1.0, 0.0).astype(BF16)
    g_hi = g.astype(BF16)
    g_lo = (g - g_hi.astype(F32)).astype(BF16)
    for c in range(R // C):
        b_ref[c * C:(c + 1) * C, :] = (
            jnp.dot(tri, g_hi[c * C:(c + 1) * C, :], preferred_element_type=F32)
            + jnp.dot(tri, g_lo[c * C:(c + 1) * C, :], preferred_element_type=F32))
    q32_ref[...] = q_ref[...].astype(F32) * (DK_HEAD ** -0.5)
    k32_ref[...] = k_ref[...].astype(F32)
    v32_ref[...] = v_ref[...].astype(F32)

    nt_dims = (((1,), (1,)), ((), ()))
    tn_dims = (((0,), (0,)), ((), ()))
    sub_iota = lax.broadcasted_iota(jnp.int32, (GLA_SUB, 1), 0)

    for c in range(R // C):
        r0 = c * C
        bc = b_ref[r0:r0 + C, :]
        qc = q32_ref[r0:r0 + C, :]
        kc = k32_ref[r0:r0 + C, :]
        vcb = v32_ref[r0:r0 + C, :].astype(BF16)
        st = state_ref[...]
        b_last = bc[C - 1:C, :]
        e_last = jnp.exp(b_last)
        safe = jnp.min(b_last) >= GLA_SAFE_LOG_DECAY
        qt = (qc * jnp.exp(bc)).astype(BF16)
        o_inter = lax.dot_general(qt, st.astype(BF16), nt_dims, preferred_element_type=F32)

        @pl.when(safe)
        def _(r0=r0, bc=bc, kc=kc, vcb=vcb, st=st, e_last=e_last, qt=qt, o_inter=o_inter):
            kt = kc * jnp.exp(-bc)
            att = lax.dot_general(qt, kt.astype(BF16), nt_dims, preferred_element_type=F32)
            att = jnp.where(causal, att, 0.0).astype(BF16)
            o_ref[r0:r0 + C, :] = o_inter + jnp.dot(att, vcb, preferred_element_type=F32)
            kd = (kt * e_last).astype(BF16)
            state_ref[...] = st * e_last + lax.dot_general(vcb, kd, tn_dims, preferred_element_type=F32)

        @pl.when(jnp.logical_not(safe))
        def _(r0=r0, bc=bc, qc=qc, kc=kc, vcb=vcb, st=st, b_last=b_last, e_last=e_last, o_inter=o_inter):
            o_ref[r0:r0 + C, :] = o_inter
            for i in range(C // GLA_SUB):
                lo = i * GLA_SUB
                bsub = bc[lo:lo + GLA_SUB, :]
                qsub = qc[lo:lo + GLA_SUB, :]
                acc = jnp.zeros((GLA_SUB, DV_HEAD), F32)
                if i > 0:
                    ref_row = bc[lo:lo + 1, :]
                    qi = (qsub * jnp.exp(bsub - ref_row)).astype(BF16)
                    ki = (kc[0:lo, :] * jnp.exp(ref_row - bc[0:lo, :])).astype(BF16)
                    att = lax.dot_general(qi, ki, nt_dims, preferred_element_type=F32)
                    acc = jnp.dot(att.astype(BF16), vcb[0:lo, :], preferred_element_type=F32)

                def diag_body(j, acc, lo=lo, bsub=bsub, qsub=qsub):
                    row = r0 + lo + j
                    bs = b_ref[pl.ds(row, 1), :]
                    ks = k32_ref[pl.ds(row, 1), :]
                    vs = v32_ref[pl.ds(row, 1), :]
                    dec = jnp.exp(jnp.where(sub_iota >= j, bsub - bs, -jnp.inf))
                    col = jnp.sum(qsub * dec * ks, axis=-1, keepdims=True)
                    return acc + col * vs

                acc = lax.fori_loop(0, GLA_SUB, diag_body, acc)
                o_ref[r0 + lo:r0 + lo + GLA_SUB, :] += acc
            kd = (kc * jnp.exp(b_last - bc)).astype(BF16)
            state_ref[...] = st * e_last + lax.dot_general(vcb, kd, tn_dims, preferred_element_type=F32)

    o = _rms(o_ref[...], ghead_ref[...])
    r = r_ref[...].astype(F32)
    b_out = o * (r * _sigmoid(r))
    m = _sigmoid(ga_ref[...].astype(F32)) * a_out + _sigmoid(gb_ref[...].astype(F32)) * b_out
    m_ref[...] = m.astype(BF16)

    @pl.when(t == pl.num_programs(2) - 1)
    def _():
        st_ref[0, 0] = state_ref[...].T


def _mix_prompt(u_all, qkvr, acode, gates, wal, bal, wpool, pscale, ghead):
    R = MIX_ROWS
    nt = SEQ // R

    def rows(b, h, t):
        return b * nt + t

    in_specs = [
        pl.BlockSpec((R, POOL_GROUP_IN), lambda b, h, t: (rows(b, h, t), h)),
        pl.BlockSpec((R, DK_HEAD), lambda b, h, t: (rows(b, h, t), h)),
        pl.BlockSpec((R, DK_HEAD), lambda b, h, t: (rows(b, h, t), GLA_HEADS + h)),
        pl.BlockSpec((R, DV_HEAD), lambda b, h, t: (rows(b, h, t), GLA_HEADS + h)),
        pl.BlockSpec((R, DV_HEAD), lambda b, h, t: (rows(b, h, t), 2 * GLA_HEADS + h)),
        pl.BlockSpec((R, LANES), lambda b, h, t: (rows(b, h, t), 0)),
        pl.BlockSpec((R, DV_HEAD), lambda b, h, t: (rows(b, h, t), h)),
        pl.BlockSpec((R, DV_HEAD), lambda b, h, t: (rows(b, h, t), GLA_HEADS + h)),
        pl.BlockSpec((LANES, DK_HEAD), lambda b, h, t: (0, h)),
        pl.BlockSpec((1, DK_HEAD), lambda b, h, t: (0, h)),
        pl.BlockSpec((1, POOL_GROUP_IN, POOL_GROUP_OUT), lambda b, h, t: (h, 0, 0)),
        pl.BlockSpec((1, POOL_GROUP_OUT), lambda b, h, t: (0, h)),
        pl.BlockSpec((1, DV_HEAD), lambda b, h, t: (0, 0)),
    ]
    out_specs = [
        pl.BlockSpec((R, DV_HEAD), lambda b, h, t: (rows(b, h, t), h)),
        pl.BlockSpec((1, 1, DK_HEAD, DV_HEAD), lambda b, h, t: (b, h, 0, 0)),
    ]
    return pl.pallas_call(
        _mixp_kernel,
        grid=(BATCH, GLA_HEADS, nt),
        in_specs=in_specs,
        out_specs=out_specs,
        out_shape=[jax.ShapeDtypeStruct((T_PROMPT, D_MODEL), BF16),
                   jax.ShapeDtypeStruct((BATCH, GLA_HEADS, DK_HEAD, DV_HEAD), F32)],
        scratch_shapes=[pltpu.VMEM((16 + R, POOL_GROUP_IN), F32),
                        pltpu.VMEM((DV_HEAD, DK_HEAD), F32),
                        pltpu.VMEM((R, DK_HEAD), F32),
                        pltpu.VMEM((R, DK_HEAD), F32),
                        pltpu.VMEM((R, DK_HEAD), F32),
                        pltpu.VMEM((R, DV_HEAD), F32),
                        pltpu.VMEM((R, DV_HEAD), F32)],
        compiler_params=_cparams(("arbitrary", "arbitrary", "arbitrary")),
        name="mix_prompt",
    )(u_all, qkvr, qkvr, qkvr, qkvr, acode, gates, gates, wal, bal, wpool, pscale, ghead)


def _spre_kernel(u_ref, sp_ref, ac_ref, wal_ref, bal_ref, wpool_ref, pscale_ref, a_ref, g_ref):
    u = u_ref[...]
    for gi, w in enumerate(POOL_WINDOWS):
        lo = gi * POOL_GROUP_IN
        s = u[:, lo:lo + POOL_GROUP_IN]
        for j in range(POOL_BUF - (w - 1), POOL_BUF):
            s = s + sp_ref[j, :, lo:lo + POOL_GROUP_IN]
        cnt = float(min(PAST_LEN + 1, w))
        p = s / cnt - u[:, lo:lo + POOL_GROUP_IN]
        a = jnp.dot(p.astype(BF16), wpool_ref[gi], preferred_element_type=F32)
        olo = gi * POOL_GROUP_OUT
        a_ref[:, olo:olo + POOL_GROUP_OUT] = a * pscale_ref[:, olo:olo + POOL_GROUP_OUT]
    g_ref[...] = _log_decay(ac_ref[...], wal_ref[...], bal_ref[...])


def _sample_pre(u_all, sp2, acode, wal, bal, wpool, pscale):
    blk = T_PROMPT // DEC_BATCH
    return pl.pallas_call(
        _spre_kernel,
        grid=(1,),
        in_specs=[pl.BlockSpec((DEC_BATCH, D_POOL), lambda i: (blk, 0)),
                  pl.BlockSpec((POOL_BUF, DEC_BATCH, D_POOL), lambda i: (0, 0, 0)),
                  pl.BlockSpec((DEC_BATCH, LANES), lambda i: (blk, 0)),
                  pl.BlockSpec((LANES, D_K), lambda i: (0, 0)),
                  pl.BlockSpec((1, D_K), lambda i: (0, 0)),
                  pl.BlockSpec((4, POOL_GROUP_IN, POOL_GROUP_OUT), lambda i: (0, 0, 0)),
                  pl.BlockSpec((1, D_MODEL), lambda i: (0, 0))],
        out_specs=[pl.BlockSpec((DEC_BATCH, D_MODEL), lambda i: (0, 0)),
                   pl.BlockSpec((DEC_BATCH, D_K), lambda i: (0, 0))],
        out_shape=[jax.ShapeDtypeStruct((DEC_BATCH, D_MODEL), F32),
                   jax.ShapeDtypeStruct((DEC_BATCH, D_K), F32)],
        compiler_params=_cparams(("arbitrary",)),
        name="sample_pre",
    )(u_all, sp2, acode, wal, bal, wpool, pscale)


SSTATE_SAMPLES = 4


def _sstate_kernel(gkq_ref, v_ref, r_ref, a_ref, gt_ref, s_ref, ghead_ref, so_ref, m_ref):
    for j in range(SSTATE_SAMPLES):
        x = gkq_ref[j]
        rowi = lax.broadcasted_iota(jnp.int32, x.shape, 0)
        x = jnp.where(rowi == 0, jnp.exp(x), jnp.where(rowi == 2, x * (DK_HEAD ** -0.5), x))
        xt = x.T
        v = v_ref[j]
        r = r_ref[j]
        a_out = a_ref[j]
        gates = gt_ref[j]
        for h in range(GLA_HEADS):
            cols = xt[h * DK_HEAD:(h + 1) * DK_HEAD, :]
            dec, kcol, qcol = cols[:, 0:1], cols[:, 1:2], cols[:, 2:3]
            lo = h * DV_HEAD
            vrow = v[:, lo:lo + DV_HEAD]
            s_new = dec * s_ref[j, h] + kcol * vrow
            so_ref[j, h] = s_new
            o = jnp.sum(qcol * s_new, axis=0, keepdims=True)
            o = _rms(o, ghead_ref[...])
            rr = r[:, lo:lo + DV_HEAD]
            b_out = o * (rr * _sigmoid(rr))
            m_ref[j, :, lo:lo + DV_HEAD] = (
                _sigmoid(gates[:, lo:lo + DV_HEAD]) * a_out[:, lo:lo + DV_HEAD]
                + _sigmoid(gates[:, D_MODEL + lo:D_MODEL + lo + DV_HEAD]) * b_out)


def _sample_state(gkq, v3, r3, a3, gates3, state, ghead):
    ns = SSTATE_SAMPLES

    def row3(width):
        return pl.BlockSpec((ns, 1, width), lambda i: (i, 0, 0))

    sspec = pl.BlockSpec((ns, GLA_HEADS, DK_HEAD, DV_HEAD), lambda i: (i, 0, 0, 0))
    return pl.pallas_call(
        _sstate_kernel,
        grid=(DEC_BATCH // ns,),
        in_specs=[pl.BlockSpec((ns, 8, D_K), lambda i: (i, 0, 0)),
                  row3(D_V), row3(D_V), row3(D_MODEL), row3(2 * D_MODEL), sspec,
                  pl.BlockSpec((1, DV_HEAD), lambda i: (0, 0))],
        out_specs=[sspec, row3(D_MODEL)],
        out_shape=[jax.ShapeDtypeStruct((DEC_BATCH, GLA_HEADS, DK_HEAD, DV_HEAD), F32),
                   jax.ShapeDtypeStruct((DEC_BATCH, 1, D_MODEL), F32)],
        compiler_params=_cparams(("arbitrary",)),
        name="sample_state",
    )(gkq, v3, r3, a3, gates3, state, ghead)


def _outproj_kernel(mp_ref, ms_ref, xp_ref, xs_ref, wo_ref, gt1p_ref, sh2p_ref, sc2p_ref,
                    gt1s_ref, sh2s_ref, sc2s_ref, g2_ref, wr_ref, br_ref,
                    x1_ref, h2_ref, ti_ref, tg_ref):
    i = pl.program_id(0)

    def body(nrows, m_ref, x_ref, gt1_ref, sh2_ref, sc2_ref):
        y = jnp.dot(m_ref[...], wo_ref[...], preferred_element_type=F32)
        x1 = x_ref[...] + (1.0 + _modval(gt1_ref)) * y
        x1_ref[0:nrows, :] = x1
        h2 = _rms(x1, g2_ref[...]) * (1.0 + _modval(sc2_ref)) + _modval(sh2_ref)
        h2_ref[0:nrows, :] = h2
        h_hi = h2.astype(BF16)
        h_lo = (h2 - h_hi.astype(F32)).astype(BF16)
        p_hi = jnp.dot(h_hi, wr_ref[...], preferred_element_type=F32)
        p_lo = jnp.dot(h_lo, wr_ref[:, 0:LANES], preferred_element_type=F32)
        logits = p_hi[:, 0:LANES] + (p_hi[:, LANES:] + p_lo) + br_ref[...]
        lane = lax.broadcasted_iota(jnp.int32, logits.shape, 1)
        vals, idxs = [], []
        cur = logits
        for _ in range(TOP_K):
            mx = jnp.max(cur, axis=-1, keepdims=True)
            ix = jnp.min(jnp.where(cur == mx, lane, LANES), axis=-1, keepdims=True)
            vals.append(mx)
            idxs.append(ix)
            cur = jnp.where(lane == ix, -jnp.inf, cur)
        exps = [jnp.exp(v - vals[0]) for v in vals]
        den = exps[0] + exps[1] + exps[2] + exps[3]
        ti = jnp.full(logits.shape, -1, jnp.int32)
        tg = jnp.zeros(logits.shape, F32)
        for k in range(TOP_K):
            ti = jnp.where(lane == k, idxs[k], ti)
            tg = jnp.where(lane == k, exps[k] / den, tg)
        ti_ref[0:nrows, :] = ti
        tg_ref[0:nrows, :] = tg

    @pl.when(i < ROW_TILES_P)
    def _():
        body(ROW_TILE, mp_ref, xp_ref, gt1p_ref, sh2p_ref, sc2p_ref)

    @pl.when(i == ROW_TILES_P)
    def _():
        body(DEC_BATCH, ms_ref, xs_ref, gt1s_ref, sh2s_ref, sc2s_ref)


def _outproj(m_p, m_s, x_p, x_s, wo, mod_p3, mod_s, g2, wr, br):
    return pl.pallas_call(
        _outproj_kernel,
        grid=(ROW_TILES_P + 1,),
        in_specs=[_prompt_rows_spec(D_MODEL), _sample_rows_spec(D_MODEL),
                  _prompt_rows_spec(D_MODEL), _sample_rows_spec(D_MODEL),
                  pl.BlockSpec((D_MODEL, D_MODEL), lambda i: (0, 0)),
                  _mod_spec_prompt(2), _mod_spec_prompt(3), _mod_spec_prompt(4),
                  _mod_spec_sample(2), _mod_spec_sample(3), _mod_spec_sample(4),
                  pl.BlockSpec((1, D_MODEL), lambda i: (0, 0)),
                  pl.BlockSpec((D_MODEL, 2 * LANES), lambda i: (0, 0)),
                  pl.BlockSpec((1, LANES), lambda i: (0, 0))],
        out_specs=[pl.BlockSpec((ROW_TILE, D_MODEL), lambda i: (i, 0)),
                   pl.BlockSpec((ROW_TILE, D_MODEL), lambda i: (i, 0)),
                   pl.BlockSpec((ROW_TILE, LANES), lambda i: (i, 0)),
                   pl.BlockSpec((ROW_TILE, LANES), lambda i: (i, 0))],
        out_shape=[jax.ShapeDtypeStruct((T_ALL, D_MODEL), F32),
                   jax.ShapeDtypeStruct((T_ALL, D_MODEL), F32),
                   jax.ShapeDtypeStruct((T_ALL, LANES), jnp.int32),
                   jax.ShapeDtypeStruct((T_ALL, LANES), F32)],
        compiler_params=_cparams(("arbitrary",)),
        name="outproj",
    )(m_p, m_s, x_p, x_s, wo, mod_p3, mod_p3, mod_p3, mod_s, mod_s, mod_s, g2, wr, br)


ROUTE_TILE = 640
ROUTE_TILES = T_ALL // ROUTE_TILE


def _route_kernel(ti_ref, dest_ref, cnt_ref, bstart_ref, run_ref, base_ref):
    phase = pl.program_id(0)
    t = pl.program_id(1)
    ti = ti_ref[...]
    lane = lax.broadcasted_iota(jnp.int32, (ROUTE_TILE, LANES), 1)
    onehots = [lane == ti[:, k:k + 1] for k in range(TOP_K)]
    per_tok = jnp.zeros((ROUTE_TILE, LANES), F32)
    for oh in onehots:
        per_tok = per_tok + jnp.where(oh, 1.0, 0.0)

    @pl.when((phase == 0) & (t == 0))
    def _():
        run_ref[...] = jnp.zeros_like(run_ref)

    @pl.when(phase == 0)
    def _():
        run_ref[...] += jnp.sum(per_tok, axis=0, keepdims=True)

    @pl.when((phase == 1) & (t == 0))
    def _():
        counts = run_ref[...]
        nblk = jnp.floor((counts + (MOE_SB - 1)) * (1.0 / MOE_SB))
        ri = lax.broadcasted_iota(jnp.int32, (LANES, LANES), 0)
        ci = lax.broadcasted_iota(jnp.int32, (LANES, LANES), 1)
        before = jnp.where(ri < ci, 1.0, 0.0).astype(BF16)
        nb8 = jnp.broadcast_to(nblk, (8, LANES)).astype(BF16)
        bstart = jnp.dot(nb8, before, preferred_element_type=F32)[0:1, :]
        cnt_ref[...] = counts.astype(jnp.int32)
        bstart_ref[...] = bstart.astype(jnp.int32)
        base_ref[...] = bstart * MOE_SB
        run_ref[...] = jnp.zeros_like(run_ref)

    @pl.when(phase == 1)
    def _():
        ri = lax.broadcasted_iota(jnp.int32, (ROUTE_TILE, ROUTE_TILE), 0)
        ci = lax.broadcasted_iota(jnp.int32, (ROUTE_TILE, ROUTE_TILE), 1)
        earlier = jnp.where(ri > ci, 1.0, 0.0).astype(BF16)
        prior = jnp.dot(earlier, per_tok.astype(BF16), preferred_element_type=F32)
        pos = prior + (run_ref[...] + base_ref[...])
        dest = jnp.full((ROUTE_TILE, LANES), -1, jnp.int32)
        for k, oh in enumerate(onehots):
            d = jnp.sum(jnp.where(oh, pos, 0.0), axis=-1, keepdims=True).astype(jnp.int32)
            dest = jnp.where(lane == k, d, dest)
        dest_ref[...] = dest
        run_ref[...] += jnp.sum(per_tok, axis=0, keepdims=True)


def _route(ti_all):
    return pl.pallas_call(
        _route_kernel,
        grid=(2, ROUTE_TILES),
        in_specs=[pl.BlockSpec((ROUTE_TILE, LANES), lambda p, t: (t, 0))],
        out_specs=[pl.BlockSpec((ROUTE_TILE, LANES), lambda p, t: (t * p, 0)),
                   pl.BlockSpec((1, LANES), lambda p, t: (0, 0)),
                   pl.BlockSpec((1, LANES), lambda p, t: (0, 0))],
        out_shape=[jax.ShapeDtypeStruct((T_ALL, LANES), jnp.int32),
                   jax.ShapeDtypeStruct((1, LANES), jnp.int32),
                   jax.ShapeDtypeStruct((1, LANES), jnp.int32)],
        scratch_shapes=[pltpu.VMEM((1, LANES), F32), pltpu.VMEM((1, LANES), F32)],
        compiler_params=_cparams(("arbitrary", "arbitrary")),
        name="route",
    )(ti_all)


GATHER_UNROLL = 8
DISP_TOK = 640
DISP_ROWS = DISP_TOK * TOP_K


def _dispatch_kernel(dest_ref, cnt_ref, bstart_ref, nused_ref, h2_ref, xb_hbm, zero_ref, sem, zsem):
    i = pl.program_id(0)

    def body(g, c):
        for u in range(GATHER_UNROLL):
            row = dest_ref[i * DISP_ROWS + g * GATHER_UNROLL + u]
            tok = g * (GATHER_UNROLL // TOP_K) + u // TOP_K
            pltpu.make_async_copy(h2_ref.at[pl.ds(tok, 1)], xb_hbm.at[pl.ds(row, 1)], sem).start(priority=u % 2)
        return c
    lax.fori_loop(0, DISP_ROWS // GATHER_UNROLL, body, 0)

    def pad_range(e):
        nblk = (cnt_ref[e] + MOE_SB - 1) // MOE_SB
        return bstart_ref[e] * MOE_SB + cnt_ref[e], (bstart_ref[e] + nblk) * MOE_SB

    def zero_row(row):
        return pltpu.make_async_copy(zero_ref.at[pl.ds(0, 1)], xb_hbm.at[pl.ds(row, 1)], zsem)

    def zero_block(blk):
        return pltpu.make_async_copy(zero_ref, xb_hbm.at[pl.ds(blk * MOE_SB, MOE_SB)], zsem)

    def for_each_unowned(row_fn, block_fn):
        def per_expert(e, c):
            lo, hi = pad_range(e)
            lax.fori_loop(lo, hi, lambda r, cc: (row_fn(r), cc)[1], 0)
            return c
        lax.fori_loop(0, N_EXPERTS, per_expert, 0)
        lax.fori_loop(nused_ref[0], MOE_NBLK, lambda b, cc: (block_fn(b), cc)[1], 0)

    @pl.when(i == 0)
    def _():
        zero_ref[...] = jnp.zeros_like(zero_ref)
        for_each_unowned(lambda r: zero_row(r).start(), lambda b: zero_block(b).start())

    for _ in range(TOP_K):
        pltpu.make_async_copy(h2_ref, xb_hbm.at[pl.ds(0, DISP_TOK)], sem).wait()

    @pl.when(i == 0)
    def _():
        for_each_unowned(lambda r: zero_row(r).wait(), lambda b: zero_block(b).wait())


def _dispatch(dest, counts, blk_start, nused, h2_all):
    return pl.pallas_call(
        _dispatch_kernel,
        grid_spec=pltpu.PrefetchScalarGridSpec(
            num_scalar_prefetch=4,
            grid=(T_ALL // DISP_TOK,),
            in_specs=[pl.BlockSpec((DISP_TOK, D_MODEL), lambda i, d, c, b, n: (i, 0))],
            out_specs=pl.BlockSpec(memory_space=pl.ANY),
            scratch_shapes=[pltpu.VMEM((MOE_SB, D_MODEL), F32),
                            pltpu.SemaphoreType.DMA(()),
                            pltpu.SemaphoreType.DMA(())]),
        out_shape=jax.ShapeDtypeStruct((MOE_ROWS, D_MODEL), F32),
        compiler_params=_cparams(("arbitrary",)),
        name="dispatch",
    )(dest, counts, blk_start, nused, h2_all)


def _moe_kernel(ie_ref, ib0_ref, inb_ref, ival_ref, nused_ref,
                xb_hbm, wg_ref, wu_ref, wd_ref, bg_ref, bu_ref, bd_ref, yb_hbm,
                xbuf, xstage, acc, wgb, wub, wdb, xsem, ysem):
    i = pl.program_id(0)
    f = pl.program_id(1)
    n_items = pl.num_programs(0)
    nb = inb_ref[i]

    def x_copy(blk, s):
        row0 = (ib0_ref[i] + blk) * MOE_SB
        return pltpu.make_async_copy(xb_hbm.at[pl.ds(row0, MOE_SB)], xstage.at[s], xsem.at[s])

    def y_copy(gblk, r0):
        return pltpu.make_async_copy(acc.at[pl.ds(r0, MOE_SB)], yb_hbm.at[pl.ds(gblk * MOE_SB, MOE_SB)], ysem)

    def wait_y_copies(count):
        def body(b, c):
            y_copy(0, 0).wait()
            return c
        lax.fori_loop(0, count, body, 0)

    @pl.when((f == 0) & (nb > 0))
    def _():
        x_copy(0, 0).start()

    @pl.when((f == 0) & (i > 0))
    def _():
        wait_y_copies(inb_ref[jnp.maximum(i - 1, 0)])

    bg = bg_ref[0]
    bu = bu_ref[0]
    bd = bd_ref[0]

    def run_blocks(first, final):
        def body(blk, c):
            r0 = blk * MOE_SB if isinstance(blk, int) else pl.multiple_of(blk * MOE_SB, MOE_SB)
            if first:
                s = blk % 2
                x_copy(blk, s).wait()
                x_copy(jnp.minimum(blk + 1, nb - 1), 1 - s).start()
                x = xstage[s].astype(BF16)
                xbuf[pl.ds(r0, MOE_SB), :] = x
            else:
                x = xbuf[pl.ds(r0, MOE_SB), :]
            g = jnp.dot(x, wgb[...], preferred_element_type=F32) + bg
            u = jnp.dot(x, wub[...], preferred_element_type=F32) + bu
            xg = jnp.minimum(g, SWIGLU_LIMIT)
            xl = jnp.clip(u, -SWIGLU_LIMIT, SWIGLU_LIMIT)
            act = xg * _sigmoid(SWIGLU_ALPHA * xg) * (xl + 1.0)
            part = jnp.dot(act.astype(BF16), wdb[...], preferred_element_type=F32)
            if first:
                acc[pl.ds(r0, MOE_SB), :] = part + bd
            else:
                acc[pl.ds(r0, MOE_SB), :] += part
            if final:
                y_copy(ib0_ref[i] + blk, r0).start()
            return c

        wgb[...] = wg_ref[0].astype(BF16)
        wub[...] = wu_ref[0].astype(BF16)
        wdb[...] = wd_ref[0].astype(BF16)
        body(0, 0)
        lax.fori_loop(1, nb, body, 0)

    @pl.when((f == 0) & (nb > 0))
    def _():
        run_blocks(True, False)
        x_copy(nb - 1, nb % 2).wait()

    @pl.when((f > 0) & (f < MOE_NF - 1) & (nb > 0))
    def _():
        run_blocks(False, False)

    @pl.when((f == MOE_NF - 1) & (nb > 0))
    def _():
        run_blocks(False, True)

    @pl.when((f == MOE_NF - 1) & (i == n_items - 1))
    def _():
        wait_y_copies(nb)
        acc[0:MOE_SB, :] = jnp.zeros((MOE_SB, D_MODEL), F32)

        def zstart(gblk, c):
            y_copy(gblk, 0).start()
            return c
        lax.fori_loop(nused_ref[0], MOE_NBLK, zstart, 0)
        wait_y_copies(MOE_NBLK - nused_ref[0])


def _moe(item_e, item_b0, item_nb, item_valid, nused, xb, w_gu, b_gu3, w_down, b_down3):
    def f_eff(f, ival, i):
        return jnp.where(ival[i] > 0, f, MOE_NF - 1)

    in_specs = [
        pl.BlockSpec(memory_space=pl.ANY),
        pl.BlockSpec((1, D_MODEL, MOE_TF), lambda i, f, ie, ib0, inb, iv, nu: (ie[i], 0, f_eff(f, iv, i))),
        pl.BlockSpec((1, D_MODEL, MOE_TF),
                     lambda i, f, ie, ib0, inb, iv, nu: (ie[i], 0, MOE_NF + f_eff(f, iv, i))),
        pl.BlockSpec((1, MOE_TF, D_MODEL), lambda i, f, ie, ib0, inb, iv, nu: (ie[i], f_eff(f, iv, i), 0)),
        pl.BlockSpec((1, 1, MOE_TF), lambda i, f, ie, ib0, inb, iv, nu: (ie[i], 0, f_eff(f, iv, i))),
        pl.BlockSpec((1, 1, MOE_TF),
                     lambda i, f, ie, ib0, inb, iv, nu: (ie[i], 0, MOE_NF + f_eff(f, iv, i))),
        pl.BlockSpec((1, 1, D_MODEL), lambda i, f, ie, ib0, inb, iv, nu: (ie[i], 0, 0)),
    ]
    return pl.pallas_call(
        _moe_kernel,
        grid_spec=pltpu.PrefetchScalarGridSpec(
            num_scalar_prefetch=5,
            grid=(MOE_NITEMS, MOE_NF),
            in_specs=in_specs,
            out_specs=pl.BlockSpec(memory_space=pl.ANY),
            scratch_shapes=[pltpu.VMEM((MOE_MC, D_MODEL), BF16),
                            pltpu.VMEM((2, MOE_SB, D_MODEL), F32),
                            pltpu.VMEM((MOE_MC, D_MODEL), F32),
                            pltpu.VMEM((D_MODEL, MOE_TF), BF16),
                            pltpu.VMEM((D_MODEL, MOE_TF), BF16),
                            pltpu.VMEM((MOE_TF, D_MODEL), BF16),
                            pltpu.SemaphoreType.DMA((2,)),
                            pltpu.SemaphoreType.DMA(())]),
        out_shape=jax.ShapeDtypeStruct((MOE_ROWS, D_MODEL), F32),
        compiler_params=_cparams(("arbitrary", "arbitrary")),
        name="moe",
    )(item_e, item_b0, item_nb, item_valid, nused, xb, w_gu, w_gu, w_down, b_gu3, b_gu3, b_down3)


FIN_TOK = 128
FIN_ROWS = FIN_TOK * TOP_K


FIN_TILES_P = T_PROMPT // FIN_TOK
GATHER_RING = 3


def _final_kernel(dest_ref, yb_hbm, x1_ref, tg_ref, gt2p_ref, gt2s_ref, gf_ref, yp_ref, ys_ref, stage_ref, sem):
    i = pl.program_id(0)
    n = pl.num_programs(0)

    def row_copy(tile, slot, j):
        return pltpu.make_async_copy(yb_hbm.at[pl.ds(dest_ref[tile * FIN_ROWS + j], 1)],
                                     stage_ref.at[slot, pl.ds((j % TOP_K) * FIN_TOK + j // TOP_K, 1)],
                                     sem.at[slot])

    @pl.when(i == 0)
    def _():
        for b in range(GATHER_RING - 1):
            def body(g, c, b=b):
                for u in range(GATHER_UNROLL):
                    row_copy(b, b, g * GATHER_UNROLL + u).start(priority=u % 2)
                return c
            lax.fori_loop(0, FIN_ROWS // GATHER_UNROLL, body, 0)

    def step(issue_ahead, gt2_ref, y_ref):
        slot = lax.rem(i, GATHER_RING)
        pltpu.make_async_copy(yb_hbm.at[pl.ds(0, FIN_ROWS)], stage_ref.at[slot], sem.at[slot]).wait()
        if issue_ahead:
            tile = i + GATHER_RING - 1
            nslot = lax.rem(tile, GATHER_RING)
            for j in range(FIN_ROWS):
                row_copy(tile, nslot, j).start(priority=j % 2)
        tg = tg_ref[...]
        f = jnp.zeros((FIN_TOK, D_MODEL), F32)
        for k in range(TOP_K):
            f = f + tg[:, k:k + 1] * stage_ref[slot, k * FIN_TOK:(k + 1) * FIN_TOK, :]
        y_ref[...] = _rms(x1_ref[...] + (1.0 + _modval(gt2_ref)) * f, gf_ref[...])

    @pl.when(i + GATHER_RING - 1 < n)
    def _():
        step(True, gt2p_ref, yp_ref)

    @pl.when((i + GATHER_RING - 1 >= n) & (i < FIN_TILES_P))
    def _():
        step(False, gt2p_ref, yp_ref)

    @pl.when(i == FIN_TILES_P)
    def _():
        step(False, gt2s_ref, ys_ref)


def _final(dest, yb, x1_all, tg_all, mod_p3, mod_s, gf):
    def ptile(i):
        return jnp.minimum(i, FIN_TILES_P - 1)

    return pl.pallas_call(
        _final_kernel,
        grid_spec=pltpu.PrefetchScalarGridSpec(
            num_scalar_prefetch=1,
            grid=(FIN_TILES_P + 1,),
            in_specs=[pl.BlockSpec(memory_space=pl.ANY),
                      pl.BlockSpec((FIN_TOK, D_MODEL), lambda i, d: (i, 0)),
                      pl.BlockSpec((FIN_TOK, LANES), lambda i, d: (i, 0)),
                      pl.BlockSpec((1, 1, D_MODEL), lambda i, d: (ptile(i) // (SEQ // FIN_TOK), 0, 5)),
                      pl.BlockSpec((DEC_BATCH, D_MODEL), lambda i, d: (0, 5)),
                      pl.BlockSpec((1, D_MODEL), lambda i, d: (0, 0))],
            out_specs=[pl.BlockSpec((FIN_TOK, D_MODEL), lambda i, d: (ptile(i), 0)),
                       pl.BlockSpec((DEC_BATCH, D_MODEL), lambda i, d: (0, 0))],
            scratch_shapes=[pltpu.VMEM((GATHER_RING, FIN_ROWS, D_MODEL), F32),
                            pltpu.SemaphoreType.DMA((GATHER_RING,))]),
        out_shape=[jax.ShapeDtypeStruct((T_PROMPT, D_MODEL), F32),
                   jax.ShapeDtypeStruct((DEC_BATCH, D_MODEL), F32)],
        compiler_params=_cparams(("arbitrary",)),
        name="final",
    )(dest, yb, x1_all, tg_all, mod_p3, mod_s, gf)


def _work_items(counts, blk_start):
    nblk = (counts + MOE_SB - 1) // MOE_SB
    n_items_e = (nblk + MOE_ITEM_BLOCKS - 1) // MOE_ITEM_BLOCKS
    item_end = jnp.cumsum(n_items_e)
    item_start = item_end - n_items_e
    total_items = item_end[-1]
    it = jnp.arange(MOE_NITEMS, dtype=jnp.int32)
    it_c = jnp.minimum(it, total_items - 1)
    item_e = jnp.minimum(jnp.sum((item_end[None, :] <= it_c[:, None]).astype(jnp.int32), axis=1),
                         N_EXPERTS - 1).astype(jnp.int32)
    jj = it_c - item_start[item_e]
    item_valid = (it < total_items).astype(jnp.int32)
    item_b0 = (blk_start[item_e] + jj * MOE_ITEM_BLOCKS).astype(jnp.int32)
    item_nb = jnp.where(item_valid > 0,
                        jnp.minimum(MOE_ITEM_BLOCKS, nblk[item_e] - jj * MOE_ITEM_BLOCKS), 0).astype(jnp.int32)
    nused = (blk_start[-1] + nblk[-1]).reshape(1).astype(jnp.int32)
    return item_e, item_b0, item_nb, item_valid, nused


def kernel(x_prompt, x_sample, c_prompt, c_sample, state_pool, state_gla, w_mod, b_mod, g_norm1, w_in,
           w_alpha, b_alpha, w_pool, pool_scale, g_head, w_out, g_norm2, w_router, b_router, w_gu, b_gu,
           w_down, b_down, g_final):
    x_p = x_prompt.reshape(T_PROMPT, D_MODEL)
    x_s = x_sample.reshape(DEC_BATCH, D_MODEL)

    c_all = jnp.concatenate([c_prompt, c_sample, jnp.zeros((4, D_MODEL), F32)], axis=0)
    mod = _mod(c_all, w_mod[0], b_mod[0])
    mod_p3 = mod[:BATCH].reshape(BATCH, 1, 6 * D_MODEL)
    mod_s = mod[BATCH:BATCH + DEC_BATCH]

    h_all = _prenorm(x_p, x_s, g_norm1, mod_p3, mod_s)

    wt = jnp.swapaxes(w_in, 1, 2)[0]
    w_a = jnp.pad(wt[COL_A:COL_G], ((0, LANES - ALPHA_RANK), (0, 0)))
    u_all = _inproj(h_all, wt, COL_U, D_POOL, F32, 512, "inproj_u")
    qkvr = _inproj(h_all, wt, COL_Q, COL_A - COL_Q, BF16, 1024, "inproj_qkvr")
    acode = _inproj(h_all, w_a, 0, LANES, F32, LANES, "inproj_a")
    gates = _inproj(h_all, wt, COL_G, 2 * D_MODEL, BF16, 1024, "inproj_g")

    wal = jnp.pad(w_alpha[0], ((0, LANES - ALPHA_RANK), (0, 0))).astype(BF16)
    bal = b_alpha[0].reshape(1, D_K)
    wpool = w_pool[0].astype(BF16)
    ghead = g_head[0].reshape(1, DV_HEAD)

    m_p, new_gla_p = _mix_prompt(u_all, qkvr, acode, gates, wal, bal, wpool, pool_scale, ghead)

    sp2 = jnp.transpose(state_pool[0], (1, 0, 2))
    a_s, g_s = _sample_pre(u_all, sp2, acode, wal, bal, wpool, pool_scale)
    zs = qkvr[T_PROMPT:].astype(F32)
    q_s, k_s = zs[:, 0:D_K], zs[:, D_K:2 * D_K]
    gkq = jnp.stack([g_s, k_s, q_s] + [jnp.zeros_like(g_s)] * 5, axis=1)
    v3 = zs[:, 2 * D_K:2 * D_K + D_V].reshape(DEC_BATCH, 1, D_V)
    r3 = zs[:, 2 * D_K + D_V:].reshape(DEC_BATCH, 1, D_V)
    gates3 = gates[T_PROMPT:].astype(F32).reshape(DEC_BATCH, 1, 2 * D_MODEL)
    new_gla_s, m_s3 = _sample_state(gkq, v3, r3, a_s.reshape(DEC_BATCH, 1, D_MODEL), gates3,
                                    state_gla[0], ghead)
    m_s = m_s3.reshape(DEC_BATCH, D_MODEL).astype(BF16)

    wo = w_out[0].astype(BF16)
    wr32 = jnp.pad(w_router[0], ((0, 0), (0, LANES - N_EXPERTS)))
    wr_hi = wr32.astype(BF16)
    wr = jnp.concatenate([wr_hi, (wr32 - wr_hi.astype(F32)).astype(BF16)], axis=1)
    br = jnp.concatenate([b_router[0], jnp.full((LANES - N_EXPERTS,), -1e30, F32)]).reshape(1, LANES)
    x1_all, h2_all, ti_all, tg_all = _outproj(m_p, m_s, x_p, x_s, wo, mod_p3, mod_s, g_norm2, wr, br)

    dest_t, counts_t, bstart_t = _route(ti_all)
    dest = dest_t[:, :TOP_K].reshape(-1)
    counts, blk_start = counts_t[0, :N_EXPERTS], bstart_t[0, :N_EXPERTS]
    item_e, item_b0, item_nb, item_valid, nused = _work_items(counts, blk_start)

    xb = _dispatch(dest, counts, blk_start, nused, h2_all)
    yb = _moe(item_e, item_b0, item_nb, item_valid, nused, xb, w_gu[0],
              b_gu[0].reshape(N_EXPERTS, 1, 2 * D_FF), w_down[0], b_down[0].reshape(N_EXPERTS, 1, D_MODEL))

    y_p, y_s = _final(dest, yb, x1_all, tg_all, mod_p3, mod_s, g_final.reshape(1, D_MODEL))

    u_p = u_all[:T_PROMPT].reshape(BATCH, SEQ, D_POOL)
    new_pool_p = u_p[:, SEQ - POOL_BUF:, :][None]
    new_pool_s = jnp.concatenate([state_pool[0][:, 1:, :], u_all[T_PROMPT:][:, None, :]], axis=1)[None]
    return (y_p.reshape(BATCH, SEQ, D_MODEL), y_s.reshape(DEC_BATCH, 1, D_MODEL),
            new_pool_p, new_gla_p[None], new_pool_s, new_gla_s[None])
```

```python
import functools

import jax
import jax.numpy as jnp
from jax import lax
from jax.experimental import pallas as pl
from jax.experimental.pallas import tpu as pltpu

F32 = jnp.float32
BF16 = jnp.bfloat16

D_MODEL = 2048
BATCH = 4
SEQ = 2048
DEC_BATCH = 128
PAST_LEN = 16384
POOL_WINDOWS = (2, 4, 8, 16)
D_POOL = D_MODEL // 2
POOL_GROUP_IN = D_POOL // 4
POOL_GROUP_OUT = D_MODEL // 4
POOL_BUF = 15
GLA_HEADS = 4
D_K = D_MODEL // 2
D_V = D_MODEL
DK_HEAD = D_K // GLA_HEADS
DV_HEAD = D_V // GLA_HEADS
ALPHA_RANK = 16
GATE_TAU = 16.0
N_EXPERTS = 32
TOP_K = 4
D_FF = D_MODEL
SWIGLU_LIMIT = 7.0
SWIGLU_ALPHA = 1.702
EPS = 1e-6

LANES = 128
T_PROMPT = BATCH * SEQ
T_ALL = T_PROMPT + DEC_BATCH
N_ASSIGN = T_ALL * TOP_K

COL_U, COL_Q, COL_K, COL_V, COL_R = 0, 1024, 2048, 3072, 5120
COL_A = 7168
COL_G = COL_A + ALPHA_RANK

GLA_CHUNK = 128
GLA_SAFE_LOG_DECAY = -60.0
GLA_SUB = 16
MIX_ROWS = 512

MOE_SB = 256
MOE_ITEM_BLOCKS = 5
MOE_MC = MOE_SB * MOE_ITEM_BLOCKS
MOE_TF = 512
MOE_NF = D_FF // MOE_TF
MOE_NBLK = -(-(N_ASSIGN + N_EXPERTS * (MOE_SB - 1)) // MOE_SB)
MOE_ROWS = MOE_NBLK * MOE_SB
MOE_NITEMS = (MOE_NBLK + N_EXPERTS * (MOE_ITEM_BLOCKS - 1)) // MOE_ITEM_BLOCKS

VMEM_LIMIT = 56 * 1024 * 1024


def _cparams(sem, vmem=VMEM_LIMIT):
    return pltpu.CompilerParams(dimension_semantics=sem, vmem_limit_bytes=vmem)


def _rms(x, g):
    return x * lax.rsqrt(jnp.mean(x * x, axis=-1, keepdims=True) + EPS) * g


def _sigmoid(x):
    return 1.0 / (1.0 + jnp.exp(-x))


def _mod_kernel(c_ref, w_ref, b_ref, o_ref):
    c = c_ref[...]
    s = (c * _sigmoid(c)).astype(BF16)
    o_ref[...] = jnp.dot(s, w_ref[...].astype(BF16), preferred_element_type=F32) + b_ref[...]


def _mod(c_all, w_mod, b_mod):
    m = c_all.shape[0]
    tn = 1024
    return pl.pallas_call(
        _mod_kernel,
        grid=(6 * D_MODEL // tn,),
        in_specs=[pl.BlockSpec((m, D_MODEL), lambda j: (0, 0)),
                  pl.BlockSpec((D_MODEL, tn), lambda j: (0, j)),
                  pl.BlockSpec((1, tn), lambda j: (0, j))],
        out_specs=pl.BlockSpec((m, tn), lambda j: (0, j)),
        out_shape=jax.ShapeDtypeStruct((m, 6 * D_MODEL), F32),
        compiler_params=_cparams(("arbitrary",)),
        name="mod",
    )(c_all, w_mod, b_mod.reshape(1, -1))


ROW_TILE = 512
ROW_TILES_P = T_PROMPT // ROW_TILE
_TILES_PER_BATCH = SEQ // ROW_TILE


def _prompt_tile(i):
    return jnp.minimum(i, ROW_TILES_P - 1)


def _prompt_rows_spec(width):
    return pl.BlockSpec((ROW_TILE, width), lambda i: (_prompt_tile(i), 0))


def _sample_rows_spec(width):
    return pl.BlockSpec((DEC_BATCH, width), lambda i: (0, 0))


def _mod_spec_prompt(chunk):
    return pl.BlockSpec((1, 1, D_MODEL), lambda i: (_prompt_tile(i) // _TILES_PER_BATCH, 0, chunk))


def _mod_spec_sample(chunk):
    return pl.BlockSpec((DEC_BATCH, D_MODEL), lambda i: (0, chunk))


def _modval(ref):
    v = ref[...]
    return v.reshape(v.shape[-2], v.shape[-1])


def _prenorm_kernel(xp_ref, xs_ref, g_ref, shp_ref, scp_ref, shs_ref, scs_ref, o_ref):
    i = pl.program_id(0)

    def body(x_ref, sh_ref, sc_ref):
        y = _rms(x_ref[...], g_ref[...])
        return (y * (1.0 + _modval(sc_ref)) + _modval(sh_ref)).astype(BF16)

    @pl.when(i < ROW_TILES_P)
    def _():
        o_ref[...] = body(xp_ref, shp_ref, scp_ref)

    @pl.when(i == ROW_TILES_P)
    def _():
        o_ref[0:DEC_BATCH, :] = body(xs_ref, shs_ref, scs_ref)


def _prenorm(x_p, x_s, g, mod_p3, mod_s):
    return pl.pallas_call(
        _prenorm_kernel,
        grid=(ROW_TILES_P + 1,),
        in_specs=[_prompt_rows_spec(D_MODEL), _sample_rows_spec(D_MODEL),
                  pl.BlockSpec((1, D_MODEL), lambda i: (0, 0)),
                  _mod_spec_prompt(0), _mod_spec_prompt(1), _mod_spec_sample(0), _mod_spec_sample(1)],
        out_specs=pl.BlockSpec((ROW_TILE, D_MODEL), lambda i: (i, 0)),
        out_shape=jax.ShapeDtypeStruct((T_ALL, D_MODEL), BF16),
        compiler_params=_cparams(("arbitrary",)),
        name="prenorm",
    )(x_p, x_s, g, mod_p3, mod_p3, mod_s, mod_s)


def _inproj_kernel(h_ref, wt_ref, o_ref, wb_ref):
    @pl.when(pl.program_id(1) == 0)
    def _():
        wb_ref[...] = wt_ref[...].astype(BF16)

    o_ref[...] = lax.dot_general(h_ref[...], wb_ref[...], (((1,), (1,)), ((), ())),
                                 preferred_element_type=F32).astype(o_ref.dtype)


def _inproj(h_all, wt, row0, nrows, out_dtype, tn, name):
    tm = 1664
    if row0 % tn == 0:
        w_spec = pl.BlockSpec((tn, D_MODEL), lambda j, i: (j + row0 // tn, 0))
    else:
        assert row0 % 16 == 0
        w_spec = pl.BlockSpec((pl.Element(tn), pl.Element(D_MODEL)),
                              lambda j, i: (pl.multiple_of(row0 + j * tn, 16), 0))
    return pl.pallas_call(
        _inproj_kernel,
        grid=(nrows // tn, T_ALL // tm),
        in_specs=[pl.BlockSpec((tm, D_MODEL), lambda j, i: (i, 0)), w_spec],
        out_specs=pl.BlockSpec((tm, tn), lambda j, i: (i, j)),
        out_shape=jax.ShapeDtypeStruct((T_ALL, nrows), out_dtype),
        scratch_shapes=[pltpu.VMEM((tn, D_MODEL), BF16)],
        compiler_params=_cparams(("arbitrary", "arbitrary")),
        name=name,
    )(h_all, wt)


def _log_decay(ac, wal, bal):
    x = jnp.dot(ac.astype(BF16), wal, preferred_element_type=F32) + bal
    return (jnp.minimum(x, 0.0) - jnp.log1p(jnp.exp(-jnp.abs(x)))) / GATE_TAU


def _mixp_kernel(u_ref, q_ref, k_ref, v_ref, r_ref, ac_ref, ga_ref, gb_ref, wal_ref, bal_ref,
                 wpool_ref, pscale_ref, ghead_ref, m_ref, st_ref,
                 ext_ref, state_ref, b_ref, k32_ref, v32_ref, o_ref):
    R = MIX_ROWS
    h = pl.program_id(1)
    t = pl.program_id(2)

    @pl.when(t == 0)
    def _():
        ext_ref[0:16, :] = jnp.zeros((16, POOL_GROUP_IN), F32)
        state_ref[...] = jnp.zeros_like(state_ref)

    u = u_ref[...]
    ext_ref[16:16 + R, :] = u
    e = ext_ref[...]
    s2 = e + pltpu.roll(e, 1, axis=0)
    s4 = s2 + pltpu.roll(s2, 2, axis=0)
    s8 = s4 + pltpu.roll(s4, 4, axis=0)
    s16 = s8 + pltpu.roll(s8, 8, axis=0)
    s = jnp.where(h == 0, s2, jnp.where(h == 1, s4, jnp.where(h == 2, s8, s16)))[16:, :]
    window = jnp.left_shift(2, h)
    pos = t * R + lax.broadcasted_iota(jnp.int32, (R, 1), 0)
    cnt = jnp.minimum(pos + 1, window).astype(F32)
    p = s / cnt - u
    a_out = jnp.dot(p.astype(BF16), wpool_ref[0], preferred_element_type=F32) * pscale_ref[...]
    ext_ref[0:16, :] = ext_ref[R:R + 16, :]

    C = GLA_CHUNK
    g = _log_decay(ac_ref[...], wal_ref[...], bal_ref[...])
    ri = lax.broadcasted_iota(jnp.int32, (C, C), 0)
    ci = lax.broadcasted_iota(jnp.int32, (C, C), 1)
    causal = ri >= ci
    tri = jnp.where(causal, 1.0, 0.0).astype(BF16)
    g_hi = g.astype(BF16)
    g_lo = (g - g_hi.astype(F32)).astype(BF16)
    for c in range(R // C):
        b_ref[c * C:(c + 1) * C, :] = (
            jnp.dot(tri, g_hi[c * C:(c + 1) * C, :], preferred_element_type=F32)
            + jnp.dot(tri, g_lo[c * C:(c + 1) * C, :], preferred_element_type=F32))

    nt_dims = (((1,), (1,)), ((), ()))
    tn_dims = (((0,), (0,)), ((), ()))
    sub_iota = lax.broadcasted_iota(jnp.int32, (GLA_SUB, 1), 0)

    for c in range(R // C):
        r0 = c * C
        bc = b_ref[r0:r0 + C, :]
        qc = q_ref[r0:r0 + C, :].astype(F32) * (DK_HEAD ** -0.5)
        kc = k_ref[r0:r0 + C, :].astype(F32)
        vcb = v_ref[r0:r0 + C, :]
        st = state_ref[...]
        b_last = bc[C - 1:C, :]
        e_last = jnp.exp(b_last)
        safe = jnp.min(b_last) >= GLA_SAFE_LOG_DECAY
        qt = (qc * jnp.exp(bc)).astype(BF16)
        o_inter = lax.dot_general(qt, st.astype(BF16), nt_dims, preferred_element_type=F32)

        @pl.when(safe)
        def _(r0=r0, bc=bc, kc=kc, vcb=vcb, st=st, e_last=e_last, qt=qt, o_inter=o_inter):
            kt = kc * jnp.exp(-bc)
            att = lax.dot_general(qt, kt.astype(BF16), nt_dims, preferred_element_type=F32)
            att = jnp.where(causal, att, 0.0).astype(BF16)
            o_ref[r0:r0 + C, :] = o_inter + jnp.dot(att, vcb, preferred_element_type=F32)
            kd = (kt * e_last).astype(BF16)
            state_ref[...] = st * e_last + lax.dot_general(vcb, kd, tn_dims, preferred_element_type=F32)

        @pl.when(jnp.logical_not(safe))
        def _(r0=r0, bc=bc, qc=qc, kc=kc, vcb=vcb, st=st, b_last=b_last, e_last=e_last, o_inter=o_inter):
            o_ref[r0:r0 + C, :] = o_inter
            k32_ref[r0:r0 + C, :] = kc
            v32_ref[r0:r0 + C, :] = vcb.astype(F32)
            for i in range(C // GLA_SUB):
                lo = i * GLA_SUB
                bsub = bc[lo:lo + GLA_SUB, :]
                qsub = qc[lo:lo + GLA_SUB, :]
                acc = jnp.zeros((GLA_SUB, DV_HEAD), F32)
                if i > 0:
                    ref_row = bc[lo:lo + 1, :]
                    qi = (qsub * jnp.exp(bsub - ref_row)).astype(BF16)
                    ki = (kc[0:lo, :] * jnp.exp(ref_row - bc[0:lo, :])).astype(BF16)
                    att = lax.dot_general(qi, ki, nt_dims, preferred_element_type=F32)
                    acc = jnp.dot(att.astype(BF16), vcb[0:lo, :], preferred_element_type=F32)

                def diag_body(j, acc, lo=lo, bsub=bsub, qsub=qsub):
                    row = r0 + lo + j
                    bs = b_ref[pl.ds(row, 1), :]
                    ks = k32_ref[pl.ds(row, 1), :]
                    vs = v32_ref[pl.ds(row, 1), :]
                    dec = jnp.exp(jnp.where(sub_iota >= j, bsub - bs, -jnp.inf))
                    col = jnp.sum(qsub * dec * ks, axis=-1, keepdims=True)
                    return acc + col * vs

                acc = lax.fori_loop(0, GLA_SUB, diag_body, acc)
                o_ref[r0 + lo:r0 + lo + GLA_SUB, :] += acc
            kd = (kc * jnp.exp(b_last - bc)).astype(BF16)
            state_ref[...] = st * e_last + lax.dot_general(vcb, kd, tn_dims, preferred_element_type=F32)

    o = _rms(o_ref[...], ghead_ref[...])
    r = r_ref[...].astype(F32)
    b_out = o * (r * _sigmoid(r))
    m = _sigmoid(ga_ref[...].astype(F32)) * a_out + _sigmoid(gb_ref[...].astype(F32)) * b_out
    m_ref[...] = m.astype(BF16)

    @pl.when(t == pl.num_programs(2) - 1)
    def _():
        st_ref[0, 0] = state_ref[...].T


def _mix_prompt(u_all, qkvr, acode, gates, wal, bal, wpool, pscale, ghead):
    R = MIX_ROWS
    nt = SEQ // R

    def rows(b, h, t):
        return b * nt + t

    in_specs = [
        pl.BlockSpec((R, POOL_GROUP_IN), lambda b, h, t: (rows(b, h, t), h)),
        pl.BlockSpec((R, DK_HEAD), lambda b, h, t: (rows(b, h, t), h)),
        pl.BlockSpec((R, DK_HEAD), lambda b, h, t: (rows(b, h, t), GLA_HEADS + h)),
        pl.BlockSpec((R, DV_HEAD), lambda b, h, t: (rows(b, h, t), GLA_HEADS + h)),
        pl.BlockSpec((R, DV_HEAD), lambda b, h, t: (rows(b, h, t), 2 * GLA_HEADS + h)),
        pl.BlockSpec((R, LANES), lambda b, h, t: (rows(b, h, t), 0)),
        pl.BlockSpec((R, DV_HEAD), lambda b, h, t: (rows(b, h, t), h)),
        pl.BlockSpec((R, DV_HEAD), lambda b, h, t: (rows(b, h, t), GLA_HEADS + h)),
        pl.BlockSpec((LANES, DK_HEAD), lambda b, h, t: (0, h)),
        pl.BlockSpec((1, DK_HEAD), lambda b, h, t: (0, h)),
        pl.BlockSpec((1, POOL_GROUP_IN, POOL_GROUP_OUT), lambda b, h, t: (h, 0, 0)),
        pl.BlockSpec((1, POOL_GROUP_OUT), lambda b, h, t: (0, h)),
        pl.BlockSpec((1, DV_HEAD), lambda b, h, t: (0, 0)),
    ]
    out_specs = [
        pl.BlockSpec((R, DV_HEAD), lambda b, h, t: (rows(b, h, t), h)),
        pl.BlockSpec((1, 1, DK_HEAD, DV_HEAD), lambda b, h, t: (b, h, 0, 0)),
    ]
    return pl.pallas_call(
        _mixp_kernel,
        grid=(BATCH, GLA_HEADS, nt),
        in_specs=in_specs,
        out_specs=out_specs,
        out_shape=[jax.ShapeDtypeStruct((T_PROMPT, D_MODEL), BF16),
                   jax.ShapeDtypeStruct((BATCH, GLA_HEADS, DK_HEAD, DV_HEAD), F32)],
        scratch_shapes=[pltpu.VMEM((16 + R, POOL_GROUP_IN), F32),
                        pltpu.VMEM((DV_HEAD, DK_HEAD), F32),
                        pltpu.VMEM((R, DK_HEAD), F32),
                        pltpu.VMEM((R, DK_HEAD), F32),
                        pltpu.VMEM((R, DV_HEAD), F32),
                        pltpu.VMEM((R, DV_HEAD), F32)],
        compiler_params=_cparams(("arbitrary", "arbitrary", "arbitrary")),
        name="mix_prompt",
    )(u_all, qkvr, qkvr, qkvr, qkvr, acode, gates, gates, wal, bal, wpool, pscale, ghead)


def _spre_kernel(u_ref, sp_ref, ac_ref, wal_ref, bal_ref, wpool_ref, pscale_ref, a_ref, g_ref):
    u = u_ref[...]
    for gi, w in enumerate(POOL_WINDOWS):
        lo = gi * POOL_GROUP_IN
        s = u[:, lo:lo + POOL_GROUP_IN]
        for j in range(POOL_BUF - (w - 1), POOL_BUF):
            s = s + sp_ref[j, :, lo:lo + POOL_GROUP_IN]
        cnt = float(min(PAST_LEN + 1, w))
        p = s / cnt - u[:, lo:lo + POOL_GROUP_IN]
        a = jnp.dot(p.astype(BF16), wpool_ref[gi], preferred_element_type=F32)
        olo = gi * POOL_GROUP_OUT
        a_ref[:, olo:olo + POOL_GROUP_OUT] = a * pscale_ref[:, olo:olo + POOL_GROUP_OUT]
    g_ref[...] = _log_decay(ac_ref[...], wal_ref[...], bal_ref[...])


def _sample_pre(u_all, sp2, acode, wal, bal, wpool, pscale):
    blk = T_PROMPT // DEC_BATCH
    return pl.pallas_call(
        _spre_kernel,
        grid=(1,),
        in_specs=[pl.BlockSpec((DEC_BATCH, D_POOL), lambda i: (blk, 0)),
                  pl.BlockSpec((POOL_BUF, DEC_BATCH, D_POOL), lambda i: (0, 0, 0)),
                  pl.BlockSpec((DEC_BATCH, LANES), lambda i: (blk, 0)),
                  pl.BlockSpec((LANES, D_K), lambda i: (0, 0)),
                  pl.BlockSpec((1, D_K), lambda i: (0, 0)),
                  pl.BlockSpec((4, POOL_GROUP_IN, POOL_GROUP_OUT), lambda i: (0, 0, 0)),
                  pl.BlockSpec((1, D_MODEL), lambda i: (0, 0))],
        out_specs=[pl.BlockSpec((DEC_BATCH, D_MODEL), lambda i: (0, 0)),
                   pl.BlockSpec((DEC_BATCH, D_K), lambda i: (0, 0))],
        out_shape=[jax.ShapeDtypeStruct((DEC_BATCH, D_MODEL), F32),
                   jax.ShapeDtypeStruct((DEC_BATCH, D_K), F32)],
        compiler_params=_cparams(("arbitrary",)),
        name="sample_pre",
    )(u_all, sp2, acode, wal, bal, wpool, pscale)


SSTATE_SAMPLES = 4


def _sstate_kernel(gkq_ref, v_ref, r_ref, a_ref, gt_ref, s_ref, ghead_ref, so_ref, m_ref):
    for j in range(SSTATE_SAMPLES):
        x = gkq_ref[j]
        rowi = lax.broadcasted_iota(jnp.int32, x.shape, 0)
        x = jnp.where(rowi == 0, jnp.exp(x), jnp.where(rowi == 2, x * (DK_HEAD ** -0.5), x))
        xt = x.T
        v = v_ref[j]
        r = r_ref[j]
        a_out = a_ref[j]
        gates = gt_ref[j]
        for h in range(GLA_HEADS):
            cols = xt[h * DK_HEAD:(h + 1) * DK_HEAD, :]
            dec, kcol, qcol = cols[:, 0:1], cols[:, 1:2], cols[:, 2:3]
            lo = h * DV_HEAD
            vrow = v[:, lo:lo + DV_HEAD]
            s_new = dec * s_ref[j, h] + kcol * vrow
            so_ref[j, h] = s_new
            o = jnp.sum(qcol * s_new, axis=0, keepdims=True)
            o = _rms(o, ghead_ref[...])
            rr = r[:, lo:lo + DV_HEAD]
            b_out = o * (rr * _sigmoid(rr))
            m_ref[j, :, lo:lo + DV_HEAD] = (
                _sigmoid(gates[:, lo:lo + DV_HEAD]) * a_out[:, lo:lo + DV_HEAD]
                + _sigmoid(gates[:, D_MODEL + lo:D_MODEL + lo + DV_HEAD]) * b_out)


def _sample_state(gkq, v3, r3, a3, gates3, state, ghead):
    ns = SSTATE_SAMPLES

    def row3(width):
        return pl.BlockSpec((ns, 1, width), lambda i: (i, 0, 0))

    sspec = pl.BlockSpec((ns, GLA_HEADS, DK_HEAD, DV_HEAD), lambda i: (i, 0, 0, 0))
    return pl.pallas_call(
        _sstate_kernel,
        grid=(DEC_BATCH // ns,),
        in_specs=[pl.BlockSpec((ns, 8, D_K), lambda i: (i, 0, 0)),
                  row3(D_V), row3(D_V), row3(D_MODEL), row3(2 * D_MODEL), sspec,
                  pl.BlockSpec((1, DV_HEAD), lambda i: (0, 0))],
        out_specs=[sspec, row3(D_MODEL)],
        out_shape=[jax.ShapeDtypeStruct((DEC_BATCH, GLA_HEADS, DK_HEAD, DV_HEAD), F32),
                   jax.ShapeDtypeStruct((DEC_BATCH, 1, D_MODEL), F32)],
        compiler_params=_cparams(("arbitrary",)),
        name="sample_state",
    )(gkq, v3, r3, a3, gates3, state, ghead)


def _outproj_kernel(mp_ref, ms_ref, xp_ref, xs_ref, wo_ref, gt1p_ref, sh2p_ref, sc2p_ref,
                    gt1s_ref, sh2s_ref, sc2s_ref, g2_ref, wr_ref, br_ref,
                    x1_ref, h2_ref, ti_ref, tg_ref):
    i = pl.program_id(0)

    def body(nrows, m_ref, x_ref, gt1_ref, sh2_ref, sc2_ref):
        y = jnp.dot(m_ref[...], wo_ref[...], preferred_element_type=F32)
        x1 = x_ref[...] + (1.0 + _modval(gt1_ref)) * y
        x1_ref[0:nrows, :] = x1
        h2 = _rms(x1, g2_ref[...]) * (1.0 + _modval(sc2_ref)) + _modval(sh2_ref)
        h2_ref[0:nrows, :] = h2
        h_hi = h2.astype(BF16)
        h_lo = (h2 - h_hi.astype(F32)).astype(BF16)
        p_hi = jnp.dot(h_hi, wr_ref[...], preferred_element_type=F32)
        p_lo = jnp.dot(h_lo, wr_ref[:, 0:LANES], preferred_element_type=F32)
        logits = p_hi[:, 0:LANES] + (p_hi[:, LANES:] + p_lo) + br_ref[...]
        lane = lax.broadcasted_iota(jnp.int32, logits.shape, 1)
        vals, idxs = [], []
        cur = logits
        for _ in range(TOP_K):
            mx = jnp.max(cur, axis=-1, keepdims=True)
            ix = jnp.min(jnp.where(cur == mx, lane, LANES), axis=-1, keepdims=True)
            vals.append(mx)
            idxs.append(ix)
            cur = jnp.where(lane == ix, -jnp.inf, cur)
        exps = [jnp.exp(v - vals[0]) for v in vals]
        den = exps[0] + exps[1] + exps[2] + exps[3]
        ti = jnp.full(logits.shape, -1, jnp.int32)
        tg = jnp.zeros(logits.shape, F32)
        for k in range(TOP_K):
            ti = jnp.where(lane == k, idxs[k], ti)
            tg = jnp.where(lane == k, exps[k] / den, tg)
        ti_ref[0:nrows, :] = ti
        tg_ref[0:nrows, :] = tg

    @pl.when(i < ROW_TILES_P)
    def _():
        body(ROW_TILE, mp_ref, xp_ref, gt1p_ref, sh2p_ref, sc2p_ref)

    @pl.when(i == ROW_TILES_P)
    def _():
        body(DEC_BATCH, ms_ref, xs_ref, gt1s_ref, sh2s_ref, sc2s_ref)


def _outproj(m_p, m_s, x_p, x_s, wo, mod_p3, mod_s, g2, wr, br):
    return pl.pallas_call(
        _outproj_kernel,
        grid=(ROW_TILES_P + 1,),
        in_specs=[_prompt_rows_spec(D_MODEL), _sample_rows_spec(D_MODEL),
                  _prompt_rows_spec(D_MODEL), _sample_rows_spec(D_MODEL),
                  pl.BlockSpec((D_MODEL, D_MODEL), lambda i: (0, 0)),
                  _mod_spec_prompt(2), _mod_spec_prompt(3), _mod_spec_prompt(4),
                  _mod_spec_sample(2), _mod_spec_sample(3), _mod_spec_sample(4),
                  pl.BlockSpec((1, D_MODEL), lambda i: (0, 0)),
                  pl.BlockSpec((D_MODEL, 2 * LANES), lambda i: (0, 0)),
                  pl.BlockSpec((1, LANES), lambda i: (0, 0))],
        out_specs=[pl.BlockSpec((ROW_TILE, D_MODEL), lambda i: (i, 0)),
                   pl.BlockSpec((ROW_TILE, D_MODEL), lambda i: (i, 0)),
                   pl.BlockSpec((ROW_TILE, LANES), lambda i: (i, 0)),
                   pl.BlockSpec((ROW_TILE, LANES), lambda i: (i, 0))],
        out_shape=[jax.ShapeDtypeStruct((T_ALL, D_MODEL), F32),
                   jax.ShapeDtypeStruct((T_ALL, D_MODEL), F32),
                   jax.ShapeDtypeStruct((T_ALL, LANES), jnp.int32),
                   jax.ShapeDtypeStruct((T_ALL, LANES), F32)],
        compiler_params=_cparams(("arbitrary",)),
        name="outproj",
    )(m_p, m_s, x_p, x_s, wo, mod_p3, mod_p3, mod_p3, mod_s, mod_s, mod_s, g2, wr, br)


ROUTE_TILE = 640
ROUTE_TILES = T_ALL // ROUTE_TILE


def _route_kernel(ti_ref, dest_ref, cnt_ref, bstart_ref, run_ref, base_ref):
    phase = pl.program_id(0)
    t = pl.program_id(1)
    ti = ti_ref[...]
    lane = lax.broadcasted_iota(jnp.int32, (ROUTE_TILE, LANES), 1)
    onehots = [lane == ti[:, k:k + 1] for k in range(TOP_K)]
    per_tok = jnp.zeros((ROUTE_TILE, LANES), F32)
    for oh in onehots:
        per_tok = per_tok + jnp.where(oh, 1.0, 0.0)

    @pl.when((phase == 0) & (t == 0))
    def _():
        run_ref[...] = jnp.zeros_like(run_ref)

    @pl.when(phase == 0)
    def _():
        run_ref[...] += jnp.sum(per_tok, axis=0, keepdims=True)

    @pl.when((phase == 1) & (t == 0))
    def _():
        counts = run_ref[...]
        nblk = jnp.floor((counts + (MOE_SB - 1)) * (1.0 / MOE_SB))
        ri = lax.broadcasted_iota(jnp.int32, (LANES, LANES), 0)
        ci = lax.broadcasted_iota(jnp.int32, (LANES, LANES), 1)
        before = jnp.where(ri < ci, 1.0, 0.0).astype(BF16)
        nb8 = jnp.broadcast_to(nblk, (8, LANES)).astype(BF16)
        bstart = jnp.dot(nb8, before, preferred_element_type=F32)[0:1, :]
        cnt_ref[...] = counts.astype(jnp.int32)
        bstart_ref[...] = bstart.astype(jnp.int32)
        base_ref[...] = bstart * MOE_SB
        run_ref[...] = jnp.zeros_like(run_ref)

    @pl.when(phase == 1)
    def _():
        ri = lax.broadcasted_iota(jnp.int32, (ROUTE_TILE, ROUTE_TILE), 0)
        ci = lax.broadcasted_iota(jnp.int32, (ROUTE_TILE, ROUTE_TILE), 1)
        earlier = jnp.where(ri > ci, 1.0, 0.0).astype(BF16)
        prior = jnp.dot(earlier, per_tok.astype(BF16), preferred_element_type=F32)
        pos = prior + (run_ref[...] + base_ref[...])
        dest = jnp.full((ROUTE_TILE, LANES), -1, jnp.int32)
        for k, oh in enumerate(onehots):
            d = jnp.sum(jnp.where(oh, pos, 0.0), axis=-1, keepdims=True).astype(jnp.int32)
            dest = jnp.where(lane == k, d, dest)
        dest_ref[...] = dest
        run_ref[...] += jnp.sum(per_tok, axis=0, keepdims=True)


def _route(ti_all):
    return pl.pallas_call(
        _route_kernel,
        grid=(2, ROUTE_TILES),
        in_specs=[pl.BlockSpec((ROUTE_TILE, LANES), lambda p, t: (t, 0))],
        out_specs=[pl.BlockSpec((ROUTE_TILE, LANES), lambda p, t: (t * p, 0)),
                   pl.BlockSpec((1, LANES), lambda p, t: (0, 0)),
                   pl.BlockSpec((1, LANES), lambda p, t: (0, 0))],
        out_shape=[jax.ShapeDtypeStruct((T_ALL, LANES), jnp.int32),
                   jax.ShapeDtypeStruct((1, LANES), jnp.int32),
                   jax.ShapeDtypeStruct((1, LANES), jnp.int32)],
        scratch_shapes=[pltpu.VMEM((1, LANES), F32), pltpu.VMEM((1, LANES), F32)],
        compiler_params=_cparams(("arbitrary", "arbitrary")),
        name="route",
    )(ti_all)


GATHER_UNROLL = 8
DISP_TOK = 640
DISP_ROWS = DISP_TOK * TOP_K


def _dispatch_kernel(dest_ref, cnt_ref, bstart_ref, nused_ref, h2_ref, xb_hbm, zero_ref, sem, zsem):
    i = pl.program_id(0)

    def body(g, c):
        for u in range(GATHER_UNROLL):
            row = dest_ref[i * DISP_ROWS + g * GATHER_UNROLL + u]
            tok = g * (GATHER_UNROLL // TOP_K) + u // TOP_K
            pltpu.make_async_copy(h2_ref.at[pl.ds(tok, 1)], xb_hbm.at[pl.ds(row, 1)], sem).start(priority=u % 2)
        return c
    lax.fori_loop(0, DISP_ROWS // GATHER_UNROLL, body, 0)

    def pad_range(e):
        nblk = (cnt_ref[e] + MOE_SB - 1) // MOE_SB
        return bstart_ref[e] * MOE_SB + cnt_ref[e], (bstart_ref[e] + nblk) * MOE_SB

    def zero_row(row):
        return pltpu.make_async_copy(zero_ref.at[pl.ds(0, 1)], xb_hbm.at[pl.ds(row, 1)], zsem)

    def zero_block(blk):
        return pltpu.make_async_copy(zero_ref, xb_hbm.at[pl.ds(blk * MOE_SB, MOE_SB)], zsem)

    def for_each_unowned(row_fn, block_fn):
        def per_expert(e, c):
            lo, hi = pad_range(e)
            lax.fori_loop(lo, hi, lambda r, cc: (row_fn(r), cc)[1], 0)
            return c
        lax.fori_loop(0, N_EXPERTS, per_expert, 0)
        lax.fori_loop(nused_ref[0], MOE_NBLK, lambda b, cc: (block_fn(b), cc)[1], 0)

    @pl.when(i == 0)
    def _():
        zero_ref[...] = jnp.zeros_like(zero_ref)
        for_each_unowned(lambda r: zero_row(r).start(), lambda b: zero_block(b).start())

    for _ in range(TOP_K):
        pltpu.make_async_copy(h2_ref, xb_hbm.at[pl.ds(0, DISP_TOK)], sem).wait()

    @pl.when(i == 0)
    def _():
        for_each_unowned(lambda r: zero_row(r).wait(), lambda b: zero_block(b).wait())


def _dispatch(dest, counts, blk_start, nused, h2_all):
    return pl.pallas_call(
        _dispatch_kernel,
        grid_spec=pltpu.PrefetchScalarGridSpec(
            num_scalar_prefetch=4,
            grid=(T_ALL // DISP_TOK,),
            in_specs=[pl.BlockSpec((DISP_TOK, D_MODEL), lambda i, d, c, b, n: (i, 0))],
            out_specs=pl.BlockSpec(memory_space=pl.ANY),
            scratch_shapes=[pltpu.VMEM((MOE_SB, D_MODEL), F32),
                            pltpu.SemaphoreType.DMA(()),
                            pltpu.SemaphoreType.DMA(())]),
        out_shape=jax.ShapeDtypeStruct((MOE_ROWS, D_MODEL), F32),
        compiler_params=_cparams(("arbitrary",)),
        name="dispatch",
    )(dest, counts, blk_start, nused, h2_all)


def _moe_kernel(ie_ref, ib0_ref, inb_ref, ival_ref, nused_ref,
                xb_hbm, wg_ref, wu_ref, wd_ref, bg_ref, bu_ref, bd_ref, yb_hbm,
                xbuf, xstage, acc, wgb, wub, wdb, xsem, ysem):
    i = pl.program_id(0)
    f = pl.program_id(1)
    n_items = pl.num_programs(0)
    nb = inb_ref[i]

    def x_copy(blk, s):
        row0 = (ib0_ref[i] + blk) * MOE_SB
        return pltpu.make_async_copy(xb_hbm.at[pl.ds(row0, MOE_SB)], xstage.at[s], xsem.at[s])

    def y_copy(gblk, r0):
        return pltpu.make_async_copy(acc.at[pl.ds(r0, MOE_SB)], yb_hbm.at[pl.ds(gblk * MOE_SB, MOE_SB)], ysem)

    def wait_y_copies(count):
        def body(b, c):
            y_copy(0, 0).wait()
            return c
        lax.fori_loop(0, count, body, 0)

    @pl.when((f == 0) & (nb > 0))
    def _():
        x_copy(0, 0).start()

    @pl.when((f == 0) & (i > 0))
    def _():
        wait_y_copies(inb_ref[jnp.maximum(i - 1, 0)])

    bg = bg_ref[0]
    bu = bu_ref[0]
    bd = bd_ref[0]

    def run_blocks(first, final):
        def body(blk, c):
            r0 = blk * MOE_SB if isinstance(blk, int) else pl.multiple_of(blk * MOE_SB, MOE_SB)
            if first:
                s = blk % 2
                x_copy(blk, s).wait()
                x_copy(jnp.minimum(blk + 1, nb - 1), 1 - s).start()
                x = xstage[s].astype(BF16)
                xbuf[pl.ds(r0, MOE_SB), :] = x
            else:
                x = xbuf[pl.ds(r0, MOE_SB), :]
            g = jnp.dot(x, wgb[...], preferred_element_type=F32) + bg
            u = jnp.dot(x, wub[...], preferred_element_type=F32) + bu
            xg = jnp.minimum(g, SWIGLU_LIMIT)
            xl = jnp.clip(u, -SWIGLU_LIMIT, SWIGLU_LIMIT)
            act = xg * _sigmoid(SWIGLU_ALPHA * xg) * (xl + 1.0)
            part = jnp.dot(act.astype(BF16), wdb[...], preferred_element_type=F32)
            if first:
                acc[pl.ds(r0, MOE_SB), :] = part + bd
            else:
                acc[pl.ds(r0, MOE_SB), :] += part
            if final:
                y_copy(ib0_ref[i] + blk, r0).start()
            return c

        wgb[...] = wg_ref[0].astype(BF16)
        wub[...] = wu_ref[0].astype(BF16)
        wdb[...] = wd_ref[0].astype(BF16)
        body(0, 0)
        lax.fori_loop(1, nb, body, 0)

    @pl.when((f == 0) & (nb > 0))
    def _():
        run_blocks(True, False)
        x_copy(nb - 1, nb % 2).wait()

    @pl.when((f > 0) & (f < MOE_NF - 1) & (nb > 0))
    def _():
        run_blocks(False, False)

    @pl.when((f == MOE_NF - 1) & (nb > 0))
    def _():
        run_blocks(False, True)

    @pl.when((f == MOE_NF - 1) & (i == n_items - 1))
    def _():
        wait_y_copies(nb)
        acc[0:MOE_SB, :] = jnp.zeros((MOE_SB, D_MODEL), F32)

        def zstart(gblk, c):
            y_copy(gblk, 0).start()
            return c
        lax.fori_loop(nused_ref[0], MOE_NBLK, zstart, 0)
        wait_y_copies(MOE_NBLK - nused_ref[0])


def _moe(item_e, item_b0, item_nb, item_valid, nused, xb, w_gu, b_gu3, w_down, b_down3):
    def f_eff(f, ival, i):
        return jnp.where(ival[i] > 0, f, MOE_NF - 1)

    in_specs = [
        pl.BlockSpec(memory_space=pl.ANY),
        pl.BlockSpec((1, D_MODEL, MOE_TF), lambda i, f, ie, ib0, inb, iv, nu: (ie[i], 0, f_eff(f, iv, i))),
        pl.BlockSpec((1, D_MODEL, MOE_TF),
                     lambda i, f, ie, ib0, inb, iv, nu: (ie[i], 0, MOE_NF + f_eff(f, iv, i))),
        pl.BlockSpec((1, MOE_TF, D_MODEL), lambda i, f, ie, ib0, inb, iv, nu: (ie[i], f_eff(f, iv, i), 0)),
        pl.BlockSpec((1, 1, MOE_TF), lambda i, f, ie, ib0, inb, iv, nu: (ie[i], 0, f_eff(f, iv, i))),
        pl.BlockSpec((1, 1, MOE_TF),
                     lambda i, f, ie, ib0, inb, iv, nu: (ie[i], 0, MOE_NF + f_eff(f, iv, i))),
        pl.BlockSpec((1, 1, D_MODEL), lambda i, f, ie, ib0, inb, iv, nu: (ie[i], 0, 0)),
    ]
    return pl.pallas_call(
        _moe_kernel,
        grid_spec=pltpu.PrefetchScalarGridSpec(
            num_scalar_prefetch=5,
            grid=(MOE_NITEMS, MOE_NF),
            in_specs=in_specs,
            out_specs=pl.BlockSpec(memory_space=pl.ANY),
            scratch_shapes=[pltpu.VMEM((MOE_MC, D_MODEL), BF16),
                            pltpu.VMEM((2, MOE_SB, D_MODEL), F32),
                            pltpu.VMEM((MOE_MC, D_MODEL), F32),
                            pltpu.VMEM((D_MODEL, MOE_TF), BF16),
                            pltpu.VMEM((D_MODEL, MOE_TF), BF16),
                            pltpu.VMEM((MOE_TF, D_MODEL), BF16),
                            pltpu.SemaphoreType.DMA((2,)),
                            pltpu.SemaphoreType.DMA(())]),
        out_shape=jax.ShapeDtypeStruct((MOE_ROWS, D_MODEL), F32),
        compiler_params=_cparams(("arbitrary", "arbitrary")),
        name="moe",
    )(item_e, item_b0, item_nb, item_valid, nused, xb, w_gu, w_gu, w_down, b_gu3, b_gu3, b_down3)


FIN_TOK = 128
FIN_ROWS = FIN_TOK * TOP_K


FIN_TILES_P = T_PROMPT // FIN_TOK
GATHER_RING = 3


def _final_kernel(dest_ref, yb_hbm, x1_ref, tg_ref, gt2p_ref, gt2s_ref, gf_ref, yp_ref, ys_ref, stage_ref, sem):
    i = pl.program_id(0)
    n = pl.num_programs(0)

    def row_copy(tile, slot, j):
        return pltpu.make_async_copy(yb_hbm.at[pl.ds(dest_ref[tile * FIN_ROWS + j], 1)],
                                     stage_ref.at[slot, pl.ds((j % TOP_K) * FIN_TOK + j // TOP_K, 1)],
                                     sem.at[slot])

    @pl.when(i == 0)
    def _():
        for b in range(GATHER_RING - 1):
            def body(g, c, b=b):
                for u in range(GATHER_UNROLL):
                    row_copy(b, b, g * GATHER_UNROLL + u).start(priority=u % 2)
                return c
            lax.fori_loop(0, FIN_ROWS // GATHER_UNROLL, body, 0)

    def step(issue_ahead, gt2_ref, y_ref):
        slot = lax.rem(i, GATHER_RING)
        pltpu.make_async_copy(yb_hbm.at[pl.ds(0, FIN_ROWS)], stage_ref.at[slot], sem.at[slot]).wait()
        if issue_ahead:
            tile = i + GATHER_RING - 1
            nslot = lax.rem(tile, GATHER_RING)
            for j in range(FIN_ROWS):
                row_copy(tile, nslot, j).start(priority=j % 2)
        tg = tg_ref[...]
        f = jnp.zeros((FIN_TOK, D_MODEL), F32)
        for k in range(TOP_K):
            f = f + tg[:, k:k + 1] * stage_ref[slot, k * FIN_TOK:(k + 1) * FIN_TOK, :]
        y_ref[...] = _rms(x1_ref[...] + (1.0 + _modval(gt2_ref)) * f, gf_ref[...])

    @pl.when(i + GATHER_RING - 1 < n)
    def _():
        step(True, gt2p_ref, yp_ref)

    @pl.when((i + GATHER_RING - 1 >= n) & (i < FIN_TILES_P))
    def _():
        step(False, gt2p_ref, yp_ref)

    @pl.when(i == FIN_TILES_P)
    def _():
        step(False, gt2s_ref, ys_ref)


def _final(dest, yb, x1_all, tg_all, mod_p3, mod_s, gf):
    def ptile(i):
        return jnp.minimum(i, FIN_TILES_P - 1)

    return pl.pallas_call(
        _final_kernel,
        grid_spec=pltpu.PrefetchScalarGridSpec(
            num_scalar_prefetch=1,
            grid=(FIN_TILES_P + 1,),
            in_specs=[pl.BlockSpec(memory_space=pl.ANY),
                      pl.BlockSpec((FIN_TOK, D_MODEL), lambda i, d: (i, 0)),
                      pl.BlockSpec((FIN_TOK, LANES), lambda i, d: (i, 0)),
                      pl.BlockSpec((1, 1, D_MODEL), lambda i, d: (ptile(i) // (SEQ // FIN_TOK), 0, 5)),
                      pl.BlockSpec((DEC_BATCH, D_MODEL), lambda i, d: (0, 5)),
                      pl.BlockSpec((1, D_MODEL), lambda i, d: (0, 0))],
            out_specs=[pl.BlockSpec((FIN_TOK, D_MODEL), lambda i, d: (ptile(i), 0)),
                       pl.BlockSpec((DEC_BATCH, D_MODEL), lambda i, d: (0, 0))],
            scratch_shapes=[pltpu.VMEM((GATHER_RING, FIN_ROWS, D_MODEL), F32),
                            pltpu.SemaphoreType.DMA((GATHER_RING,))]),
        out_shape=[jax.ShapeDtypeStruct((T_PROMPT, D_MODEL), F32),
                   jax.ShapeDtypeStruct((DEC_BATCH, D_MODEL), F32)],
        compiler_params=_cparams(("arbitrary",)),
        name="final",
    )(dest, yb, x1_all, tg_all, mod_p3, mod_s, gf)


def _work_items(counts, blk_start):
    nblk = (counts + MOE_SB - 1) // MOE_SB
    n_items_e = (nblk + MOE_ITEM_BLOCKS - 1) // MOE_ITEM_BLOCKS
    item_end = jnp.cumsum(n_items_e)
    item_start = item_end - n_items_e
    total_items = item_end[-1]
    it = jnp.arange(MOE_NITEMS, dtype=jnp.int32)
    it_c = jnp.minimum(it, total_items - 1)
    item_e = jnp.minimum(jnp.sum((item_end[None, :] <= it_c[:, None]).astype(jnp.int32), axis=1),
                         N_EXPERTS - 1).astype(jnp.int32)
    jj = it_c - item_start[item_e]
    item_valid = (it < total_items).astype(jnp.int32)
    item_b0 = (blk_start[item_e] + jj * MOE_ITEM_BLOCKS).astype(jnp.int32)
    item_nb = jnp.where(item_valid > 0,
                        jnp.minimum(MOE_ITEM_BLOCKS, nblk[item_e] - jj * MOE_ITEM_BLOCKS), 0).astype(jnp.int32)
    nused = (blk_start[-1] + nblk[-1]).reshape(1).astype(jnp.int32)
    return item_e, item_b0, item_nb, item_valid, nused


def kernel(x_prompt, x_sample, c_prompt, c_sample, state_pool, state_gla, w_mod, b_mod, g_norm1, w_in,
           w_alpha, b_alpha, w_pool, pool_scale, g_head, w_out, g_norm2, w_router, b_router, w_gu, b_gu,
           w_down, b_down, g_final):
    x_p = x_prompt.reshape(T_PROMPT, D_MODEL)
    x_s = x_sample.reshape(DEC_BATCH, D_MODEL)

    c_all = jnp.concatenate([c_sample, c_prompt, jnp.zeros((4, D_MODEL), F32)], axis=0)
    mod = _mod(c_all, w_mod[0], b_mod[0])
    mod_p3 = mod[DEC_BATCH:DEC_BATCH + BATCH].reshape(BATCH, 1, 6 * D_MODEL)
    mod_s = mod

    h_all = _prenorm(x_p, x_s, g_norm1, mod_p3, mod_s)

    wt = jnp.swapaxes(w_in, 1, 2)[0]
    w_a = jnp.pad(wt[COL_A:COL_G], ((0, LANES - ALPHA_RANK), (0, 0)))
    u_all = _inproj(h_all, wt, COL_U, D_POOL, F32, 512, "inproj_u")
    qkvr = _inproj(h_all, wt, COL_Q, COL_A - COL_Q, BF16, 1024, "inproj_qkvr")
    acode = _inproj(h_all, w_a, 0, LANES, F32, LANES, "inproj_a")
    gates = _inproj(h_all, wt, COL_G, 2 * D_MODEL, BF16, 1024, "inproj_g")

    wal = jnp.pad(w_alpha[0], ((0, LANES - ALPHA_RANK), (0, 0))).astype(BF16)
    bal = b_alpha[0].reshape(1, D_K)
    wpool = w_pool[0].astype(BF16)
    ghead = g_head[0].reshape(1, DV_HEAD)

    m_p, new_gla_p = _mix_prompt(u_all, qkvr, acode, gates, wal, bal, wpool, pool_scale, ghead)

    sp2 = jnp.transpose(state_pool[0], (1, 0, 2))
    a_s, g_s = _sample_pre(u_all, sp2, acode, wal, bal, wpool, pool_scale)
    zs = qkvr[T_PROMPT:].astype(F32)
    q_s, k_s = zs[:, 0:D_K], zs[:, D_K:2 * D_K]
    gkq = jnp.stack([g_s, k_s, q_s] + [jnp.zeros_like(g_s)] * 5, axis=1)
    v3 = zs[:, 2 * D_K:2 * D_K + D_V].reshape(DEC_BATCH, 1, D_V)
    r3 = zs[:, 2 * D_K + D_V:].reshape(DEC_BATCH, 1, D_V)
    gates3 = gates[T_PROMPT:].astype(F32).reshape(DEC_BATCH, 1, 2 * D_MODEL)
    new_gla_s, m_s3 = _sample_state(gkq, v3, r3, a_s.reshape(DEC_BATCH, 1, D_MODEL), gates3,
                                    state_gla[0], ghead)
    m_s = m_s3.reshape(DEC_BATCH, D_MODEL).astype(BF16)

    wo = w_out[0].astype(BF16)
    wr32 = jnp.pad(w_router[0], ((0, 0), (0, LANES - N_EXPERTS)))
    wr_hi = wr32.astype(BF16)
    wr = jnp.concatenate([wr_hi, (wr32 - wr_hi.astype(F32)).astype(BF16)], axis=1)
    br = jnp.concatenate([b_router[0], jnp.full((LANES - N_EXPERTS,), -1e30, F32)]).reshape(1, LANES)
    x1_all, h2_all, ti_all, tg_all = _outproj(m_p, m_s, x_p, x_s, wo, mod_p3, mod_s, g_norm2, wr, br)

    dest_t, counts_t, bstart_t = _route(ti_all)
    dest = dest_t[:, :TOP_K].reshape(-1)
    counts, blk_start = counts_t[0, :N_EXPERTS], bstart_t[0, :N_EXPERTS]
    item_e, item_b0, item_nb, item_valid, nused = _work_items(counts, blk_start)

    xb = _dispatch(dest, counts, blk_start, nused, h2_all)
    yb = _moe(item_e, item_b0, item_nb, item_valid, nused, xb, w_gu[0],
              b_gu[0].reshape(N_EXPERTS, 1, 2 * D_FF), w_down[0], b_down[0].reshape(N_EXPERTS, 1, D_MODEL))

    y_p, y_s = _final(dest, yb, x1_all, tg_all, mod_p3, mod_s, g_final.reshape(1, D_MODEL))

    u_p = u_all[:T_PROMPT].reshape(BATCH, SEQ, D_POOL)
    new_pool_p = u_p[:, SEQ - POOL_BUF:, :][None]
    new_pool_s = jnp.concatenate([state_pool[0][:, 1:, :], u_all[T_PROMPT:][:, None, :]], axis=1)[None]
    return (y_p.reshape(BATCH, SEQ, D_MODEL), y_s.reshape(DEC_BATCH, 1, D_MODEL),
            new_pool_p, new_gla_p[None], new_pool_s, new_gla_s[None])
```

```python
import jax
import jax.numpy as jnp
from jax import lax
from jax.experimental import pallas as pl
from jax.experimental.pallas import tpu as pltpu

F32 = jnp.float32
BF16 = jnp.bfloat16

D_MODEL = 2048
BATCH = 4
SEQ = 2048
DEC_BATCH = 128
PAST_LEN = 16384
POOL_WINDOWS = (2, 4, 8, 16)
D_POOL = D_MODEL // 2
POOL_GROUP_IN = D_POOL // 4
POOL_GROUP_OUT = D_MODEL // 4
POOL_BUF = 15
GLA_HEADS = 4
D_K = D_MODEL // 2
D_V = D_MODEL
DK_HEAD = D_K // GLA_HEADS
DV_HEAD = D_V // GLA_HEADS
ALPHA_RANK = 16
GATE_TAU = 16.0
N_EXPERTS = 32
TOP_K = 4
D_FF = D_MODEL
SWIGLU_LIMIT = 7.0
SWIGLU_ALPHA = 1.702
EPS = 1e-6

LANES = 128
T_PROMPT = BATCH * SEQ
T_ALL = T_PROMPT + DEC_BATCH
N_ASSIGN = T_ALL * TOP_K

COL_U, COL_Q, COL_K, COL_V, COL_R = 0, 1024, 2048, 3072, 5120
COL_A = 7168
COL_G = COL_A + ALPHA_RANK

INPROJ_ROWS = T_ALL // 5
ROUTER_PAD_LOGIT = -1e30

GLA_CHUNK = 128
GLA_SAFE_LOG_DECAY = -60.0
GLA_SUB = 16
MIX_ROWS = 512

MOE_SB = 256
MOE_ITEM_BLOCKS = 5
MOE_MC = MOE_SB * MOE_ITEM_BLOCKS
MOE_TF = 512
MOE_NF = D_FF // MOE_TF
MOE_NBLK = -(-(N_ASSIGN + N_EXPERTS * (MOE_SB - 1)) // MOE_SB)
MOE_ROWS = MOE_NBLK * MOE_SB
MOE_NITEMS = (MOE_NBLK + N_EXPERTS * (MOE_ITEM_BLOCKS - 1)) // MOE_ITEM_BLOCKS

VMEM_LIMIT = 56 * 1024 * 1024


def _cparams(sem, vmem=VMEM_LIMIT):
    return pltpu.CompilerParams(dimension_semantics=sem, vmem_limit_bytes=vmem)


def _rms(x, g):
    return x * lax.rsqrt(jnp.mean(x * x, axis=-1, keepdims=True) + EPS) * g


def _sigmoid(x):
    return 1.0 / (1.0 + jnp.exp(-x))


def _mod_kernel(c_ref, w_ref, b_ref, o_ref):
    c = c_ref[...]
    s = (c * _sigmoid(c)).astype(BF16)
    o_ref[...] = jnp.dot(s, w_ref[...].astype(BF16), preferred_element_type=F32) + b_ref[...]


def _mod(c_all, w_mod, b_mod):
    m = c_all.shape[0]
    tn = 1024
    return pl.pallas_call(
        _mod_kernel,
        grid=(6 * D_MODEL // tn,),
        in_specs=[pl.BlockSpec((m, D_MODEL), lambda j: (0, 0)),
                  pl.BlockSpec((D_MODEL, tn), lambda j: (0, j)),
                  pl.BlockSpec((1, tn), lambda j: (0, j))],
        out_specs=pl.BlockSpec((m, tn), lambda j: (0, j)),
        out_shape=jax.ShapeDtypeStruct((m, 6 * D_MODEL), F32),
        compiler_params=_cparams(("arbitrary",)),
        name="mod",
    )(c_all, w_mod, b_mod.reshape(1, -1))


ROW_TILE = 512
ROW_TILES_P = T_PROMPT // ROW_TILE
_TILES_PER_BATCH = SEQ // ROW_TILE


def _prompt_tile(i):
    return jnp.minimum(i, ROW_TILES_P - 1)


def _prompt_rows_spec(width):
    return pl.BlockSpec((ROW_TILE, width), lambda i: (_prompt_tile(i), 0))


def _sample_rows_spec(width):
    return pl.BlockSpec((DEC_BATCH, width), lambda i: (0, 0))


def _mod_spec_prompt(chunk):
    return pl.BlockSpec((1, 1, D_MODEL), lambda i: (_prompt_tile(i) // _TILES_PER_BATCH, 0, chunk))


def _mod_spec_sample(chunk):
    return pl.BlockSpec((DEC_BATCH, D_MODEL), lambda i: (0, chunk))


def _modval(ref):
    v = ref[...]
    return v.reshape(v.shape[-2], v.shape[-1])


def _prenorm_kernel(xp_ref, xs_ref, g_ref, shp_ref, scp_ref, shs_ref, scs_ref, o_ref):
    i = pl.program_id(0)

    def body(x_ref, sh_ref, sc_ref):
        y = _rms(x_ref[...], g_ref[...])
        return (y * (1.0 + _modval(sc_ref)) + _modval(sh_ref)).astype(BF16)

    @pl.when(i < ROW_TILES_P)
    def _():
        o_ref[...] = body(xp_ref, shp_ref, scp_ref)

    @pl.when(i == ROW_TILES_P)
    def _():
        o_ref[0:DEC_BATCH, :] = body(xs_ref, shs_ref, scs_ref)


def _prenorm(x_p, x_s, g, mod_p3, mod_s):
    return pl.pallas_call(
        _prenorm_kernel,
        grid=(ROW_TILES_P + 1,),
        in_specs=[_prompt_rows_spec(D_MODEL), _sample_rows_spec(D_MODEL),
                  pl.BlockSpec((1, D_MODEL), lambda i: (0, 0)),
                  _mod_spec_prompt(0), _mod_spec_prompt(1), _mod_spec_sample(0), _mod_spec_sample(1)],
        out_specs=pl.BlockSpec((ROW_TILE, D_MODEL), lambda i: (i, 0)),
        out_shape=jax.ShapeDtypeStruct((T_ALL, D_MODEL), BF16),
        compiler_params=_cparams(("arbitrary",)),
        name="prenorm",
    )(x_p, x_s, g, mod_p3, mod_p3, mod_s, mod_s)


def _inproj_kernel(h_ref, wt_ref, o_ref, wb_ref):
    @pl.when(pl.program_id(1) == 0)
    def _():
        wb_ref[...] = wt_ref[...].astype(BF16)

    o_ref[...] = lax.dot_general(h_ref[...], wb_ref[...], (((1,), (1,)), ((), ())),
                                 preferred_element_type=F32).astype(o_ref.dtype)


def _inproj(h_all, wt, row0, nrows, out_dtype, tn, name):
    tm = INPROJ_ROWS
    if row0 % tn == 0:
        w_spec = pl.BlockSpec((tn, D_MODEL), lambda j, i: (j + row0 // tn, 0))
    else:
        assert row0 % 16 == 0
        w_spec = pl.BlockSpec((pl.Element(tn), pl.Element(D_MODEL)),
                              lambda j, i: (pl.multiple_of(row0 + j * tn, 16), 0))
    return pl.pallas_call(
        _inproj_kernel,
        grid=(nrows // tn, T_ALL // tm),
        in_specs=[pl.BlockSpec((tm, D_MODEL), lambda j, i: (i, 0)), w_spec],
        out_specs=pl.BlockSpec((tm, tn), lambda j, i: (i, j)),
        out_shape=jax.ShapeDtypeStruct((T_ALL, nrows), out_dtype),
        scratch_shapes=[pltpu.VMEM((tn, D_MODEL), BF16)],
        compiler_params=_cparams(("arbitrary", "arbitrary")),
        name=name,
    )(h_all, wt)


def _log_decay(ac, wal, bal):
    x = jnp.dot(ac.astype(BF16), wal, preferred_element_type=F32) + bal
    return (jnp.minimum(x, 0.0) - jnp.log1p(jnp.exp(-jnp.abs(x)))) / GATE_TAU


def _mixp_kernel(u_ref, q_ref, k_ref, v_ref, r_ref, ac_ref, ga_ref, gb_ref, wal_ref, bal_ref,
                 wpool_ref, pscale_ref, ghead_ref, m_ref, st_ref,
                 ext_ref, state_ref, b_ref, k32_ref, v32_ref, o_ref):
    R = MIX_ROWS
    h = pl.program_id(1)
    t = pl.program_id(2)

    @pl.when(t == 0)
    def _():
        ext_ref[0:16, :] = jnp.zeros((16, POOL_GROUP_IN), F32)
        state_ref[...] = jnp.zeros_like(state_ref)

    u = u_ref[...]
    ext_ref[16:16 + R, :] = u
    e = ext_ref[...]
    s2 = e + pltpu.roll(e, 1, axis=0)
    s4 = s2 + pltpu.roll(s2, 2, axis=0)
    s8 = s4 + pltpu.roll(s4, 4, axis=0)
    s16 = s8 + pltpu.roll(s8, 8, axis=0)
    s = jnp.where(h == 0, s2, jnp.where(h == 1, s4, jnp.where(h == 2, s8, s16)))[16:, :]
    window = jnp.left_shift(2, h)
    pos = t * R + lax.broadcasted_iota(jnp.int32, (R, 1), 0)
    cnt = jnp.minimum(pos + 1, window).astype(F32)
    p = s / cnt - u
    a_out = jnp.dot(p.astype(BF16), wpool_ref[0], preferred_element_type=F32) * pscale_ref[...]
    ext_ref[0:16, :] = ext_ref[R:R + 16, :]

    C = GLA_CHUNK
    g = _log_decay(ac_ref[...], wal_ref[...], bal_ref[...])
    ri = lax.broadcasted_iota(jnp.int32, (C, C), 0)
    ci = lax.broadcasted_iota(jnp.int32, (C, C), 1)
    causal = ri >= ci
    tri = jnp.where(causal, 1.0, 0.0).astype(BF16)
    g_hi = g.astype(BF16)
    g_lo = (g - g_hi.astype(F32)).astype(BF16)
    for c in range(R // C):
        b_ref[c * C:(c + 1) * C, :] = (
            jnp.dot(tri, g_hi[c * C:(c + 1) * C, :], preferred_element_type=F32)
            + jnp.dot(tri, g_lo[c * C:(c + 1) * C, :], preferred_element_type=F32))

    nt_dims = (((1,), (1,)), ((), ()))
    tn_dims = (((0,), (0,)), ((), ()))
    sub_iota = lax.broadcasted_iota(jnp.int32, (GLA_SUB, 1), 0)

    for c in range(R // C):
        r0 = c * C
        bc = b_ref[r0:r0 + C, :]
        qc = q_ref[r0:r0 + C, :].astype(F32) * (DK_HEAD ** -0.5)
        kc = k_ref[r0:r0 + C, :].astype(F32)
        vcb = v_ref[r0:r0 + C, :]
        st = state_ref[...]
        b_last = bc[C - 1:C, :]
        e_last = jnp.exp(b_last)
        safe = jnp.min(b_last) >= GLA_SAFE_LOG_DECAY
        qt = (qc * jnp.exp(bc)).astype(BF16)
        o_inter = lax.dot_general(qt, st.astype(BF16), nt_dims, preferred_element_type=F32)

        @pl.when(safe)
        def _(r0=r0, bc=bc, kc=kc, vcb=vcb, st=st, e_last=e_last, qt=qt, o_inter=o_inter):
            kt = kc * jnp.exp(-bc)
            att = lax.dot_general(qt, kt.astype(BF16), nt_dims, preferred_element_type=F32)
            att = jnp.where(causal, att, 0.0).astype(BF16)
            o_ref[r0:r0 + C, :] = o_inter + jnp.dot(att, vcb, preferred_element_type=F32)
            kd = (kt * e_last).astype(BF16)
            state_ref[...] = st * e_last + lax.dot_general(vcb, kd, tn_dims, preferred_element_type=F32)

        @pl.when(jnp.logical_not(safe))
        def _(r0=r0, bc=bc, qc=qc, kc=kc, vcb=vcb, st=st, b_last=b_last, e_last=e_last, o_inter=o_inter):
            o_ref[r0:r0 + C, :] = o_inter
            k32_ref[r0:r0 + C, :] = kc
            v32_ref[r0:r0 + C, :] = vcb.astype(F32)
            for i in range(C // GLA_SUB):
                lo = i * GLA_SUB
                bsub = bc[lo:lo + GLA_SUB, :]
                qsub = qc[lo:lo + GLA_SUB, :]
                acc = jnp.zeros((GLA_SUB, DV_HEAD), F32)
                if i > 0:
                    ref_row = bc[lo:lo + 1, :]
                    qi = (qsub * jnp.exp(bsub - ref_row)).astype(BF16)
                    ki = (kc[0:lo, :] * jnp.exp(ref_row - bc[0:lo, :])).astype(BF16)
                    att = lax.dot_general(qi, ki, nt_dims, preferred_element_type=F32)
                    acc = jnp.dot(att.astype(BF16), vcb[0:lo, :], preferred_element_type=F32)

                def diag_body(j, acc, lo=lo, bsub=bsub, qsub=qsub):
                    row = r0 + lo + j
                    bs = b_ref[pl.ds(row, 1), :]
                    ks = k32_ref[pl.ds(row, 1), :]
                    vs = v32_ref[pl.ds(row, 1), :]
                    dec = jnp.exp(jnp.where(sub_iota >= j, bsub - bs, -jnp.inf))
                    col = jnp.sum(qsub * dec * ks, axis=-1, keepdims=True)
                    return acc + col * vs

                acc = lax.fori_loop(0, GLA_SUB, diag_body, acc)
                o_ref[r0 + lo:r0 + lo + GLA_SUB, :] += acc
            kd = (kc * jnp.exp(b_last - bc)).astype(BF16)
            state_ref[...] = st * e_last + lax.dot_general(vcb, kd, tn_dims, preferred_element_type=F32)

    o = _rms(o_ref[...], ghead_ref[...])
    r = r_ref[...].astype(F32)
    b_out = o * (r * _sigmoid(r))
    m = _sigmoid(ga_ref[...].astype(F32)) * a_out + _sigmoid(gb_ref[...].astype(F32)) * b_out
    m_ref[...] = m.astype(BF16)

    @pl.when(t == pl.num_programs(2) - 1)
    def _():
        st_ref[0, 0] = state_ref[...].T


def _mix_prompt(u_all, qkvr, acode, gates, wal, bal, wpool, pscale, ghead):
    R = MIX_ROWS
    nt = SEQ // R

    def rows(b, h, t):
        return b * nt + t

    in_specs = [
        pl.BlockSpec((R, POOL_GROUP_IN), lambda b, h, t: (rows(b, h, t), h)),
        pl.BlockSpec((R, DK_HEAD), lambda b, h, t: (rows(b, h, t), h)),
        pl.BlockSpec((R, DK_HEAD), lambda b, h, t: (rows(b, h, t), GLA_HEADS + h)),
        pl.BlockSpec((R, DV_HEAD), lambda b, h, t: (rows(b, h, t), GLA_HEADS + h)),
        pl.BlockSpec((R, DV_HEAD), lambda b, h, t: (rows(b, h, t), 2 * GLA_HEADS + h)),
        pl.BlockSpec((R, LANES), lambda b, h, t: (rows(b, h, t), 0)),
        pl.BlockSpec((R, DV_HEAD), lambda b, h, t: (rows(b, h, t), h)),
        pl.BlockSpec((R, DV_HEAD), lambda b, h, t: (rows(b, h, t), GLA_HEADS + h)),
        pl.BlockSpec((LANES, DK_HEAD), lambda b, h, t: (0, h)),
        pl.BlockSpec((1, DK_HEAD), lambda b, h, t: (0, h)),
        pl.BlockSpec((1, POOL_GROUP_IN, POOL_GROUP_OUT), lambda b, h, t: (h, 0, 0)),
        pl.BlockSpec((1, POOL_GROUP_OUT), lambda b, h, t: (0, h)),
        pl.BlockSpec((1, DV_HEAD), lambda b, h, t: (0, 0)),
    ]
    out_specs = [
        pl.BlockSpec((R, DV_HEAD), lambda b, h, t: (rows(b, h, t), h)),
        pl.BlockSpec((1, 1, DK_HEAD, DV_HEAD), lambda b, h, t: (b, h, 0, 0)),
    ]
    return pl.pallas_call(
        _mixp_kernel,
        grid=(BATCH, GLA_HEADS, nt),
        in_specs=in_specs,
        out_specs=out_specs,
        out_shape=[jax.ShapeDtypeStruct((T_PROMPT, D_MODEL), BF16),
                   jax.ShapeDtypeStruct((BATCH, GLA_HEADS, DK_HEAD, DV_HEAD), F32)],
        scratch_shapes=[pltpu.VMEM((16 + R, POOL_GROUP_IN), F32),
                        pltpu.VMEM((DV_HEAD, DK_HEAD), F32),
                        pltpu.VMEM((R, DK_HEAD), F32),
                        pltpu.VMEM((R, DK_HEAD), F32),
                        pltpu.VMEM((R, DV_HEAD), F32),
                        pltpu.VMEM((R, DV_HEAD), F32)],
        compiler_params=_cparams(("arbitrary", "arbitrary", "arbitrary")),
        name="mix_prompt",
    )(u_all, qkvr, qkvr, qkvr, qkvr, acode, gates, gates, wal, bal, wpool, pscale, ghead)


def _spre_kernel(u_ref, sp_ref, ac_ref, wal_ref, bal_ref, wpool_ref, pscale_ref, a_ref, g_ref):
    u = u_ref[...]
    for gi, w in enumerate(POOL_WINDOWS):
        lo = gi * POOL_GROUP_IN
        s = u[:, lo:lo + POOL_GROUP_IN]
        for j in range(POOL_BUF - (w - 1), POOL_BUF):
            s = s + sp_ref[j, :, lo:lo + POOL_GROUP_IN]
        cnt = float(min(PAST_LEN + 1, w))
        p = s / cnt - u[:, lo:lo + POOL_GROUP_IN]
        a = jnp.dot(p.astype(BF16), wpool_ref[gi], preferred_element_type=F32)
        olo = gi * POOL_GROUP_OUT
        a_ref[:, olo:olo + POOL_GROUP_OUT] = a * pscale_ref[:, olo:olo + POOL_GROUP_OUT]
    g_ref[...] = _log_decay(ac_ref[...], wal_ref[...], bal_ref[...])


def _sample_pre(u_all, sp2, acode, wal, bal, wpool, pscale):
    blk = T_PROMPT // DEC_BATCH
    return pl.pallas_call(
        _spre_kernel,
        grid=(1,),
        in_specs=[pl.BlockSpec((DEC_BATCH, D_POOL), lambda i: (blk, 0)),
                  pl.BlockSpec((POOL_BUF, DEC_BATCH, D_POOL), lambda i: (0, 0, 0)),
                  pl.BlockSpec((DEC_BATCH, LANES), lambda i: (blk, 0)),
                  pl.BlockSpec((LANES, D_K), lambda i: (0, 0)),
                  pl.BlockSpec((1, D_K), lambda i: (0, 0)),
                  pl.BlockSpec((4, POOL_GROUP_IN, POOL_GROUP_OUT), lambda i: (0, 0, 0)),
                  pl.BlockSpec((1, D_MODEL), lambda i: (0, 0))],
        out_specs=[pl.BlockSpec((DEC_BATCH, D_MODEL), lambda i: (0, 0)),
                   pl.BlockSpec((DEC_BATCH, D_K), lambda i: (0, 0))],
        out_shape=[jax.ShapeDtypeStruct((DEC_BATCH, D_MODEL), F32),
                   jax.ShapeDtypeStruct((DEC_BATCH, D_K), F32)],
        compiler_params=_cparams(("arbitrary",)),
        name="sample_pre",
    )(u_all, sp2, acode, wal, bal, wpool, pscale)


SSTATE_SAMPLES = 4


def _sstate_kernel(gkq_ref, v_ref, r_ref, a_ref, gt_ref, s_ref, ghead_ref, so_ref, m_ref):
    for j in range(SSTATE_SAMPLES):
        x = gkq_ref[j]
        rowi = lax.broadcasted_iota(jnp.int32, x.shape, 0)
        x = jnp.where(rowi == 0, jnp.exp(x), jnp.where(rowi == 2, x * (DK_HEAD ** -0.5), x))
        xt = x.T
        v = v_ref[j]
        r = r_ref[j]
        a_out = a_ref[j]
        gates = gt_ref[j]
        for h in range(GLA_HEADS):
            cols = xt[h * DK_HEAD:(h + 1) * DK_HEAD, :]
            dec, kcol, qcol = cols[:, 0:1], cols[:, 1:2], cols[:, 2:3]
            lo = h * DV_HEAD
            vrow = v[:, lo:lo + DV_HEAD]
            s_new = dec * s_ref[j, h] + kcol * vrow
            so_ref[j, h] = s_new
            o = jnp.sum(qcol * s_new, axis=0, keepdims=True)
            o = _rms(o, ghead_ref[...])
            rr = r[:, lo:lo + DV_HEAD]
            b_out = o * (rr * _sigmoid(rr))
            m_ref[j, :, lo:lo + DV_HEAD] = (
                _sigmoid(gates[:, lo:lo + DV_HEAD]) * a_out[:, lo:lo + DV_HEAD]
                + _sigmoid(gates[:, D_MODEL + lo:D_MODEL + lo + DV_HEAD]) * b_out)


def _sample_state(gkq, v3, r3, a3, gates3, state, ghead):
    ns = SSTATE_SAMPLES

    def row3(width):
        return pl.BlockSpec((ns, 1, width), lambda i: (i, 0, 0))

    sspec = pl.BlockSpec((ns, GLA_HEADS, DK_HEAD, DV_HEAD), lambda i: (i, 0, 0, 0))
    return pl.pallas_call(
        _sstate_kernel,
        grid=(DEC_BATCH // ns,),
        in_specs=[pl.BlockSpec((ns, 8, D_K), lambda i: (i, 0, 0)),
                  row3(D_V), row3(D_V), row3(D_MODEL), row3(2 * D_MODEL), sspec,
                  pl.BlockSpec((1, DV_HEAD), lambda i: (0, 0))],
        out_specs=[sspec, row3(D_MODEL)],
        out_shape=[jax.ShapeDtypeStruct((DEC_BATCH, GLA_HEADS, DK_HEAD, DV_HEAD), F32),
                   jax.ShapeDtypeStruct((DEC_BATCH, 1, D_MODEL), F32)],
        compiler_params=_cparams(("arbitrary",)),
        name="sample_state",
    )(gkq, v3, r3, a3, gates3, state, ghead)


def _outproj_kernel(mp_ref, ms_ref, xp_ref, xs_ref, wo_ref, gt1p_ref, sh2p_ref, sc2p_ref,
                    gt1s_ref, sh2s_ref, sc2s_ref, g2_ref, wr_ref, br_ref,
                    x1_ref, h2_ref, ti_ref, tg_ref):
    i = pl.program_id(0)

    def body(nrows, m_ref, x_ref, gt1_ref, sh2_ref, sc2_ref):
        y = jnp.dot(m_ref[...], wo_ref[...], preferred_element_type=F32)
        x1 = x_ref[...] + (1.0 + _modval(gt1_ref)) * y
        x1_ref[0:nrows, :] = x1
        h2 = _rms(x1, g2_ref[...]) * (1.0 + _modval(sc2_ref)) + _modval(sh2_ref)
        h2_ref[0:nrows, :] = h2
        h_hi = h2.astype(BF16)
        h_lo = (h2 - h_hi.astype(F32)).astype(BF16)
        p_hi = jnp.dot(h_hi, wr_ref[...], preferred_element_type=F32)
        p_lo = jnp.dot(h_lo, wr_ref[:, 0:LANES], preferred_element_type=F32)
        logits = p_hi[:, 0:LANES] + (p_hi[:, LANES:] + p_lo) + br_ref[...]
        lane = lax.broadcasted_iota(jnp.int32, logits.shape, 1)
        vals, idxs = [], []
        cur = logits
        for _ in range(TOP_K):
            mx = jnp.max(cur, axis=-1, keepdims=True)
            ix = jnp.min(jnp.where(cur == mx, lane, LANES), axis=-1, keepdims=True)
            vals.append(mx)
            idxs.append(ix)
            cur = jnp.where(lane == ix, -jnp.inf, cur)
        exps = [jnp.exp(v - vals[0]) for v in vals]
        den = exps[0] + exps[1] + exps[2] + exps[3]
        ti = jnp.full(logits.shape, -1, jnp.int32)
        tg = jnp.zeros(logits.shape, F32)
        for k in range(TOP_K):
            ti = jnp.where(lane == k, idxs[k], ti)
            tg = jnp.where(lane == k, exps[k] / den, tg)
        ti_ref[0:nrows, :] = ti
        tg_ref[0:nrows, :] = tg

    @pl.when(i < ROW_TILES_P)
    def _():
        body(ROW_TILE, mp_ref, xp_ref, gt1p_ref, sh2p_ref, sc2p_ref)

    @pl.when(i == ROW_TILES_P)
    def _():
        body(DEC_BATCH, ms_ref, xs_ref, gt1s_ref, sh2s_ref, sc2s_ref)


def _outproj(m_p, m_s, x_p, x_s, wo, mod_p3, mod_s, g2, wr, br):
    return pl.pallas_call(
        _outproj_kernel,
        grid=(ROW_TILES_P + 1,),
        in_specs=[_prompt_rows_spec(D_MODEL), _sample_rows_spec(D_MODEL),
                  _prompt_rows_spec(D_MODEL), _sample_rows_spec(D_MODEL),
                  pl.BlockSpec((D_MODEL, D_MODEL), lambda i: (0, 0)),
                  _mod_spec_prompt(2), _mod_spec_prompt(3), _mod_spec_prompt(4),
                  _mod_spec_sample(2), _mod_spec_sample(3), _mod_spec_sample(4),
                  pl.BlockSpec((1, D_MODEL), lambda i: (0, 0)),
                  pl.BlockSpec((D_MODEL, 2 * LANES), lambda i: (0, 0)),
                  pl.BlockSpec((1, LANES), lambda i: (0, 0))],
        out_specs=[pl.BlockSpec((ROW_TILE, D_MODEL), lambda i: (i, 0)),
                   pl.BlockSpec((ROW_TILE, D_MODEL), lambda i: (i, 0)),
                   pl.BlockSpec((ROW_TILE, LANES), lambda i: (i, 0)),
                   pl.BlockSpec((ROW_TILE, LANES), lambda i: (i, 0))],
        out_shape=[jax.ShapeDtypeStruct((T_ALL, D_MODEL), F32),
                   jax.ShapeDtypeStruct((T_ALL, D_MODEL), F32),
                   jax.ShapeDtypeStruct((T_ALL, LANES), jnp.int32),
                   jax.ShapeDtypeStruct((T_ALL, LANES), F32)],
        compiler_params=_cparams(("arbitrary",)),
        name="outproj",
    )(m_p, m_s, x_p, x_s, wo, mod_p3, mod_p3, mod_p3, mod_s, mod_s, mod_s, g2, wr, br)


ROUTE_TILE = 640
ROUTE_TILES = T_ALL // ROUTE_TILE


def _route_kernel(ti_ref, dest_ref, cnt_ref, bstart_ref, run_ref, base_ref):
    phase = pl.program_id(0)
    t = pl.program_id(1)
    ti = ti_ref[...]
    lane = lax.broadcasted_iota(jnp.int32, (ROUTE_TILE, LANES), 1)
    onehots = [lane == ti[:, k:k + 1] for k in range(TOP_K)]
    per_tok = jnp.zeros((ROUTE_TILE, LANES), F32)
    for oh in onehots:
        per_tok = per_tok + jnp.where(oh, 1.0, 0.0)

    @pl.when((phase == 0) & (t == 0))
    def _():
        run_ref[...] = jnp.zeros_like(run_ref)

    @pl.when(phase == 0)
    def _():
        run_ref[...] += jnp.sum(per_tok, axis=0, keepdims=True)

    @pl.when((phase == 1) & (t == 0))
    def _():
        counts = run_ref[...]
        nblk = jnp.floor((counts + (MOE_SB - 1)) * (1.0 / MOE_SB))
        ri = lax.broadcasted_iota(jnp.int32, (LANES, LANES), 0)
        ci = lax.broadcasted_iota(jnp.int32, (LANES, LANES), 1)
        before = jnp.where(ri < ci, 1.0, 0.0).astype(BF16)
        nb8 = jnp.broadcast_to(nblk, (8, LANES)).astype(BF16)
        bstart = jnp.dot(nb8, before, preferred_element_type=F32)[0:1, :]
        cnt_ref[...] = counts.astype(jnp.int32)
        bstart_ref[...] = bstart.astype(jnp.int32)
        base_ref[...] = bstart * MOE_SB
        run_ref[...] = jnp.zeros_like(run_ref)

    @pl.when(phase == 1)
    def _():
        ri = lax.broadcasted_iota(jnp.int32, (ROUTE_TILE, ROUTE_TILE), 0)
        ci = lax.broadcasted_iota(jnp.int32, (ROUTE_TILE, ROUTE_TILE), 1)
        earlier = jnp.where(ri > ci, 1.0, 0.0).astype(BF16)
        prior = jnp.dot(earlier, per_tok.astype(BF16), preferred_element_type=F32)
        pos = prior + (run_ref[...] + base_ref[...])
        dest = jnp.full((ROUTE_TILE, LANES), -1, jnp.int32)
        for k, oh in enumerate(onehots):
            d = jnp.sum(jnp.where(oh, pos, 0.0), axis=-1, keepdims=True).astype(jnp.int32)
            dest = jnp.where(lane == k, d, dest)
        dest_ref[...] = dest
        run_ref[...] += jnp.sum(per_tok, axis=0, keepdims=True)


def _route(ti_all):
    return pl.pallas_call(
        _route_kernel,
        grid=(2, ROUTE_TILES),
        in_specs=[pl.BlockSpec((ROUTE_TILE, LANES), lambda p, t: (t, 0))],
        out_specs=[pl.BlockSpec((ROUTE_TILE, LANES), lambda p, t: (t * p, 0)),
                   pl.BlockSpec((1, LANES), lambda p, t: (0, 0)),
                   pl.BlockSpec((1, LANES), lambda p, t: (0, 0))],
        out_shape=[jax.ShapeDtypeStruct((T_ALL, LANES), jnp.int32),
                   jax.ShapeDtypeStruct((1, LANES), jnp.int32),
                   jax.ShapeDtypeStruct((1, LANES), jnp.int32)],
        scratch_shapes=[pltpu.VMEM((1, LANES), F32), pltpu.VMEM((1, LANES), F32)],
        compiler_params=_cparams(("arbitrary", "arbitrary")),
        name="route",
    )(ti_all)


GATHER_UNROLL = 8
DISP_TOK = T_ALL // 5
DISP_ROWS = DISP_TOK * TOP_K


def _dispatch_kernel(dest_ref, cnt_ref, bstart_ref, nused_ref, h2_ref, xb_hbm, zero_ref, sem, zsem):
    i = pl.program_id(0)

    def body(g, c):
        for u in range(GATHER_UNROLL):
            row = dest_ref[i * DISP_ROWS + g * GATHER_UNROLL + u]
            tok = g * (GATHER_UNROLL // TOP_K) + u // TOP_K
            pltpu.make_async_copy(h2_ref.at[pl.ds(tok, 1)], xb_hbm.at[pl.ds(row, 1)], sem).start(priority=u % 2)
        return c
    lax.fori_loop(0, DISP_ROWS // GATHER_UNROLL, body, 0)

    def pad_range(e):
        nblk = (cnt_ref[e] + MOE_SB - 1) // MOE_SB
        return bstart_ref[e] * MOE_SB + cnt_ref[e], (bstart_ref[e] + nblk) * MOE_SB

    def zero_row(row):
        return pltpu.make_async_copy(zero_ref.at[pl.ds(0, 1)], xb_hbm.at[pl.ds(row, 1)], zsem)

    def zero_block(blk):
        return pltpu.make_async_copy(zero_ref, xb_hbm.at[pl.ds(blk * MOE_SB, MOE_SB)], zsem)

    def for_each_unowned(row_fn, block_fn):
        def per_expert(e, c):
            lo, hi = pad_range(e)
            lax.fori_loop(lo, hi, lambda r, cc: (row_fn(r), cc)[1], 0)
            return c
        lax.fori_loop(0, N_EXPERTS, per_expert, 0)
        lax.fori_loop(nused_ref[0], MOE_NBLK, lambda b, cc: (block_fn(b), cc)[1], 0)

    @pl.when(i == 0)
    def _():
        zero_ref[...] = jnp.zeros_like(zero_ref)
        for_each_unowned(lambda r: zero_row(r).start(), lambda b: zero_block(b).start())

    for _ in range(TOP_K):
        pltpu.make_async_copy(h2_ref, xb_hbm.at[pl.ds(0, DISP_TOK)], sem).wait()

    @pl.when(i == 0)
    def _():
        for_each_unowned(lambda r: zero_row(r).wait(), lambda b: zero_block(b).wait())


def _dispatch(dest, counts, blk_start, nused, h2_all):
    return pl.pallas_call(
        _dispatch_kernel,
        grid_spec=pltpu.PrefetchScalarGridSpec(
            num_scalar_prefetch=4,
            grid=(T_ALL // DISP_TOK,),
            in_specs=[pl.BlockSpec((DISP_TOK, D_MODEL), lambda i, d, c, b, n: (i, 0))],
            out_specs=pl.BlockSpec(memory_space=pl.ANY),
            scratch_shapes=[pltpu.VMEM((MOE_SB, D_MODEL), F32),
                            pltpu.SemaphoreType.DMA(()),
                            pltpu.SemaphoreType.DMA(())]),
        out_shape=jax.ShapeDtypeStruct((MOE_ROWS, D_MODEL), F32),
        compiler_params=_cparams(("arbitrary",)),
        name="dispatch",
    )(dest, counts, blk_start, nused, h2_all)


def _moe_kernel(ie_ref, ib0_ref, inb_ref, ival_ref, nused_ref,
                xb_hbm, wg_ref, wu_ref, wd_ref, bg_ref, bu_ref, bd_ref, yb_hbm,
                xbuf, xstage, acc, wgb, wub, wdb, xsem, ysem):
    i = pl.program_id(0)
    f = pl.program_id(1)
    n_items = pl.num_programs(0)
    nb = inb_ref[i]

    def x_copy(blk, s):
        row0 = (ib0_ref[i] + blk) * MOE_SB
        return pltpu.make_async_copy(xb_hbm.at[pl.ds(row0, MOE_SB)], xstage.at[s], xsem.at[s])

    def y_copy(gblk, r0):
        return pltpu.make_async_copy(acc.at[pl.ds(r0, MOE_SB)], yb_hbm.at[pl.ds(gblk * MOE_SB, MOE_SB)], ysem)

    def wait_y_copies(count):
        def body(b, c):
            y_copy(0, 0).wait()
            return c
        lax.fori_loop(0, count, body, 0)

    @pl.when((f == 0) & (nb > 0))
    def _():
        x_copy(0, 0).start()

    @pl.when((f == 0) & (i > 0))
    def _():
        wait_y_copies(inb_ref[jnp.maximum(i - 1, 0)])

    bg = bg_ref[0]
    bu = bu_ref[0]
    bd = bd_ref[0]

    def run_blocks(first, final):
        def body(blk, c):
            r0 = blk * MOE_SB if isinstance(blk, int) else pl.multiple_of(blk * MOE_SB, MOE_SB)
            if first:
                s = blk % 2
                x_copy(blk, s).wait()
                x_copy(jnp.minimum(blk + 1, nb - 1), 1 - s).start()
                x = xstage[s].astype(BF16)
                xbuf[pl.ds(r0, MOE_SB), :] = x
            else:
                x = xbuf[pl.ds(r0, MOE_SB), :]
            g = jnp.dot(x, wgb[...], preferred_element_type=F32) + bg
            u = jnp.dot(x, wub[...], preferred_element_type=F32) + bu
            xg = jnp.minimum(g, SWIGLU_LIMIT)
            xl = jnp.clip(u, -SWIGLU_LIMIT, SWIGLU_LIMIT)
            act = xg * _sigmoid(SWIGLU_ALPHA * xg) * (xl + 1.0)
            part = jnp.dot(act.astype(BF16), wdb[...], preferred_element_type=F32)
            if first:
                acc[pl.ds(r0, MOE_SB), :] = part + bd
            else:
                acc[pl.ds(r0, MOE_SB), :] += part
            if final:
                y_copy(ib0_ref[i] + blk, r0).start()
            return c

        wgb[...] = wg_ref[0].astype(BF16)
        wub[...] = wu_ref[0].astype(BF16)
        wdb[...] = wd_ref[0].astype(BF16)
        body(0, 0)
        lax.fori_loop(1, nb, body, 0)

    @pl.when((f == 0) & (nb > 0))
    def _():
        run_blocks(True, False)
        x_copy(nb - 1, nb % 2).wait()

    @pl.when((f > 0) & (f < MOE_NF - 1) & (nb > 0))
    def _():
        run_blocks(False, False)

    @pl.when((f == MOE_NF - 1) & (nb > 0))
    def _():
        run_blocks(False, True)

    @pl.when((f == MOE_NF - 1) & (i == n_items - 1))
    def _():
        wait_y_copies(nb)
        acc[0:MOE_SB, :] = jnp.zeros((MOE_SB, D_MODEL), F32)

        def zstart(gblk, c):
            y_copy(gblk, 0).start()
            return c
        lax.fori_loop(nused_ref[0], MOE_NBLK, zstart, 0)
        wait_y_copies(MOE_NBLK - nused_ref[0])


def _moe(item_e, item_b0, item_nb, item_valid, nused, xb, w_gu, b_gu3, w_down, b_down3):
    def f_eff(f, ival, i):
        return jnp.where(ival[i] > 0, f, MOE_NF - 1)

    in_specs = [
        pl.BlockSpec(memory_space=pl.ANY),
        pl.BlockSpec((1, D_MODEL, MOE_TF), lambda i, f, ie, ib0, inb, iv, nu: (ie[i], 0, f_eff(f, iv, i))),
        pl.BlockSpec((1, D_MODEL, MOE_TF),
                     lambda i, f, ie, ib0, inb, iv, nu: (ie[i], 0, MOE_NF + f_eff(f, iv, i))),
        pl.BlockSpec((1, MOE_TF, D_MODEL), lambda i, f, ie, ib0, inb, iv, nu: (ie[i], f_eff(f, iv, i), 0)),
        pl.BlockSpec((1, 1, MOE_TF), lambda i, f, ie, ib0, inb, iv, nu: (ie[i], 0, f_eff(f, iv, i))),
        pl.BlockSpec((1, 1, MOE_TF),
                     lambda i, f, ie, ib0, inb, iv, nu: (ie[i], 0, MOE_NF + f_eff(f, iv, i))),
        pl.BlockSpec((1, 1, D_MODEL), lambda i, f, ie, ib0, inb, iv, nu: (ie[i], 0, 0)),
    ]
    return pl.pallas_call(
        _moe_kernel,
        grid_spec=pltpu.PrefetchScalarGridSpec(
            num_scalar_prefetch=5,
            grid=(MOE_NITEMS, MOE_NF),
            in_specs=in_specs,
            out_specs=pl.BlockSpec(memory_space=pl.ANY),
            scratch_shapes=[pltpu.VMEM((MOE_MC, D_MODEL), BF16),
                            pltpu.VMEM((2, MOE_SB, D_MODEL), F32),
                            pltpu.VMEM((MOE_MC, D_MODEL), F32),
                            pltpu.VMEM((D_MODEL, MOE_TF), BF16),
                            pltpu.VMEM((D_MODEL, MOE_TF), BF16),
                            pltpu.VMEM((MOE_TF, D_MODEL), BF16),
                            pltpu.SemaphoreType.DMA((2,)),
                            pltpu.SemaphoreType.DMA(())]),
        out_shape=jax.ShapeDtypeStruct((MOE_ROWS, D_MODEL), F32),
        compiler_params=_cparams(("arbitrary", "arbitrary")),
        name="moe",
    )(item_e, item_b0, item_nb, item_valid, nused, xb, w_gu, w_gu, w_down, b_gu3, b_gu3, b_down3)


FIN_TOK = 128
FIN_ROWS = FIN_TOK * TOP_K


FIN_TILES_P = T_PROMPT // FIN_TOK
GATHER_RING = 3


def _final_kernel(dest_ref, yb_hbm, x1_ref, tg_ref, gt2p_ref, gt2s_ref, gf_ref, yp_ref, ys_ref, stage_ref, sem):
    i = pl.program_id(0)
    n = pl.num_programs(0)

    def row_copy(tile, slot, j):
        return pltpu.make_async_copy(yb_hbm.at[pl.ds(dest_ref[tile * FIN_ROWS + j], 1)],
                                     stage_ref.at[slot, pl.ds((j % TOP_K) * FIN_TOK + j // TOP_K, 1)],
                                     sem.at[slot])

    @pl.when(i == 0)
    def _():
        for b in range(GATHER_RING - 1):
            def body(g, c, b=b):
                for u in range(GATHER_UNROLL):
                    row_copy(b, b, g * GATHER_UNROLL + u).start(priority=u % 2)
                return c
            lax.fori_loop(0, FIN_ROWS // GATHER_UNROLL, body, 0)

    def step(issue_ahead, gt2_ref, y_ref):
        slot = lax.rem(i, GATHER_RING)
        pltpu.make_async_copy(yb_hbm.at[pl.ds(0, FIN_ROWS)], stage_ref.at[slot], sem.at[slot]).wait()
        if issue_ahead:
            tile = i + GATHER_RING - 1
            nslot = lax.rem(tile, GATHER_RING)
            for j in range(FIN_ROWS):
                row_copy(tile, nslot, j).start(priority=j % 2)
        tg = tg_ref[...]
        f = jnp.zeros((FIN_TOK, D_MODEL), F32)
        for k in range(TOP_K):
            f = f + tg[:, k:k + 1] * stage_ref[slot, k * FIN_TOK:(k + 1) * FIN_TOK, :]
        y_ref[...] = _rms(x1_ref[...] + (1.0 + _modval(gt2_ref)) * f, gf_ref[...])

    @pl.when(i + GATHER_RING - 1 < n)
    def _():
        step(True, gt2p_ref, yp_ref)

    @pl.when((i + GATHER_RING - 1 >= n) & (i < FIN_TILES_P))
    def _():
        step(False, gt2p_ref, yp_ref)

    @pl.when(i == FIN_TILES_P)
    def _():
        step(False, gt2s_ref, ys_ref)


def _final(dest, yb, x1_all, tg_all, mod_p3, mod_s, gf):
    def ptile(i):
        return jnp.minimum(i, FIN_TILES_P - 1)

    return pl.pallas_call(
        _final_kernel,
        grid_spec=pltpu.PrefetchScalarGridSpec(
            num_scalar_prefetch=1,
            grid=(FIN_TILES_P + 1,),
            in_specs=[pl.BlockSpec(memory_space=pl.ANY),
                      pl.BlockSpec((FIN_TOK, D_MODEL), lambda i, d: (i, 0)),
                      pl.BlockSpec((FIN_TOK, LANES), lambda i, d: (i, 0)),
                      pl.BlockSpec((1, 1, D_MODEL), lambda i, d: (ptile(i) // (SEQ // FIN_TOK), 0, 5)),
                      pl.BlockSpec((DEC_BATCH, D_MODEL), lambda i, d: (0, 5)),
                      pl.BlockSpec((1, D_MODEL), lambda i, d: (0, 0))],
            out_specs=[pl.BlockSpec((FIN_TOK, D_MODEL), lambda i, d: (ptile(i), 0)),
                       pl.BlockSpec((DEC_BATCH, D_MODEL), lambda i, d: (0, 0))],
            scratch_shapes=[pltpu.VMEM((GATHER_RING, FIN_ROWS, D_MODEL), F32),
                            pltpu.SemaphoreType.DMA((GATHER_RING,))]),
        out_shape=[jax.ShapeDtypeStruct((T_PROMPT, D_MODEL), F32),
                   jax.ShapeDtypeStruct((DEC_BATCH, D_MODEL), F32)],
        compiler_params=_cparams(("arbitrary",)),
        name="final",
    )(dest, yb, x1_all, tg_all, mod_p3, mod_s, gf)


def _work_items(counts, blk_start):
    nblk = (counts + MOE_SB - 1) // MOE_SB
    n_items_e = (nblk + MOE_ITEM_BLOCKS - 1) // MOE_ITEM_BLOCKS
    item_end = jnp.cumsum(n_items_e)
    item_start = item_end - n_items_e
    total_items = item_end[-1]
    it = jnp.arange(MOE_NITEMS, dtype=jnp.int32)
    it_c = jnp.minimum(it, total_items - 1)
    item_e = jnp.minimum(jnp.sum((item_end[None, :] <= it_c[:, None]).astype(jnp.int32), axis=1),
                         N_EXPERTS - 1).astype(jnp.int32)
    jj = it_c - item_start[item_e]
    item_valid = (it < total_items).astype(jnp.int32)
    item_b0 = (blk_start[item_e] + jj * MOE_ITEM_BLOCKS).astype(jnp.int32)
    item_nb = jnp.where(item_valid > 0,
                        jnp.minimum(MOE_ITEM_BLOCKS, nblk[item_e] - jj * MOE_ITEM_BLOCKS), 0).astype(jnp.int32)
    nused = (blk_start[-1] + nblk[-1]).reshape(1).astype(jnp.int32)
    return item_e, item_b0, item_nb, item_valid, nused


def kernel(x_prompt, x_sample, c_prompt, c_sample, state_pool, state_gla, w_mod, b_mod, g_norm1, w_in,
           w_alpha, b_alpha, w_pool, pool_scale, g_head, w_out, g_norm2, w_router, b_router, w_gu, b_gu,
           w_down, b_down, g_final):
    x_p = x_prompt.reshape(T_PROMPT, D_MODEL)
    x_s = x_sample.reshape(DEC_BATCH, D_MODEL)

    c_all = jnp.concatenate([c_sample, c_prompt, jnp.zeros((4, D_MODEL), F32)], axis=0)
    mod = _mod(c_all, w_mod[0], b_mod[0])
    mod_p3 = mod[DEC_BATCH:DEC_BATCH + BATCH].reshape(BATCH, 1, 6 * D_MODEL)
    mod_s = mod

    h_all = _prenorm(x_p, x_s, g_norm1, mod_p3, mod_s)

    wt = jnp.swapaxes(w_in, 1, 2)[0]
    w_a = jnp.pad(wt[COL_A:COL_G], ((0, LANES - ALPHA_RANK), (0, 0)))
    u_all = _inproj(h_all, wt, COL_U, D_POOL, F32, 512, "inproj_u")
    qkvr = _inproj(h_all, wt, COL_Q, COL_A - COL_Q, BF16, 1024, "inproj_qkvr")
    acode = _inproj(h_all, w_a, 0, LANES, F32, LANES, "inproj_a")
    gates = _inproj(h_all, wt, COL_G, 2 * D_MODEL, BF16, 1024, "inproj_g")

    wal = jnp.pad(w_alpha[0], ((0, LANES - ALPHA_RANK), (0, 0))).astype(BF16)
    bal = b_alpha[0].reshape(1, D_K)
    wpool = w_pool[0].astype(BF16)
    ghead = g_head[0].reshape(1, DV_HEAD)

    m_p, new_gla_p = _mix_prompt(u_all, qkvr, acode, gates, wal, bal, wpool, pool_scale, ghead)

    sp2 = jnp.transpose(state_pool[0], (1, 0, 2))
    a_s, g_s = _sample_pre(u_all, sp2, acode, wal, bal, wpool, pool_scale)
    zs = qkvr[T_PROMPT:].astype(F32)
    q_s, k_s = zs[:, 0:D_K], zs[:, D_K:2 * D_K]
    gkq = jnp.stack([g_s, k_s, q_s] + [jnp.zeros_like(g_s)] * 5, axis=1)
    v3 = zs[:, 2 * D_K:2 * D_K + D_V].reshape(DEC_BATCH, 1, D_V)
    r3 = zs[:, 2 * D_K + D_V:].reshape(DEC_BATCH, 1, D_V)
    gates3 = gates[T_PROMPT:].astype(F32).reshape(DEC_BATCH, 1, 2 * D_MODEL)
    new_gla_s, m_s3 = _sample_state(gkq, v3, r3, a_s.reshape(DEC_BATCH, 1, D_MODEL), gates3,
                                    state_gla[0], ghead)
    m_s = m_s3.reshape(DEC_BATCH, D_MODEL).astype(BF16)

    wo = w_out[0].astype(BF16)
    wr32 = jnp.pad(w_router[0], ((0, 0), (0, LANES - N_EXPERTS)))
    wr_hi = wr32.astype(BF16)
    wr = jnp.concatenate([wr_hi, (wr32 - wr_hi.astype(F32)).astype(BF16)], axis=1)
    br = jnp.concatenate([b_router[0], jnp.full((LANES - N_EXPERTS,), ROUTER_PAD_LOGIT, F32)]).reshape(1, LANES)
    x1_all, h2_all, ti_all, tg_all = _outproj(m_p, m_s, x_p, x_s, wo, mod_p3, mod_s, g_norm2, wr, br)

    dest_t, counts_t, bstart_t = _route(ti_all)
    dest = dest_t[:, :TOP_K].reshape(-1)
    counts, blk_start = counts_t[0, :N_EXPERTS], bstart_t[0, :N_EXPERTS]
    item_e, item_b0, item_nb, item_valid, nused = _work_items(counts, blk_start)

    xb = _dispatch(dest, counts, blk_start, nused, h2_all)
    yb = _moe(item_e, item_b0, item_nb, item_valid, nused, xb, w_gu[0],
              b_gu[0].reshape(N_EXPERTS, 1, 2 * D_FF), w_down[0], b_down[0].reshape(N_EXPERTS, 1, D_MODEL))

    y_p, y_s = _final(dest, yb, x1_all, tg_all, mod_p3, mod_s, g_final.reshape(1, D_MODEL))

    u_p = u_all[:T_PROMPT].reshape(BATCH, SEQ, D_POOL)
    new_pool_p = u_p[:, SEQ - POOL_BUF:, :][None]
    new_pool_s = jnp.concatenate([state_pool[0][:, 1:, :], u_all[T_PROMPT:][:, None, :]], axis=1)[None]
    return (y_p.reshape(BATCH, SEQ, D_MODEL), y_s.reshape(DEC_BATCH, 1, D_MODEL),
            new_pool_p, new_gla_p[None], new_pool_s, new_gla_s[None])
```

```python
import jax
import jax.numpy as jnp
from jax import lax
from jax.experimental import pallas as pl
from jax.experimental.pallas import tpu as pltpu

F32 = jnp.float32
BF16 = jnp.bfloat16

D_MODEL = 2048
BATCH = 4
SEQ = 2048
DEC_BATCH = 128
PAST_LEN = 16384
POOL_WINDOWS = (2, 4, 8, 16)
D_POOL = D_MODEL // 2
POOL_GROUP_IN = D_POOL // 4
POOL_GROUP_OUT = D_MODEL // 4
POOL_BUF = 15
GLA_HEADS = 4
D_K = D_MODEL // 2
D_V = D_MODEL
DK_HEAD = D_K // GLA_HEADS
DV_HEAD = D_V // GLA_HEADS
ALPHA_RANK = 16
GATE_TAU = 16.0
N_EXPERTS = 32
TOP_K = 4
D_FF = D_MODEL
SWIGLU_LIMIT = 7.0
SWIGLU_ALPHA = 1.702
EPS = 1e-6

LANES = 128
T_PROMPT = BATCH * SEQ
T_ALL = T_PROMPT + DEC_BATCH
N_ASSIGN = T_ALL * TOP_K

COL_U, COL_Q, COL_K, COL_V, COL_R = 0, 1024, 2048, 3072, 5120
COL_A = 7168
COL_G = COL_A + ALPHA_RANK

INPROJ_ROWS = T_ALL // 5
ROUTER_PAD_LOGIT = -1e30

GLA_CHUNK = 128
GLA_SAFE_LOG_DECAY = -60.0
GLA_SUB = 16
MIX_ROWS = 512

MOE_SB = 256
MOE_ITEM_BLOCKS = 5
MOE_MC = MOE_SB * MOE_ITEM_BLOCKS
MOE_TF = 512
MOE_NF = D_FF // MOE_TF
MOE_NBLK = -(-(N_ASSIGN + N_EXPERTS * (MOE_SB - 1)) // MOE_SB)
MOE_ROWS = MOE_NBLK * MOE_SB
MOE_NITEMS = (MOE_NBLK + N_EXPERTS * (MOE_ITEM_BLOCKS - 1)) // MOE_ITEM_BLOCKS

VMEM_LIMIT = 56 * 1024 * 1024


def _cparams(sem, vmem=VMEM_LIMIT):
    return pltpu.CompilerParams(dimension_semantics=sem, vmem_limit_bytes=vmem)


def _rms(x, g):
    return x * lax.rsqrt(jnp.mean(x * x, axis=-1, keepdims=True) + EPS) * g


def _sigmoid(x):
    return 1.0 / (1.0 + jnp.exp(-x))


def _mod_kernel(c_ref, w_ref, b_ref, o_ref):
    c = c_ref[...]
    s = (c * _sigmoid(c)).astype(BF16)
    o_ref[...] = jnp.dot(s, w_ref[...].astype(BF16), preferred_element_type=F32) + b_ref[...]


def _mod(c_all, w_mod, b_mod):
    m = c_all.shape[0]
    tn = 1024
    return pl.pallas_call(
        _mod_kernel,
        grid=(6 * D_MODEL // tn,),
        in_specs=[pl.BlockSpec((m, D_MODEL), lambda j: (0, 0)),
                  pl.BlockSpec((D_MODEL, tn), lambda j: (0, j)),
                  pl.BlockSpec((1, tn), lambda j: (0, j))],
        out_specs=pl.BlockSpec((m, tn), lambda j: (0, j)),
        out_shape=jax.ShapeDtypeStruct((m, 6 * D_MODEL), F32),
        compiler_params=_cparams(("arbitrary",)),
        name="mod",
    )(c_all, w_mod, b_mod.reshape(1, -1))


ROW_TILE = 512
ROW_TILES_P = T_PROMPT // ROW_TILE
_TILES_PER_BATCH = SEQ // ROW_TILE


def _prompt_tile(i):
    return jnp.minimum(i, ROW_TILES_P - 1)


def _prompt_rows_spec(width):
    return pl.BlockSpec((ROW_TILE, width), lambda i: (_prompt_tile(i), 0))


def _sample_rows_spec(width):
    return pl.BlockSpec((DEC_BATCH, width), lambda i: (0, 0))


def _mod_spec_prompt(chunk):
    return pl.BlockSpec((1, 1, D_MODEL), lambda i: (_prompt_tile(i) // _TILES_PER_BATCH, 0, chunk))


def _mod_spec_sample(chunk):
    return pl.BlockSpec((DEC_BATCH, D_MODEL), lambda i: (0, chunk))


def _modval(ref):
    v = ref[...]
    return v.reshape(v.shape[-2], v.shape[-1])


def _prenorm_kernel(xp_ref, xs_ref, g_ref, shp_ref, scp_ref, shs_ref, scs_ref, o_ref):
    i = pl.program_id(0)

    def body(x_ref, sh_ref, sc_ref):
        y = _rms(x_ref[...], g_ref[...])
        return (y * (1.0 + _modval(sc_ref)) + _modval(sh_ref)).astype(BF16)

    @pl.when(i < ROW_TILES_P)
    def _():
        o_ref[...] = body(xp_ref, shp_ref, scp_ref)

    @pl.when(i == ROW_TILES_P)
    def _():
        o_ref[0:DEC_BATCH, :] = body(xs_ref, shs_ref, scs_ref)


def _prenorm(x_p, x_s, g, mod_p3, mod_s):
    return pl.pallas_call(
        _prenorm_kernel,
        grid=(ROW_TILES_P + 1,),
        in_specs=[_prompt_rows_spec(D_MODEL), _sample_rows_spec(D_MODEL),
                  pl.BlockSpec((1, D_MODEL), lambda i: (0, 0)),
                  _mod_spec_prompt(0), _mod_spec_prompt(1), _mod_spec_sample(0), _mod_spec_sample(1)],
        out_specs=pl.BlockSpec((ROW_TILE, D_MODEL), lambda i: (i, 0)),
        out_shape=jax.ShapeDtypeStruct((T_ALL, D_MODEL), BF16),
        compiler_params=_cparams(("arbitrary",)),
        name="prenorm",
    )(x_p, x_s, g, mod_p3, mod_p3, mod_s, mod_s)


def _inproj_kernel(h_ref, wt_ref, o_ref, wb_ref):
    @pl.when(pl.program_id(1) == 0)
    def _():
        wb_ref[...] = wt_ref[...].astype(BF16)

    o_ref[...] = lax.dot_general(h_ref[...], wb_ref[...], (((1,), (1,)), ((), ())),
                                 preferred_element_type=F32).astype(o_ref.dtype)


def _inproj(h_all, wt, row0, nrows, out_dtype, tn, name):
    tm = INPROJ_ROWS
    if row0 % tn == 0:
        w_spec = pl.BlockSpec((tn, D_MODEL), lambda j, i: (j + row0 // tn, 0))
    else:
        assert row0 % 16 == 0
        w_spec = pl.BlockSpec((pl.Element(tn), pl.Element(D_MODEL)),
                              lambda j, i: (pl.multiple_of(row0 + j * tn, 16), 0))
    return pl.pallas_call(
        _inproj_kernel,
        grid=(nrows // tn, T_ALL // tm),
        in_specs=[pl.BlockSpec((tm, D_MODEL), lambda j, i: (i, 0)), w_spec],
        out_specs=pl.BlockSpec((tm, tn), lambda j, i: (i, j)),
        out_shape=jax.ShapeDtypeStruct((T_ALL, nrows), out_dtype),
        scratch_shapes=[pltpu.VMEM((tn, D_MODEL), BF16)],
        compiler_params=_cparams(("arbitrary", "arbitrary")),
        name=name,
    )(h_all, wt)


def _log_decay(ac, wal, bal):
    x = jnp.dot(ac.astype(BF16), wal, preferred_element_type=F32) + bal
    return (jnp.minimum(x, 0.0) - jnp.log1p(jnp.exp(-jnp.abs(x)))) / GATE_TAU


def _mixp_kernel(u_ref, q_ref, k_ref, v_ref, r_ref, ac_ref, ga_ref, gb_ref, wal_ref, bal_ref,
                 wpool_ref, pscale_ref, ghead_ref, m_ref, st_ref,
                 ext_ref, state_ref, b_ref, k32_ref, v32_ref, o_ref):
    R = MIX_ROWS
    h = pl.program_id(1)
    t = pl.program_id(2)

    @pl.when(t == 0)
    def _():
        ext_ref[0:16, :] = jnp.zeros((16, POOL_GROUP_IN), F32)
        state_ref[...] = jnp.zeros_like(state_ref)

    u = u_ref[...]
    ext_ref[16:16 + R, :] = u
    e = ext_ref[...]
    s2 = e + pltpu.roll(e, 1, axis=0)
    s4 = s2 + pltpu.roll(s2, 2, axis=0)
    s8 = s4 + pltpu.roll(s4, 4, axis=0)
    s16 = s8 + pltpu.roll(s8, 8, axis=0)
    s = jnp.where(h == 0, s2, jnp.where(h == 1, s4, jnp.where(h == 2, s8, s16)))[16:, :]
    window = jnp.left_shift(2, h)
    pos = t * R + lax.broadcasted_iota(jnp.int32, (R, 1), 0)
    cnt = jnp.minimum(pos + 1, window).astype(F32)
    p = s / cnt - u
    a_out = jnp.dot(p.astype(BF16), wpool_ref[0], preferred_element_type=F32) * pscale_ref[...]
    ext_ref[0:16, :] = ext_ref[R:R + 16, :]

    C = GLA_CHUNK
    g = _log_decay(ac_ref[...], wal_ref[...], bal_ref[...])
    ri = lax.broadcasted_iota(jnp.int32, (C, C), 0)
    ci = lax.broadcasted_iota(jnp.int32, (C, C), 1)
    causal = ri >= ci
    tri = jnp.where(causal, 1.0, 0.0).astype(BF16)
    g_hi = g.astype(BF16)
    g_lo = (g - g_hi.astype(F32)).astype(BF16)
    for c in range(R // C):
        b_ref[c * C:(c + 1) * C, :] = (
            jnp.dot(tri, g_hi[c * C:(c + 1) * C, :], preferred_element_type=F32)
            + jnp.dot(tri, g_lo[c * C:(c + 1) * C, :], preferred_element_type=F32))

    nt_dims = (((1,), (1,)), ((), ()))
    tn_dims = (((0,), (0,)), ((), ()))
    sub_iota = lax.broadcasted_iota(jnp.int32, (GLA_SUB, 1), 0)

    for c in range(R // C):
        r0 = c * C
        bc = b_ref[r0:r0 + C, :]
        qc = q_ref[r0:r0 + C, :].astype(F32) * (DK_HEAD ** -0.5)
        kc = k_ref[r0:r0 + C, :].astype(F32)
        vcb = v_ref[r0:r0 + C, :]
        st = state_ref[...]
        b_last = bc[C - 1:C, :]
        e_last = jnp.exp(b_last)
        safe = jnp.min(b_last) >= GLA_SAFE_LOG_DECAY
        qt = (qc * jnp.exp(bc)).astype(BF16)
        o_inter = lax.dot_general(qt, st.astype(BF16), nt_dims, preferred_element_type=F32)

        @pl.when(safe)
        def _(r0=r0, bc=bc, kc=kc, vcb=vcb, st=st, e_last=e_last, qt=qt, o_inter=o_inter):
            kt = kc * jnp.exp(-bc)
            att = lax.dot_general(qt, kt.astype(BF16), nt_dims, preferred_element_type=F32)
            att = jnp.where(causal, att, 0.0).astype(BF16)
            o_ref[r0:r0 + C, :] = o_inter + jnp.dot(att, vcb, preferred_element_type=F32)
            kd = (kt * e_last).astype(BF16)
            state_ref[...] = st * e_last + lax.dot_general(vcb, kd, tn_dims, preferred_element_type=F32)

        @pl.when(jnp.logical_not(safe))
        def _(r0=r0, bc=bc, qc=qc, kc=kc, vcb=vcb, st=st, b_last=b_last, e_last=e_last, o_inter=o_inter):
            o_ref[r0:r0 + C, :] = o_inter
            k32_ref[r0:r0 + C, :] = kc
            v32_ref[r0:r0 + C, :] = vcb.astype(F32)
            for i in range(C // GLA_SUB):
                lo = i * GLA_SUB
                bsub = bc[lo:lo + GLA_SUB, :]
                qsub = qc[lo:lo + GLA_SUB, :]
                acc = jnp.zeros((GLA_SUB, DV_HEAD), F32)
                if i > 0:
                    ref_row = bc[lo:lo + 1, :]
                    qi = (qsub * jnp.exp(bsub - ref_row)).astype(BF16)
                    ki = (kc[0:lo, :] * jnp.exp(ref_row - bc[0:lo, :])).astype(BF16)
                    att = lax.dot_general(qi, ki, nt_dims, preferred_element_type=F32)
                    acc = jnp.dot(att.astype(BF16), vcb[0:lo, :], preferred_element_type=F32)

                def diag_body(j, acc, lo=lo, bsub=bsub, qsub=qsub):
                    row = r0 + lo + j
                    bs = b_ref[pl.ds(row, 1), :]
                    ks = k32_ref[pl.ds(row, 1), :]
                    vs = v32_ref[pl.ds(row, 1), :]
                    dec = jnp.exp(jnp.where(sub_iota >= j, bsub - bs, -jnp.inf))
                    col = jnp.sum(qsub * dec * ks, axis=-1, keepdims=True)
                    return acc + col * vs

                acc = lax.fori_loop(0, GLA_SUB, diag_body, acc)
                o_ref[r0 + lo:r0 + lo + GLA_SUB, :] += acc
            kd = (kc * jnp.exp(b_last - bc)).astype(BF16)
            state_ref[...] = st * e_last + lax.dot_general(vcb, kd, tn_dims, preferred_element_type=F32)

    o = _rms(o_ref[...], ghead_ref[...])
    r = r_ref[...].astype(F32)
    b_out = o * (r * _sigmoid(r))
    m = _sigmoid(ga_ref[...].astype(F32)) * a_out + _sigmoid(gb_ref[...].astype(F32)) * b_out
    m_ref[...] = m.astype(BF16)

    @pl.when(t == pl.num_programs(2) - 1)
    def _():
        st_ref[0, 0] = state_ref[...].T


def _mix_prompt(u_all, qkvr, acode, gates, wal, bal, wpool, pscale, ghead):
    R = MIX_ROWS
    nt = SEQ // R

    def rows(b, h, t):
        return b * nt + t

    in_specs = [
        pl.BlockSpec((R, POOL_GROUP_IN), lambda b, h, t: (rows(b, h, t), h)),
        pl.BlockSpec((R, DK_HEAD), lambda b, h, t: (rows(b, h, t), h)),
        pl.BlockSpec((R, DK_HEAD), lambda b, h, t: (rows(b, h, t), GLA_HEADS + h)),
        pl.BlockSpec((R, DV_HEAD), lambda b, h, t: (rows(b, h, t), GLA_HEADS + h)),
        pl.BlockSpec((R, DV_HEAD), lambda b, h, t: (rows(b, h, t), 2 * GLA_HEADS + h)),
        pl.BlockSpec((R, LANES), lambda b, h, t: (rows(b, h, t), 0)),
        pl.BlockSpec((R, DV_HEAD), lambda b, h, t: (rows(b, h, t), h)),
        pl.BlockSpec((R, DV_HEAD), lambda b, h, t: (rows(b, h, t), GLA_HEADS + h)),
        pl.BlockSpec((LANES, DK_HEAD), lambda b, h, t: (0, h)),
        pl.BlockSpec((1, DK_HEAD), lambda b, h, t: (0, h)),
        pl.BlockSpec((1, POOL_GROUP_IN, POOL_GROUP_OUT), lambda b, h, t: (h, 0, 0)),
        pl.BlockSpec((1, POOL_GROUP_OUT), lambda b, h, t: (0, h)),
        pl.BlockSpec((1, DV_HEAD), lambda b, h, t: (0, 0)),
    ]
    out_specs = [
        pl.BlockSpec((R, DV_HEAD), lambda b, h, t: (rows(b, h, t), h)),
        pl.BlockSpec((1, 1, DK_HEAD, DV_HEAD), lambda b, h, t: (b, h, 0, 0)),
    ]
    return pl.pallas_call(
        _mixp_kernel,
        grid=(BATCH, GLA_HEADS, nt),
        in_specs=in_specs,
        out_specs=out_specs,
        out_shape=[jax.ShapeDtypeStruct((T_PROMPT, D_MODEL), BF16),
                   jax.ShapeDtypeStruct((BATCH, GLA_HEADS, DK_HEAD, DV_HEAD), F32)],
        scratch_shapes=[pltpu.VMEM((16 + R, POOL_GROUP_IN), F32),
                        pltpu.VMEM((DV_HEAD, DK_HEAD), F32),
                        pltpu.VMEM((R, DK_HEAD), F32),
                        pltpu.VMEM((R, DK_HEAD), F32),
                        pltpu.VMEM((R, DV_HEAD), F32),
                        pltpu.VMEM((R, DV_HEAD), F32)],
        compiler_params=_cparams(("arbitrary", "arbitrary", "arbitrary")),
        name="mix_prompt",
    )(u_all, qkvr, qkvr, qkvr, qkvr, acode, gates, gates, wal, bal, wpool, pscale, ghead)


def _spre_kernel(u_ref, sp_ref, ac_ref, wal_ref, bal_ref, wpool_ref, pscale_ref, a_ref, g_ref):
    u = u_ref[...]
    for gi, w in enumerate(POOL_WINDOWS):
        lo = gi * POOL_GROUP_IN
        s = u[:, lo:lo + POOL_GROUP_IN]
        for j in range(POOL_BUF - (w - 1), POOL_BUF):
            s = s + sp_ref[j, :, lo:lo + POOL_GROUP_IN]
        cnt = float(min(PAST_LEN + 1, w))
        p = s / cnt - u[:, lo:lo + POOL_GROUP_IN]
        a = jnp.dot(p.astype(BF16), wpool_ref[gi], preferred_element_type=F32)
        olo = gi * POOL_GROUP_OUT
        a_ref[:, olo:olo + POOL_GROUP_OUT] = a * pscale_ref[:, olo:olo + POOL_GROUP_OUT]
    g_ref[...] = _log_decay(ac_ref[...], wal_ref[...], bal_ref[...])


def _sample_pre(u_all, sp2, acode, wal, bal, wpool, pscale):
    blk = T_PROMPT // DEC_BATCH
    return pl.pallas_call(
        _spre_kernel,
        grid=(1,),
        in_specs=[pl.BlockSpec((DEC_BATCH, D_POOL), lambda i: (blk, 0)),
                  pl.BlockSpec((POOL_BUF, DEC_BATCH, D_POOL), lambda i: (0, 0, 0)),
                  pl.BlockSpec((DEC_BATCH, LANES), lambda i: (blk, 0)),
                  pl.BlockSpec((LANES, D_K), lambda i: (0, 0)),
                  pl.BlockSpec((1, D_K), lambda i: (0, 0)),
                  pl.BlockSpec((4, POOL_GROUP_IN, POOL_GROUP_OUT), lambda i: (0, 0, 0)),
                  pl.BlockSpec((1, D_MODEL), lambda i: (0, 0))],
        out_specs=[pl.BlockSpec((DEC_BATCH, D_MODEL), lambda i: (0, 0)),
                   pl.BlockSpec((DEC_BATCH, D_K), lambda i: (0, 0))],
        out_shape=[jax.ShapeDtypeStruct((DEC_BATCH, D_MODEL), F32),
                   jax.ShapeDtypeStruct((DEC_BATCH, D_K), F32)],
        compiler_params=_cparams(("arbitrary",)),
        name="sample_pre",
    )(u_all, sp2, acode, wal, bal, wpool, pscale)


SSTATE_SAMPLES = 4


def _sstate_kernel(gkq_ref, v_ref, r_ref, a_ref, gt_ref, s_ref, ghead_ref, so_ref, m_ref):
    for j in range(SSTATE_SAMPLES):
        x = gkq_ref[j]
        rowi = lax.broadcasted_iota(jnp.int32, x.shape, 0)
        x = jnp.where(rowi == 0, jnp.exp(x), jnp.where(rowi == 2, x * (DK_HEAD ** -0.5), x))
        xt = x.T
        v = v_ref[j]
        r = r_ref[j]
        a_out = a_ref[j]
        gates = gt_ref[j]
        for h in range(GLA_HEADS):
            cols = xt[h * DK_HEAD:(h + 1) * DK_HEAD, :]
            dec, kcol, qcol = cols[:, 0:1], cols[:, 1:2], cols[:, 2:3]
            lo = h * DV_HEAD
            vrow = v[:, lo:lo + DV_HEAD]
            s_new = dec * s_ref[j, h] + kcol * vrow
            so_ref[j, h] = s_new
            o = jnp.sum(qcol * s_new, axis=0, keepdims=True)
            o = _rms(o, ghead_ref[...])
            rr = r[:, lo:lo + DV_HEAD]
            b_out = o * (rr * _sigmoid(rr))
            m_ref[j, :, lo:lo + DV_HEAD] = (
                _sigmoid(gates[:, lo:lo + DV_HEAD]) * a_out[:, lo:lo + DV_HEAD]
                + _sigmoid(gates[:, D_MODEL + lo:D_MODEL + lo + DV_HEAD]) * b_out)


def _sample_state(gkq, v3, r3, a3, gates3, state, ghead):
    ns = SSTATE_SAMPLES

    def row3(width):
        return pl.BlockSpec((ns, 1, width), lambda i: (i, 0, 0))

    sspec = pl.BlockSpec((ns, GLA_HEADS, DK_HEAD, DV_HEAD), lambda i: (i, 0, 0, 0))
    return pl.pallas_call(
        _sstate_kernel,
        grid=(DEC_BATCH // ns,),
        in_specs=[pl.BlockSpec((ns, 8, D_K), lambda i: (i, 0, 0)),
                  row3(D_V), row3(D_V), row3(D_MODEL), row3(2 * D_MODEL), sspec,
                  pl.BlockSpec((1, DV_HEAD), lambda i: (0, 0))],
        out_specs=[sspec, row3(D_MODEL)],
        out_shape=[jax.ShapeDtypeStruct((DEC_BATCH, GLA_HEADS, DK_HEAD, DV_HEAD), F32),
                   jax.ShapeDtypeStruct((DEC_BATCH, 1, D_MODEL), F32)],
        compiler_params=_cparams(("arbitrary",)),
        name="sample_state",
    )(gkq, v3, r3, a3, gates3, state, ghead)


def _outproj_kernel(mp_ref, ms_ref, xp_ref, xs_ref, wo_ref, gt1p_ref, sh2p_ref, sc2p_ref,
                    gt1s_ref, sh2s_ref, sc2s_ref, g2_ref, wr_ref, br_ref,
                    x1_ref, h2_ref, ti_ref, tg_ref):
    i = pl.program_id(0)

    def body(nrows, m_ref, x_ref, gt1_ref, sh2_ref, sc2_ref):
        y = jnp.dot(m_ref[...], wo_ref[...], preferred_element_type=F32)
        x1 = x_ref[...] + (1.0 + _modval(gt1_ref)) * y
        x1_ref[0:nrows, :] = x1
        h2 = _rms(x1, g2_ref[...]) * (1.0 + _modval(sc2_ref)) + _modval(sh2_ref)
        h2_ref[0:nrows, :] = h2
        h_hi = h2.astype(BF16)
        h_lo = (h2 - h_hi.astype(F32)).astype(BF16)
        p_hi = jnp.dot(h_hi, wr_ref[...], preferred_element_type=F32)
        p_lo = jnp.dot(h_lo, wr_ref[:, 0:LANES], preferred_element_type=F32)
        logits = p_hi[:, 0:LANES] + (p_hi[:, LANES:] + p_lo) + br_ref[...]
        lane = lax.broadcasted_iota(jnp.int32, logits.shape, 1)
        vals, idxs = [], []
        cur = logits
        for _ in range(TOP_K):
            mx = jnp.max(cur, axis=-1, keepdims=True)
            ix = jnp.min(jnp.where(cur == mx, lane, LANES), axis=-1, keepdims=True)
            vals.append(mx)
            idxs.append(ix)
            cur = jnp.where(lane == ix, -jnp.inf, cur)
        exps = [jnp.exp(v - vals[0]) for v in vals]
        den = exps[0] + exps[1] + exps[2] + exps[3]
        ti = jnp.full(logits.shape, -1, jnp.int32)
        tg = jnp.zeros(logits.shape, F32)
        for k in range(TOP_K):
            ti = jnp.where(lane == k, idxs[k], ti)
            tg = jnp.where(lane == k, exps[k] / den, tg)
        ti_ref[0:nrows, :] = ti
        tg_ref[0:nrows, :] = tg

    @pl.when(i < ROW_TILES_P)
    def _():
        body(ROW_TILE, mp_ref, xp_ref, gt1p_ref, sh2p_ref, sc2p_ref)

    @pl.when(i == ROW_TILES_P)
    def _():
        body(DEC_BATCH, ms_ref, xs_ref, gt1s_ref, sh2s_ref, sc2s_ref)


def _outproj(m_p, m_s, x_p, x_s, wo, mod_p3, mod_s, g2, wr, br):
    return pl.pallas_call(
        _outproj_kernel,
        grid=(ROW_TILES_P + 1,),
        in_specs=[_prompt_rows_spec(D_MODEL), _sample_rows_spec(D_MODEL),
                  _prompt_rows_spec(D_MODEL), _sample_rows_spec(D_MODEL),
                  pl.BlockSpec((D_MODEL, D_MODEL), lambda i: (0, 0)),
                  _mod_spec_prompt(2), _mod_spec_prompt(3), _mod_spec_prompt(4),
                  _mod_spec_sample(2), _mod_spec_sample(3), _mod_spec_sample(4),
                  pl.BlockSpec((1, D_MODEL), lambda i: (0, 0)),
                  pl.BlockSpec((D_MODEL, 2 * LANES), lambda i: (0, 0)),
                  pl.BlockSpec((1, LANES), lambda i: (0, 0))],
        out_specs=[pl.BlockSpec((ROW_TILE, D_MODEL), lambda i: (i, 0)),
                   pl.BlockSpec((ROW_TILE, D_MODEL), lambda i: (i, 0)),
                   pl.BlockSpec((ROW_TILE, LANES), lambda i: (i, 0)),
                   pl.BlockSpec((ROW_TILE, LANES), lambda i: (i, 0))],
        out_shape=[jax.ShapeDtypeStruct((T_ALL, D_MODEL), F32),
                   jax.ShapeDtypeStruct((T_ALL, D_MODEL), F32),
                   jax.ShapeDtypeStruct((T_ALL, LANES), jnp.int32),
                   jax.ShapeDtypeStruct((T_ALL, LANES), F32)],
        compiler_params=_cparams(("arbitrary",)),
        name="outproj",
    )(m_p, m_s, x_p, x_s, wo, mod_p3, mod_p3, mod_p3, mod_s, mod_s, mod_s, g2, wr, br)


ROUTE_TILE = 640
ROUTE_TILES = T_ALL // ROUTE_TILE


def _route_kernel(ti_ref, dest_ref, cnt_ref, bstart_ref, run_ref, base_ref):
    phase = pl.program_id(0)
    t = pl.program_id(1)
    ti = ti_ref[...]
    lane = lax.broadcasted_iota(jnp.int32, (ROUTE_TILE, LANES), 1)
    onehots = [lane == ti[:, k:k + 1] for k in range(TOP_K)]
    per_tok = jnp.zeros((ROUTE_TILE, LANES), F32)
    for oh in onehots:
        per_tok = per_tok + jnp.where(oh, 1.0, 0.0)

    @pl.when((phase == 0) & (t == 0))
    def _():
        run_ref[...] = jnp.zeros_like(run_ref)

    @pl.when(phase == 0)
    def _():
        run_ref[...] += jnp.sum(per_tok, axis=0, keepdims=True)

    @pl.when((phase == 1) & (t == 0))
    def _():
        counts = run_ref[...]
        nblk = jnp.floor((counts + (MOE_SB - 1)) * (1.0 / MOE_SB))
        ri = lax.broadcasted_iota(jnp.int32, (LANES, LANES), 0)
        ci = lax.broadcasted_iota(jnp.int32, (LANES, LANES), 1)
        before = jnp.where(ri < ci, 1.0, 0.0).astype(BF16)
        nb8 = jnp.broadcast_to(nblk, (8, LANES)).astype(BF16)
        bstart = jnp.dot(nb8, before, preferred_element_type=F32)[0:1, :]
        cnt_ref[...] = counts.astype(jnp.int32)
        bstart_ref[...] = bstart.astype(jnp.int32)
        base_ref[...] = bstart * MOE_SB
        run_ref[...] = jnp.zeros_like(run_ref)

    @pl.when(phase == 1)
    def _():
        ri = lax.broadcasted_iota(jnp.int32, (ROUTE_TILE, ROUTE_TILE), 0)
        ci = lax.broadcasted_iota(jnp.int32, (ROUTE_TILE, ROUTE_TILE), 1)
        earlier = jnp.where(ri > ci, 1.0, 0.0).astype(BF16)
        prior = jnp.dot(earlier, per_tok.astype(BF16), preferred_element_type=F32)
        pos = prior + (run_ref[...] + base_ref[...])
        dest = jnp.full((ROUTE_TILE, LANES), -1, jnp.int32)
        for k, oh in enumerate(onehots):
            d = jnp.sum(jnp.where(oh, pos, 0.0), axis=-1, keepdims=True).astype(jnp.int32)
            dest = jnp.where(lane == k, d, dest)
        dest_ref[...] = dest
        run_ref[...] += jnp.sum(per_tok, axis=0, keepdims=True)


def _route(ti_all):
    return pl.pallas_call(
        _route_kernel,
        grid=(2, ROUTE_TILES),
        in_specs=[pl.BlockSpec((ROUTE_TILE, LANES), lambda p, t: (t, 0))],
        out_specs=[pl.BlockSpec((ROUTE_TILE, LANES), lambda p, t: (t * p, 0)),
                   pl.BlockSpec((1, LANES), lambda p, t: (0, 0)),
                   pl.BlockSpec((1, LANES), lambda p, t: (0, 0))],
        out_shape=[jax.ShapeDtypeStruct((T_ALL, LANES), jnp.int32),
                   jax.ShapeDtypeStruct((1, LANES), jnp.int32),
                   jax.ShapeDtypeStruct((1, LANES), jnp.int32)],
        scratch_shapes=[pltpu.VMEM((1, LANES), F32), pltpu.VMEM((1, LANES), F32)],
        compiler_params=_cparams(("arbitrary", "arbitrary")),
        name="route",
    )(ti_all)


GATHER_UNROLL = 32
DISP_TOK = T_ALL // 5
DISP_ROWS = DISP_TOK * TOP_K


def _dispatch_kernel(dest_ref, cnt_ref, bstart_ref, nused_ref, h2_ref, xb_hbm, zero_ref, sem, zsem):
    i = pl.program_id(0)

    def body(g, c):
        for u in range(GATHER_UNROLL):
            row = dest_ref[i * DISP_ROWS + g * GATHER_UNROLL + u]
            tok = g * (GATHER_UNROLL // TOP_K) + u // TOP_K
            pltpu.make_async_copy(h2_ref.at[pl.ds(tok, 1)], xb_hbm.at[pl.ds(row, 1)], sem).start(priority=u % 2)
        return c
    lax.fori_loop(0, DISP_ROWS // GATHER_UNROLL, body, 0)

    def pad_range(e):
        nblk = (cnt_ref[e] + MOE_SB - 1) // MOE_SB
        return bstart_ref[e] * MOE_SB + cnt_ref[e], (bstart_ref[e] + nblk) * MOE_SB

    def zero_row(row):
        return pltpu.make_async_copy(zero_ref.at[pl.ds(0, 1)], xb_hbm.at[pl.ds(row, 1)], zsem)

    def zero_block(blk):
        return pltpu.make_async_copy(zero_ref, xb_hbm.at[pl.ds(blk * MOE_SB, MOE_SB)], zsem)

    def for_each_unowned(row_fn, block_fn):
        def per_expert(e, c):
            lo, hi = pad_range(e)
            lax.fori_loop(lo, hi, lambda r, cc: (row_fn(r), cc)[1], 0)
            return c
        lax.fori_loop(0, N_EXPERTS, per_expert, 0)
        lax.fori_loop(nused_ref[0], MOE_NBLK, lambda b, cc: (block_fn(b), cc)[1], 0)

    @pl.when(i == 0)
    def _():
        zero_ref[...] = jnp.zeros_like(zero_ref)
        for_each_unowned(lambda r: zero_row(r).start(), lambda b: zero_block(b).start())

    for _ in range(TOP_K):
        pltpu.make_async_copy(h2_ref, xb_hbm.at[pl.ds(0, DISP_TOK)], sem).wait()

    @pl.when(i == 0)
    def _():
        for_each_unowned(lambda r: zero_row(r).wait(), lambda b: zero_block(b).wait())


def _dispatch(dest, counts, blk_start, nused, h2_all):
    return pl.pallas_call(
        _dispatch_kernel,
        grid_spec=pltpu.PrefetchScalarGridSpec(
            num_scalar_prefetch=4,
            grid=(T_ALL // DISP_TOK,),
            in_specs=[pl.BlockSpec((DISP_TOK, D_MODEL), lambda i, d, c, b, n: (i, 0))],
            out_specs=pl.BlockSpec(memory_space=pl.ANY),
            scratch_shapes=[pltpu.VMEM((MOE_SB, D_MODEL), F32),
                            pltpu.SemaphoreType.DMA(()),
                            pltpu.SemaphoreType.DMA(())]),
        out_shape=jax.ShapeDtypeStruct((MOE_ROWS, D_MODEL), F32),
        compiler_params=_cparams(("arbitrary",)),
        name="dispatch",
    )(dest, counts, blk_start, nused, h2_all)


def _moe_kernel(ie_ref, ib0_ref, inb_ref, ival_ref, nused_ref,
                xb_hbm, wg_ref, wu_ref, wd_ref, bg_ref, bu_ref, bd_ref, yb_hbm,
                xbuf, xstage, acc, wgb, wub, wdb, xsem, ysem):
    i = pl.program_id(0)
    f = pl.program_id(1)
    n_items = pl.num_programs(0)
    nb = inb_ref[i]

    def x_copy(blk, s):
        row0 = (ib0_ref[i] + blk) * MOE_SB
        return pltpu.make_async_copy(xb_hbm.at[pl.ds(row0, MOE_SB)], xstage.at[s], xsem.at[s])

    def y_copy(gblk, r0):
        return pltpu.make_async_copy(acc.at[pl.ds(r0, MOE_SB)], yb_hbm.at[pl.ds(gblk * MOE_SB, MOE_SB)], ysem)

    def wait_y_copies(count):
        def body(b, c):
            y_copy(0, 0).wait()
            return c
        lax.fori_loop(0, count, body, 0)

    @pl.when((f == 0) & (nb > 0))
    def _():
        x_copy(0, 0).start()

    @pl.when((f == 0) & (i > 0))
    def _():
        wait_y_copies(inb_ref[jnp.maximum(i - 1, 0)])

    bg = bg_ref[0]
    bu = bu_ref[0]
    bd = bd_ref[0]

    def run_blocks(first, final):
        def body(blk, c):
            r0 = blk * MOE_SB if isinstance(blk, int) else pl.multiple_of(blk * MOE_SB, MOE_SB)
            if first:
                s = blk % 2
                x_copy(blk, s).wait()
                x_copy(jnp.minimum(blk + 1, nb - 1), 1 - s).start()
                x = xstage[s].astype(BF16)
                xbuf[pl.ds(r0, MOE_SB), :] = x
            else:
                x = xbuf[pl.ds(r0, MOE_SB), :]
            g = jnp.dot(x, wgb[...], preferred_element_type=F32) + bg
            u = jnp.dot(x, wub[...], preferred_element_type=F32) + bu
            xg = jnp.minimum(g, SWIGLU_LIMIT)
            xl = jnp.clip(u, -SWIGLU_LIMIT, SWIGLU_LIMIT)
            act = xg * _sigmoid(SWIGLU_ALPHA * xg) * (xl + 1.0)
            part = jnp.dot(act.astype(BF16), wdb[...], preferred_element_type=F32)
            if first:
                acc[pl.ds(r0, MOE_SB), :] = part + bd
            else:
                acc[pl.ds(r0, MOE_SB), :] += part
            if final:
                y_copy(ib0_ref[i] + blk, r0).start()
            return c

        wgb[...] = wg_ref[0].astype(BF16)
        wub[...] = wu_ref[0].astype(BF16)
        wdb[...] = wd_ref[0].astype(BF16)
        body(0, 0)
        lax.fori_loop(1, nb, body, 0)

    @pl.when((f == 0) & (nb > 0))
    def _():
        run_blocks(True, False)
        x_copy(nb - 1, nb % 2).wait()

    @pl.when((f > 0) & (f < MOE_NF - 1) & (nb > 0))
    def _():
        run_blocks(False, False)

    @pl.when((f == MOE_NF - 1) & (nb > 0))
    def _():
        run_blocks(False, True)

    @pl.when((f == MOE_NF - 1) & (i == n_items - 1))
    def _():
        wait_y_copies(nb)
        acc[0:MOE_SB, :] = jnp.zeros((MOE_SB, D_MODEL), F32)

        def zstart(gblk, c):
            y_copy(gblk, 0).start()
            return c
        lax.fori_loop(nused_ref[0], MOE_NBLK, zstart, 0)
        wait_y_copies(MOE_NBLK - nused_ref[0])


def _moe(item_e, item_b0, item_nb, item_valid, nused, xb, w_gu, b_gu3, w_down, b_down3):
    def f_eff(f, ival, i):
        return jnp.where(ival[i] > 0, f, MOE_NF - 1)

    in_specs = [
        pl.BlockSpec(memory_space=pl.ANY),
        pl.BlockSpec((1, D_MODEL, MOE_TF), lambda i, f, ie, ib0, inb, iv, nu: (ie[i], 0, f_eff(f, iv, i))),
        pl.BlockSpec((1, D_MODEL, MOE_TF),
                     lambda i, f, ie, ib0, inb, iv, nu: (ie[i], 0, MOE_NF + f_eff(f, iv, i))),
        pl.BlockSpec((1, MOE_TF, D_MODEL), lambda i, f, ie, ib0, inb, iv, nu: (ie[i], f_eff(f, iv, i), 0)),
        pl.BlockSpec((1, 1, MOE_TF), lambda i, f, ie, ib0, inb, iv, nu: (ie[i], 0, f_eff(f, iv, i))),
        pl.BlockSpec((1, 1, MOE_TF),
                     lambda i, f, ie, ib0, inb, iv, nu: (ie[i], 0, MOE_NF + f_eff(f, iv, i))),
        pl.BlockSpec((1, 1, D_MODEL), lambda i, f, ie, ib0, inb, iv, nu: (ie[i], 0, 0)),
    ]
    return pl.pallas_call(
        _moe_kernel,
        grid_spec=pltpu.PrefetchScalarGridSpec(
            num_scalar_prefetch=5,
            grid=(MOE_NITEMS, MOE_NF),
            in_specs=in_specs,
            out_specs=pl.BlockSpec(memory_space=pl.ANY),
            scratch_shapes=[pltpu.VMEM((MOE_MC, D_MODEL), BF16),
                            pltpu.VMEM((2, MOE_SB, D_MODEL), F32),
                            pltpu.VMEM((MOE_MC, D_MODEL), F32),
                            pltpu.VMEM((D_MODEL, MOE_TF), BF16),
                            pltpu.VMEM((D_MODEL, MOE_TF), BF16),
                            pltpu.VMEM((MOE_TF, D_MODEL), BF16),
                            pltpu.SemaphoreType.DMA((2,)),
                            pltpu.SemaphoreType.DMA(())]),
        out_shape=jax.ShapeDtypeStruct((MOE_ROWS, D_MODEL), F32),
        compiler_params=_cparams(("arbitrary", "arbitrary")),
        name="moe",
    )(item_e, item_b0, item_nb, item_valid, nused, xb, w_gu, w_gu, w_down, b_gu3, b_gu3, b_down3)


FIN_TOK = 128
FIN_ROWS = FIN_TOK * TOP_K


FIN_TILES_P = T_PROMPT // FIN_TOK
GATHER_RING = 3


def _final_kernel(dest_ref, yb_hbm, x1_ref, tg_ref, gt2p_ref, gt2s_ref, gf_ref, yp_ref, ys_ref, stage_ref, sem):
    i = pl.program_id(0)
    n = pl.num_programs(0)

    def row_copy(tile, slot, j):
        return pltpu.make_async_copy(yb_hbm.at[pl.ds(dest_ref[tile * FIN_ROWS + j], 1)],
                                     stage_ref.at[slot, pl.ds((j % TOP_K) * FIN_TOK + j // TOP_K, 1)],
                                     sem.at[slot])

    @pl.when(i == 0)
    def _():
        for b in range(GATHER_RING - 1):
            def body(g, c, b=b):
                for u in range(GATHER_UNROLL):
                    row_copy(b, b, g * GATHER_UNROLL + u).start(priority=u % 2)
                return c
            lax.fori_loop(0, FIN_ROWS // GATHER_UNROLL, body, 0)

    def step(issue_ahead, gt2_ref, y_ref):
        slot = lax.rem(i, GATHER_RING)
        pltpu.make_async_copy(yb_hbm.at[pl.ds(0, FIN_ROWS)], stage_ref.at[slot], sem.at[slot]).wait()
        if issue_ahead:
            tile = i + GATHER_RING - 1
            nslot = lax.rem(tile, GATHER_RING)
            for j in range(FIN_ROWS):
                row_copy(tile, nslot, j).start(priority=j % 2)
        tg = tg_ref[...]
        f = jnp.zeros((FIN_TOK, D_MODEL), F32)
        for k in range(TOP_K):
            f = f + tg[:, k:k + 1] * stage_ref[slot, k * FIN_TOK:(k + 1) * FIN_TOK, :]
        y_ref[...] = _rms(x1_ref[...] + (1.0 + _modval(gt2_ref)) * f, gf_ref[...])

    @pl.when(i + GATHER_RING - 1 < n)
    def _():
        step(True, gt2p_ref, yp_ref)

    @pl.when((i + GATHER_RING - 1 >= n) & (i < FIN_TILES_P))
    def _():
        step(False, gt2p_ref, yp_ref)

    @pl.when(i == FIN_TILES_P)
    def _():
        step(False, gt2s_ref, ys_ref)


def _final(dest, yb, x1_all, tg_all, mod_p3, mod_s, gf):
    def ptile(i):
        return jnp.minimum(i, FIN_TILES_P - 1)

    return pl.pallas_call(
        _final_kernel,
        grid_spec=pltpu.PrefetchScalarGridSpec(
            num_scalar_prefetch=1,
            grid=(FIN_TILES_P + 1,),
            in_specs=[pl.BlockSpec(memory_space=pl.ANY),
                      pl.BlockSpec((FIN_TOK, D_MODEL), lambda i, d: (i, 0)),
                      pl.BlockSpec((FIN_TOK, LANES), lambda i, d: (i, 0)),
                      pl.BlockSpec((1, 1, D_MODEL), lambda i, d: (ptile(i) // (SEQ // FIN_TOK), 0, 5)),
                      pl.BlockSpec((DEC_BATCH, D_MODEL), lambda i, d: (0, 5)),
                      pl.BlockSpec((1, D_MODEL), lambda i, d: (0, 0))],
            out_specs=[pl.BlockSpec((FIN_TOK, D_MODEL), lambda i, d: (ptile(i), 0)),
                       pl.BlockSpec((DEC_BATCH, D_MODEL), lambda i, d: (0, 0))],
            scratch_shapes=[pltpu.VMEM((GATHER_RING, FIN_ROWS, D_MODEL), F32),
                            pltpu.SemaphoreType.DMA((GATHER_RING,))]),
        out_shape=[jax.ShapeDtypeStruct((T_PROMPT, D_MODEL), F32),
                   jax.ShapeDtypeStruct((DEC_BATCH, D_MODEL), F32)],
        compiler_params=_cparams(("arbitrary",)),
        name="final",
    )(dest, yb, x1_all, tg_all, mod_p3, mod_s, gf)


def _work_items(counts, blk_start):
    nblk = (counts + MOE_SB - 1) // MOE_SB
    n_items_e = (nblk + MOE_ITEM_BLOCKS - 1) // MOE_ITEM_BLOCKS
    item_end = jnp.cumsum(n_items_e)
    item_start = item_end - n_items_e
    total_items = item_end[-1]
    it = jnp.arange(MOE_NITEMS, dtype=jnp.int32)
    it_c = jnp.minimum(it, total_items - 1)
    item_e = jnp.minimum(jnp.sum((item_end[None, :] <= it_c[:, None]).astype(jnp.int32), axis=1),
                         N_EXPERTS - 1).astype(jnp.int32)
    jj = it_c - item_start[item_e]
    item_valid = (it < total_items).astype(jnp.int32)
    item_b0 = (blk_start[item_e] + jj * MOE_ITEM_BLOCKS).astype(jnp.int32)
    item_nb = jnp.where(item_valid > 0,
                        jnp.minimum(MOE_ITEM_BLOCKS, nblk[item_e] - jj * MOE_ITEM_BLOCKS), 0).astype(jnp.int32)
    nused = (blk_start[-1] + nblk[-1]).reshape(1).astype(jnp.int32)
    return item_e, item_b0, item_nb, item_valid, nused


def kernel(x_prompt, x_sample, c_prompt, c_sample, state_pool, state_gla, w_mod, b_mod, g_norm1, w_in,
           w_alpha, b_alpha, w_pool, pool_scale, g_head, w_out, g_norm2, w_router, b_router, w_gu, b_gu,
           w_down, b_down, g_final):
    x_p = x_prompt.reshape(T_PROMPT, D_MODEL)
    x_s = x_sample.reshape(DEC_BATCH, D_MODEL)

    c_all = jnp.concatenate([c_sample, c_prompt, jnp.zeros((4, D_MODEL), F32)], axis=0)
    mod = _mod(c_all, w_mod[0], b_mod[0])
    mod_p3 = mod[DEC_BATCH:DEC_BATCH + BATCH].reshape(BATCH, 1, 6 * D_MODEL)
    mod_s = mod

    h_all = _prenorm(x_p, x_s, g_norm1, mod_p3, mod_s)

    wt = jnp.swapaxes(w_in, 1, 2)[0]
    w_a = jnp.pad(wt[COL_A:COL_G], ((0, LANES - ALPHA_RANK), (0, 0)))
    u_all = _inproj(h_all, wt, COL_U, D_POOL, F32, 512, "inproj_u")
    qkvr = _inproj(h_all, wt, COL_Q, COL_A - COL_Q, BF16, 1024, "inproj_qkvr")
    acode = _inproj(h_all, w_a, 0, LANES, F32, LANES, "inproj_a")
    gates = _inproj(h_all, wt, COL_G, 2 * D_MODEL, BF16, 1024, "inproj_g")

    wal = jnp.pad(w_alpha[0], ((0, LANES - ALPHA_RANK), (0, 0))).astype(BF16)
    bal = b_alpha[0].reshape(1, D_K)
    wpool = w_pool[0].astype(BF16)
    ghead = g_head[0].reshape(1, DV_HEAD)

    m_p, new_gla_p = _mix_prompt(u_all, qkvr, acode, gates, wal, bal, wpool, pool_scale, ghead)

    sp2 = jnp.transpose(state_pool[0], (1, 0, 2))
    a_s, g_s = _sample_pre(u_all, sp2, acode, wal, bal, wpool, pool_scale)
    zs = qkvr[T_PROMPT:].astype(F32)
    q_s, k_s = zs[:, 0:D_K], zs[:, D_K:2 * D_K]
    gkq = jnp.stack([g_s, k_s, q_s] + [jnp.zeros_like(g_s)] * 5, axis=1)
    v3 = zs[:, 2 * D_K:2 * D_K + D_V].reshape(DEC_BATCH, 1, D_V)
    r3 = zs[:, 2 * D_K + D_V:].reshape(DEC_BATCH, 1, D_V)
    gates3 = gates[T_PROMPT:].astype(F32).reshape(DEC_BATCH, 1, 2 * D_MODEL)
    new_gla_s, m_s3 = _sample_state(gkq, v3, r3, a_s.reshape(DEC_BATCH, 1, D_MODEL), gates3,
                                    state_gla[0], ghead)
    m_s = m_s3.reshape(DEC_BATCH, D_MODEL).astype(BF16)

    wo = w_out[0].astype(BF16)
    wr32 = jnp.pad(w_router[0], ((0, 0), (0, LANES - N_EXPERTS)))
    wr_hi = wr32.astype(BF16)
    wr = jnp.concatenate([wr_hi, (wr32 - wr_hi.astype(F32)).astype(BF16)], axis=1)
    br = jnp.concatenate([b_router[0], jnp.full((LANES - N_EXPERTS,), ROUTER_PAD_LOGIT, F32)]).reshape(1, LANES)
    x1_all, h2_all, ti_all, tg_all = _outproj(m_p, m_s, x_p, x_s, wo, mod_p3, mod_s, g_norm2, wr, br)

    dest_t, counts_t, bstart_t = _route(ti_all)
    dest = dest_t[:, :TOP_K].reshape(-1)
    counts, blk_start = counts_t[0, :N_EXPERTS], bstart_t[0, :N_EXPERTS]
    item_e, item_b0, item_nb, item_valid, nused = _work_items(counts, blk_start)

    xb = _dispatch(dest, counts, blk_start, nused, h2_all)
    yb = _moe(item_e, item_b0, item_nb, item_valid, nused, xb, w_gu[0],
              b_gu[0].reshape(N_EXPERTS, 1, 2 * D_FF), w_down[0], b_down[0].reshape(N_EXPERTS, 1, D_MODEL))

    y_p, y_s = _final(dest, yb, x1_all, tg_all, mod_p3, mod_s, g_final.reshape(1, D_MODEL))

    u_p = u_all[:T_PROMPT].reshape(BATCH, SEQ, D_POOL)
    new_pool_p = u_p[:, SEQ - POOL_BUF:, :][None]
    new_pool_s = jnp.concatenate([state_pool[0][:, 1:, :], u_all[T_PROMPT:][:, None, :]], axis=1)[None]
    return (y_p.reshape(BATCH, SEQ, D_MODEL), y_s.reshape(DEC_BATCH, 1, D_MODEL),
            new_pool_p, new_gla_p[None], new_pool_s, new_gla_s[None])
```
